```python
import jax
import jax.numpy as jnp
from jax import lax
import numpy as np

D_MODEL = 2048
BATCH = 2
SEQ = 4096
DEPTH = 4
DEC_BATCH = 8
DEC_SEQ = 8
PAST_LEN = 16384
PAGE_SIZE = 128

PLE_DIM = 256
HEAD_DIM = 64
A_WIDTH = 512
A_CONV_LEN = 31
B_WIDTH = 512
B_HEADS = 8
CHUNK = 128
C_WIDTH = 448
C_CONV_LEN = 3
ATTN_GROUPS = ((128, 1), (512, 4), (2048, 16))
HEADS_PER_GROUP = 3
WIN_KEYS = 128
D_HEADS = HEADS_PER_GROUP * len(ATTN_GROUPS)
D_WIDTH = D_HEADS * HEAD_DIM
MIX_WIDTH = A_WIDTH + B_WIDTH + C_WIDTH + D_WIDTH
IN_COLS = 2 * A_WIDTH + 2 * B_WIDTH + 3 * C_WIDTH + 3 * D_WIDTH
ATTN_SCALE = HEAD_DIM ** -0.5
MOE_GROUPS = 4
EXPERTS_PER_GROUP = 4
N_EXPERTS = MOE_GROUPS * EXPERTS_PER_GROUP
EXPERT_FF = 512
TOP_K = 2
RMS_EPS = 1e-6
LN_EPS = 1e-5

kernel_name = 'hybrid_conv_chunkmlp_dilated_attn_hmoe_step'


def _rmsnorm(x, g):
    xf = x.astype(jnp.float32)
    y = xf * lax.rsqrt(jnp.mean(xf * xf, axis=-1, keepdims=True) + RMS_EPS)
    return (y * g.astype(jnp.float32)).astype(x.dtype)


def _layernorm(x, g, b):
    xf = x.astype(jnp.float32)
    mu = jnp.mean(xf, axis=-1, keepdims=True)
    xc = xf - mu
    var = jnp.mean(xc * xc, axis=-1, keepdims=True)
    y = xc * lax.rsqrt(var + LN_EPS) * g.astype(jnp.float32) + b.astype(jnp.float32)
    return y.astype(x.dtype)


def _causal_depthwise(x, hist, w):
    xx = jnp.concatenate([hist.astype(x.dtype), x], axis=1)
    y = lax.conv_general_dilated(xx, w[:, None, :].astype(x.dtype), window_strides=(1,), padding='VALID',
                                 dimension_numbers=('NWC', 'WIO', 'NWC'), feature_group_count=x.shape[-1])
    return y, xx[:, xx.shape[1] - (w.shape[0] - 1):]


def _chunk_spatial_gate(u, v, w_s, b_s):
    n, t, _ = v.shape
    t_pad = -(-t // CHUNK) * CHUNK
    vc = jnp.pad(v, ((0, 0), (0, t_pad - t), (0, 0)))
    vc = vc.reshape(n, t_pad // CHUNK, CHUNK, B_HEADS, B_WIDTH // B_HEADS)
    w_causal = jnp.where(jnp.tril(jnp.ones((CHUNK, CHUNK), bool))[None], w_s, 0.0).astype(v.dtype)
    mixed = jnp.einsum('hts,ncshe->ncthe', w_causal, vc) + b_s.T.astype(v.dtype)[None, None, :, :, None]
    mixed = mixed.reshape(n, t_pad, B_WIDTH)[:, :t]
    return u * mixed


def _softmax_stats(scores):
    m = jnp.max(scores, axis=-1, keepdims=True)
    ex = jnp.exp(scores - m)
    den = jnp.sum(ex, axis=-1, keepdims=True)
    return ex / den, (m + jnp.log(den))[..., 0]


def _dilated_band_attention(q, k, v, dil):
    n, s, h, e = q.shape
    span = dil * WIN_KEYS
    s_pad = -(-s // span) * span
    n_blk = s_pad // span
    sub_len = s_pad // dil

    def to_blocks(t):
        t = jnp.pad(t, ((0, 0), (0, s_pad - s), (0, 0), (0, 0)))
        t = t.reshape(n, sub_len, dil, h, e).transpose(0, 2, 1, 3, 4)
        return t.reshape(n, dil, n_blk, WIN_KEYS, h, e)

    def with_prev(t):
        prev = jnp.concatenate([jnp.zeros_like(t[:, :, :1]), t[:, :, :-1]], axis=2)
        return jnp.concatenate([prev, t], axis=3)

    qb = to_blocks(q)
    kb = with_prev(to_blocks(k))
    vb = with_prev(to_blocks(v))
    qi = WIN_KEYS + jnp.arange(WIN_KEYS)[:, None]
    kj = jnp.arange(2 * WIN_KEYS)[None, :]
    dist = qi - kj
    band = (dist >= 0) & (dist <= WIN_KEYS)
    key_sub = jnp.arange(n_blk)[:, None, None] * WIN_KEYS + kj[None] - WIN_KEYS
    mask = band[None] & (key_sub >= 0)
    scores = jnp.einsum('bdcqhe,bdckhe->bdchqk', qb, kb).astype(jnp.float32) * ATTN_SCALE
    scores = jnp.where(mask[None, None, :, None], scores, -jnp.inf)
    probs, lse = _softmax_stats(scores)
    out = jnp.einsum('bdchqk,bdckhe->bdcqhe', probs.astype(v.dtype), vb)
    out = out.reshape(n, dil, sub_len, h, e).transpose(0, 2, 1, 3, 4).reshape(n, s_pad, h, e)[:, :s]
    lse = lse.transpose(0, 1, 2, 4, 3).reshape(n, dil, sub_len, h).transpose(0, 2, 1, 3).reshape(n, s_pad, h)[:, :s]
    return out, lse


def _dilated_window_step(q, k, v, kv_buf, dil):
    t = q.shape[1]
    buf_len = kv_buf.shape[1]
    kv_all = jnp.concatenate([kv_buf, jnp.stack([k, v], axis=2).astype(kv_buf.dtype)], axis=1)
    idx = buf_len + jnp.arange(t)[:, None] - dil * jnp.arange(WIN_KEYS + 1)[None, :]
    valid = idx >= 0
    kv_g = kv_all[:, jnp.clip(idx, 0)].astype(q.dtype)
    scores = jnp.einsum('nthe,ntkhe->nthk', q, kv_g[:, :, :, 0]).astype(jnp.float32) * ATTN_SCALE
    scores = jnp.where(valid[None, :, None, :], scores, -jnp.inf)
    probs, lse = _softmax_stats(scores)
    out = jnp.einsum('nthk,ntkhe->nthe', probs.astype(q.dtype), kv_g[:, :, :, 1])
    return out, lse, kv_all[:, t:]


def _token_mixers(a, w_in, conv_a_w, conv_a_b, ln_a_g, ln_a_b, ln_b_g, ln_b_b, sgu_w, sgu_b,
                  conv_c_w, g_q, g_k, hist_a, hist_c, kv_bufs):
    n, t, _ = a.shape
    z = a @ w_in
    s1 = 2 * A_WIDTH
    s2 = s1 + 2 * B_WIDTH
    s3 = s2 + 3 * C_WIDTH
    z_a, z_b, z_c, z_d = jnp.split(z, [s1, s2, s3], axis=-1)
    a_val, a_gate = jnp.split(z_a, 2, axis=-1)
    a_conv, new_hist_a = _causal_depthwise(a_val * jax.nn.sigmoid(a_gate), hist_a, conv_a_w)
    o_a = jax.nn.silu(_layernorm(a_conv + conv_a_b.astype(a_conv.dtype), ln_a_g, ln_a_b))
    u_b, v_b = jnp.split(jax.nn.gelu(z_b, approximate=False), 2, axis=-1)
    v_b = _layernorm(v_b, ln_b_g, ln_b_b)
    o_b = _chunk_spatial_gate(u_b, v_b, sgu_w, sgu_b)
    g_b, g_c, x_c = jnp.split(z_c, 3, axis=-1)
    c_conv, new_hist_c = _causal_depthwise(g_c * x_c, hist_c, conv_c_w)
    o_c = g_b * c_conv
    q, k, v = [r.reshape(n, t, D_HEADS, HEAD_DIM) for r in jnp.split(z_d, 3, axis=-1)]
    q = _rmsnorm(q, g_q)
    k = _rmsnorm(k, g_k)
    outs, lses, new_kv = [], [], []
    for g, (window, dil) in enumerate(ATTN_GROUPS):
        hs = slice(g * HEADS_PER_GROUP, (g + 1) * HEADS_PER_GROUP)
        q_g, k_g, v_g = q[:, :, hs], k[:, :, hs], v[:, :, hs]
        if kv_bufs is None:
            o_g, lse_g = _dilated_band_attention(q_g, k_g, v_g, dil)
            keep = min(window, t)
            new_kv.append(jnp.stack([k_g, v_g], axis=2)[:, t - keep:])
        else:
            o_g, lse_g, buf = _dilated_window_step(q_g, k_g, v_g, kv_bufs[g], dil)
            new_kv.append(buf)
        outs.append(o_g)
        lses.append(lse_g)
    alpha = jax.nn.softmax(jnp.stack(lses, axis=0), axis=0)
    o_d = jnp.concatenate([o_g * alpha[g][..., None].astype(o_g.dtype) for g, o_g in enumerate(outs)], axis=2)
    o_d = o_d.reshape(n, t, D_WIDTH)
    mix = jnp.concatenate([o_a, o_b, o_c, o_d], axis=-1)
    return mix, new_hist_a, v_b, new_hist_c, new_kv


def _hier_moe(x, w_rg, b_rg, w_re, b_re, w_gate, w_up, w_down):
    n_tok = x.shape[0]
    grp_logits = (x @ w_rg).astype(jnp.float32) + b_rg.astype(jnp.float32)
    grp_idx = jnp.argmax(grp_logits, axis=-1)
    grp_gate = jnp.max(jax.nn.softmax(grp_logits, axis=-1), axis=-1)
    exp_all = jnp.einsum('nd,gde->nge', x, w_re).astype(jnp.float32) + b_re.astype(jnp.float32)
    exp_logits = exp_all[jnp.arange(n_tok), grp_idx]
    top_val, top_idx = lax.top_k(exp_logits, TOP_K)
    top_w = jax.nn.softmax(top_val, axis=-1) * grp_gate[:, None]
    expert_id = grp_idx[:, None] * EXPERTS_PER_GROUP + top_idx
    combine = jnp.sum(jax.nn.one_hot(expert_id, N_EXPERTS, dtype=jnp.float32) * top_w[..., None], axis=1)
    hidden = jax.nn.silu(jnp.einsum('nd,edf->nef', x, w_gate)) * jnp.einsum('nd,edf->nef', x, w_up)
    hidden = hidden * combine.astype(x.dtype)[:, :, None]
    return jnp.einsum('nef,efd->nd', hidden, w_down)


def _trunk(x, p, hist_a, hist_c, kv_bufs, weights):
    (g_mix, w_in, conv_a_w, conv_a_b, ln_a_g, ln_a_b, ln_b_g, ln_b_b, sgu_w, sgu_b, conv_c_w, g_q, g_k,
     w_out, g_ffn, w_router_grp, b_router_grp, w_router_exp, b_router_exp, w_gate, w_up, w_down,
     g_ple, w_ple_gate, w_ple_proj) = weights
    n, t, d = x.shape
    h = x
    st_a, st_b, st_c = [], [], []
    st_kv = [[] for _ in ATTN_GROUPS]
    for i in range(DEPTH):
        ha = jnp.zeros((n, A_CONV_LEN - 1, A_WIDTH), x.dtype) if hist_a is None else hist_a[i]
        hc = jnp.zeros((n, C_CONV_LEN - 1, C_WIDTH), x.dtype) if hist_c is None else hist_c[i]
        bufs = None if kv_bufs is None else tuple(b[i] for b in kv_bufs)
        a = _rmsnorm(h, g_mix[i])
        mix, new_a, v_rows, new_c, new_kv = _token_mixers(
            a, w_in[i], conv_a_w[i], conv_a_b[i], ln_a_g[i], ln_a_b[i], ln_b_g[i], ln_b_b[i],
            sgu_w[i], sgu_b[i], conv_c_w[i], g_q[i], g_k[i], ha, hc, bufs)
        h = h + mix @ w_out[i]
        m = _rmsnorm(h, g_ffn[i]).reshape(n * t, d)
        h = h + _hier_moe(m, w_router_grp[i], b_router_grp[i], w_router_exp[i], b_router_exp[i],
                          w_gate[i], w_up[i], w_down[i]).reshape(n, t, d)
        gate = jax.nn.sigmoid(_rmsnorm(h, g_ple[i]) @ w_ple_gate[i])
        h = h + gate * (p[i] @ w_ple_proj[i])
        st_a.append(new_a)
        st_c.append(new_c)
        if kv_bufs is not None:
            st_b.append(v_rows)
        for g in range(len(ATTN_GROUPS)):
            st_kv[g].append(new_kv[g])
    chunk_v = jnp.stack(st_b) if st_b else None
    return h, jnp.stack(st_a), chunk_v, jnp.stack(st_c), tuple(jnp.stack(s) for s in st_kv)


def setup_inputs(seed: int = 0) -> dict:
    key = jax.random.key(seed)
    keys = iter(jax.random.split(key, 48))

    def nrm(shape, scale):
        return scale * jax.random.normal(next(keys), shape, jnp.float32)

    def gain(shape):
        return 1.0 + nrm(shape, 0.02)

    buf_len = [min(w, PAST_LEN) for w, _ in ATTN_GROUPS]

    def kv_shape(length):
        return (DEPTH, DEC_BATCH, length, 2, HEADS_PER_GROUP, HEAD_DIM)

    return {
        'x_prompt': nrm((BATCH, SEQ, D_MODEL), 1.0),
        'x_sample': nrm((DEC_BATCH, DEC_SEQ, D_MODEL), 1.0),
        'p_prompt': nrm((DEPTH, BATCH, SEQ, PLE_DIM), 1.0),
        'p_sample': nrm((DEPTH, DEC_BATCH, DEC_SEQ, PLE_DIM), 1.0),
        'state_conv_a': nrm((DEPTH, DEC_BATCH, A_CONV_LEN - 1, A_WIDTH), 0.5),
        'state_conv_c': nrm((DEPTH, DEC_BATCH, C_CONV_LEN - 1, C_WIDTH), 0.5),
        'cache_kv_w128': nrm(kv_shape(buf_len[0]), 1.0),
        'cache_kv_w512': nrm(kv_shape(buf_len[1]), 1.0),
        'cache_kv_w2048': nrm(kv_shape(buf_len[2]), 1.0),
        'g_mix': gain((DEPTH, D_MODEL)),
        'w_in': nrm((DEPTH, D_MODEL, IN_COLS), D_MODEL ** -0.5),
        'conv_a_w': nrm((DEPTH, A_CONV_LEN, A_WIDTH), A_CONV_LEN ** -0.5),
        'conv_a_b': nrm((DEPTH, A_WIDTH), 0.02),
        'ln_a_g': gain((DEPTH, A_WIDTH)),
        'ln_a_b': nrm((DEPTH, A_WIDTH), 0.02),
        'ln_b_g': gain((DEPTH, B_WIDTH)),
        'ln_b_b': nrm((DEPTH, B_WIDTH), 0.02),
        'sgu_w': nrm((DEPTH, B_HEADS, CHUNK, CHUNK), CHUNK ** -0.5),
        'sgu_b': 1.0 + nrm((DEPTH, B_HEADS, CHUNK), 0.1),
        'conv_c_w': nrm((DEPTH, C_CONV_LEN, C_WIDTH), C_CONV_LEN ** -0.5),
        'g_q': gain((DEPTH, HEAD_DIM)),
        'g_k': gain((DEPTH, HEAD_DIM)),
        'w_out': nrm((DEPTH, MIX_WIDTH, D_MODEL), MIX_WIDTH ** -0.5),
        'g_ffn': gain((DEPTH, D_MODEL)),
        'w_router_grp': nrm((DEPTH, D_MODEL, MOE_GROUPS), D_MODEL ** -0.5),
        'b_router_grp': nrm((DEPTH, MOE_GROUPS), 0.01),
        'w_router_exp': nrm((DEPTH, MOE_GROUPS, D_MODEL, EXPERTS_PER_GROUP), D_MODEL ** -0.5),
        'b_router_exp': nrm((DEPTH, MOE_GROUPS, EXPERTS_PER_GROUP), 0.01),
        'w_gate': nrm((DEPTH, N_EXPERTS, D_MODEL, EXPERT_FF), D_MODEL ** -0.5),
        'w_up': nrm((DEPTH, N_EXPERTS, D_MODEL, EXPERT_FF), D_MODEL ** -0.5),
        'w_down': nrm((DEPTH, N_EXPERTS, EXPERT_FF, D_MODEL), EXPERT_FF ** -0.5),
        'g_ple': gain((DEPTH, D_MODEL)),
        'w_ple_gate': nrm((DEPTH, D_MODEL, D_MODEL), D_MODEL ** -0.5),
        'w_ple_proj': nrm((DEPTH, PLE_DIM, D_MODEL), PLE_DIM ** -0.5),
    }


def reference(x_prompt, x_sample, p_prompt, p_sample, state_conv_a, state_conv_c,
              cache_kv_w128, cache_kv_w512, cache_kv_w2048, g_mix, w_in, conv_a_w, conv_a_b,
              ln_a_g, ln_a_b, ln_b_g, ln_b_b, sgu_w, sgu_b, conv_c_w, g_q, g_k, w_out, g_ffn,
              w_router_grp, b_router_grp, w_router_exp, b_router_exp, w_gate, w_up, w_down,
              g_ple, w_ple_gate, w_ple_proj):
    weights = (g_mix, w_in, conv_a_w, conv_a_b, ln_a_g, ln_a_b, ln_b_g, ln_b_b, sgu_w, sgu_b, conv_c_w,
               g_q, g_k, w_out, g_ffn, w_router_grp, b_router_grp, w_router_exp, b_router_exp,
               w_gate, w_up, w_down, g_ple, w_ple_gate, w_ple_proj)
    y_prompt, conv_a_prompt, _, conv_c_prompt, kv_prompt = _trunk(
        x_prompt, p_prompt, None, None, None, weights)
    y_sample, conv_a_sample, chunk_v_sample, conv_c_sample, kv_sample = _trunk(
        x_sample, p_sample, state_conv_a, state_conv_c, (cache_kv_w128, cache_kv_w512, cache_kv_w2048), weights)
    kv_w128_prompt, kv_w512_prompt, kv_w2048_prompt = kv_prompt
    kv_w128_sample, kv_w512_sample, kv_w2048_sample = kv_sample
    return (y_prompt, y_sample, conv_a_prompt, conv_a_sample, conv_c_prompt, conv_c_sample, chunk_v_sample,
            kv_w128_prompt, kv_w128_sample, kv_w512_prompt, kv_w512_sample, kv_w2048_prompt, kv_w2048_sample)
```

```python
import functools

import jax
import jax.numpy as jnp
from jax import lax
from jax.experimental import pallas as pl
from jax.experimental.pallas import tpu as pltpu

F32 = jnp.float32
BF16 = jnp.bfloat16

D_MODEL = 2048
DEPTH = 4
PLE_DIM = 256
HEAD_DIM = 64
A_WIDTH = 512
A_CONV_LEN = 31
B_WIDTH = 512
B_HEADS = 8
CHUNK = 128
C_WIDTH = 448
C_CONV_LEN = 3
ATTN_GROUPS = ((128, 1), (512, 4), (2048, 16))
HEADS_PER_GROUP = 3
WIN_KEYS = 128
D_HEADS = 9
D_WIDTH = D_HEADS * HEAD_DIM
GROUP_WIDTH = HEADS_PER_GROUP * HEAD_DIM
IN_COLS = 2 * A_WIDTH + 2 * B_WIDTH + 3 * C_WIDTH + 3 * D_WIDTH
COL_B = 2 * A_WIDTH
COL_C = COL_B + 2 * B_WIDTH
COL_D = COL_C + 3 * C_WIDTH
ATTN_SCALE = HEAD_DIM ** -0.5
MOE_GROUPS = 4
EXPERTS_PER_GROUP = 4
N_EXPERTS = 16
EXPERT_FF = 512
RMS_EPS = 1e-6
LN_EPS = 1e-5

LANE = 128
ABC_PAD = 1536
OD_PAD = 3 * 256
ROUTER_COLS = 128
VMEM_LIMIT = 56 * 1024 * 1024
NEG_INF = float("-inf")


def _cparams(sem):
    return pltpu.CompilerParams(dimension_semantics=sem, vmem_limit_bytes=VMEM_LIMIT)


def _rms_rows(x, g):
    return x * lax.rsqrt(jnp.mean(x * x, axis=-1, keepdims=True) + RMS_EPS) * g


def _layernorm_rows(x, g, b):
    mu = jnp.mean(x, axis=-1, keepdims=True)
    xc = x - mu
    var = jnp.mean(xc * xc, axis=-1, keepdims=True)
    return xc * lax.rsqrt(var + LN_EPS) * g + b


def _sigmoid(x):
    return 1.0 / (1.0 + jnp.exp(-x))


def _silu(x):
    return x * _sigmoid(x)


def _gelu(x):
    return 0.5 * x * (1.0 + lax.erf(x * (2.0 ** -0.5)))


def _split_bf16(x):
    hi = x.astype(BF16)
    return hi, (x - hi.astype(F32)).astype(BF16)


def _mm(x, w, hp):
    if not hp:
        return jnp.dot(x.astype(BF16), w.astype(BF16), preferred_element_type=F32)
    rows = x.shape[0]
    xh, xl = _split_bf16(x)
    wh, wl = _split_bf16(w)
    r = jnp.dot(jnp.concatenate([xh, xl], axis=0), wh, preferred_element_type=F32)
    return r[:rows] + r[rows:] + jnp.dot(xh, wl, preferred_element_type=F32)


def _act_dtype(hp):
    return F32 if hp else BF16


def _head_norm(x, g):
    outs = []
    for h in range(D_HEADS):
        xh = x[:, HEAD_DIM * h:HEAD_DIM * (h + 1)]
        outs.append(_rms_rows(xh, g))
    return outs


def _split_qkv(z_ref, rows):
    q_lo = (COL_D // LANE) * LANE
    zq = z_ref[0, rows, q_lo:q_lo + 640]
    q = zq[:, COL_D - q_lo:COL_D - q_lo + D_WIDTH]
    k_lo = COL_D + D_WIDTH
    zk = z_ref[0, rows, k_lo:k_lo + 640]
    k = zk[:, :D_WIDTH]
    v_lo = ((COL_D + 2 * D_WIDTH) // LANE) * LANE
    zv = z_ref[0, rows, v_lo:v_lo + 640]
    v = zv[:, COL_D + 2 * D_WIDTH - v_lo:]
    return q, k, v


def _split_c(z_ref, rows):
    zc = z_ref[0, rows, COL_C:COL_C + 1408]
    return zc[:, 0:C_WIDTH], zc[:, C_WIDTH:2 * C_WIDTH], zc[:, 2 * C_WIDTH:3 * C_WIDTH]


def _inproj_kernel(x_ref, g_ref, w_ref, o_ref, xn_ref, *, hp):
    @pl.when(pl.program_id(1) == 0)
    def _():
        xn_ref[...] = _rms_rows(x_ref[...], g_ref[...]).astype(xn_ref.dtype)

    o_ref[...] = _mm(xn_ref[...], w_ref[...], hp)


def _inproj(h, g_all, w_all, layer, tm, tn, hp=False):
    rows = h.shape[0]
    return pl.pallas_call(
        functools.partial(_inproj_kernel, hp=hp),
        grid=(rows // tm, IN_COLS // tn),
        in_specs=[
            pl.BlockSpec((tm, D_MODEL), lambda i, j: (i, 0)),
            pl.BlockSpec((None, 1, D_MODEL), lambda i, j: (layer, 0, 0)),
            pl.BlockSpec((None, D_MODEL, tn), lambda i, j: (layer, 0, j)),
        ],
        out_specs=pl.BlockSpec((tm, tn), lambda i, j: (i, j)),
        out_shape=jax.ShapeDtypeStruct((rows, IN_COLS), F32),
        scratch_shapes=[pltpu.VMEM((tm, D_MODEL), _act_dtype(hp))],
        compiler_params=_cparams(("arbitrary", "arbitrary")),
        name="inproj",
    )(h, g_all, w_all)


MIX_TT = 256
CONV_ROWS = 64
A_HALO = 32
C_HALO = 8


def _pair_weights(sw_ref, wp_ref):
    row = lax.broadcasted_iota(jnp.int32, (CHUNK, CHUNK), 0)
    col = lax.broadcasted_iota(jnp.int32, (CHUNK, CHUNK), 1)
    keep = col <= row
    for p in range(B_HEADS // 2):
        w0 = jnp.where(keep, sw_ref[2 * p], 0.0)
        w1 = jnp.where(keep, sw_ref[2 * p + 1], 0.0)
        wp_ref[p] = jnp.concatenate([w0, w1], axis=1).astype(BF16)


def _mixer_kernel(z_ref, caw_ref, cab_ref, lag_ref, lab_ref, lbg_ref, lbb_ref, sw_ref, sb_ref, ccw_ref,
                  gq_ref, gk_ref,
                  mix_ref, q1_ref, kv1_ref, q4_ref, kv4_ref, q16_ref, kv16_ref,
                  sta_ref, stc_ref, st1_ref, st4_ref, st16_ref,
                  abuf, cbuf, wp_ref, qs_ref, kvs_ref):
    t = pl.program_id(1)
    tt = MIX_TT

    @pl.when(t == 0)
    def _():
        abuf[0:A_HALO, :] = jnp.zeros((A_HALO, A_WIDTH), F32)
        cbuf[0:C_HALO, :] = jnp.zeros((C_HALO, C_WIDTH), F32)
        _pair_weights(sw_ref, wp_ref)

    @pl.when(t > 0)
    def _():
        abuf[0:A_HALO, :] = abuf[tt:tt + A_HALO, :]
        cbuf[0:C_HALO, :] = cbuf[tt:tt + C_HALO, :]

    za = z_ref[0, :, 0:2 * A_WIDTH]
    abuf[A_HALO:A_HALO + tt, :] = za[:, :A_WIDTH] * _sigmoid(za[:, A_WIDTH:])
    sta_ref[0] = abuf[tt:tt + A_HALO, :]
    base = A_HALO - (A_CONV_LEN - 1)
    for r0 in range(0, tt, CONV_ROWS):
        acc = caw_ref[0:1, :] * abuf[r0 + base:r0 + base + CONV_ROWS, :]
        for j in range(1, A_CONV_LEN):
            acc = acc + caw_ref[j:j + 1, :] * abuf[r0 + base + j:r0 + base + j + CONV_ROWS, :]
        y = _layernorm_rows(acc + cab_ref[...], lag_ref[...], lab_ref[...])
        mix_ref[0, r0:r0 + CONV_ROWS, 0:A_WIDTH] = _silu(y).astype(mix_ref.dtype)

    lane = lax.broadcasted_iota(jnp.int32, (CHUNK, LANE), 1)
    for c0 in range(0, tt, CHUNK):
        gb = _gelu(z_ref[0, c0:c0 + CHUNK, COL_B:COL_B + 2 * B_WIDTH])
        u = gb[:, :B_WIDTH]
        v = _layernorm_rows(gb[:, B_WIDTH:], lbg_ref[...], lbb_ref[...])
        pieces = []
        for p in range(B_HEADS // 2):
            v128 = v[:, LANE * p:LANE * (p + 1)]
            rhs = jnp.concatenate([jnp.where(lane < HEAD_DIM, v128, 0.0),
                                   jnp.where(lane >= HEAD_DIM, v128, 0.0)], axis=0).astype(BF16)
            pieces.append(jnp.dot(wp_ref[p], rhs, preferred_element_type=F32))
        mixed = jnp.concatenate(pieces, axis=1) + sb_ref[...]
        mix_ref[0, c0:c0 + CHUNK, A_WIDTH:A_WIDTH + B_WIDTH] = (u * mixed).astype(mix_ref.dtype)

    g_b, g_c, x_c = _split_c(z_ref, slice(None))
    cbuf[C_HALO:C_HALO + tt, :] = g_c * x_c
    stc_ref[0] = cbuf[tt:tt + C_HALO, :]
    cbase = C_HALO - (C_CONV_LEN - 1)
    conv = ccw_ref[0:1, :] * cbuf[cbase:cbase + tt, :]
    for j in range(1, C_CONV_LEN):
        conv = conv + ccw_ref[j:j + 1, :] * cbuf[cbase + j:cbase + j + tt, :]
    o_c = jnp.concatenate([g_b * conv, jnp.zeros((tt, ABC_PAD - 2 * A_WIDTH - C_WIDTH), F32)], axis=1)
    mix_ref[0, :, 2 * A_WIDTH:ABC_PAD] = o_c.astype(mix_ref.dtype)

    q, k, v = _split_qkv(z_ref, slice(None))
    qn = _head_norm(q, gq_ref[...] * ATTN_SCALE)
    kn = _head_norm(k, gk_ref[...])
    outs = ((q1_ref, kv1_ref, st1_ref), (q4_ref, kv4_ref, st4_ref), (q16_ref, kv16_ref, st16_ref))
    for g, (window, dil) in enumerate(ATTN_GROUPS):
        q_ref, kv_ref, st_ref = outs[g]
        hs = slice(HEADS_PER_GROUP * g, HEADS_PER_GROUP * (g + 1))
        q_g = jnp.concatenate(qn[hs], axis=1)
        kv_g = jnp.concatenate(kn[hs] + [v[:, GROUP_WIDTH * g:GROUP_WIDTH * (g + 1)]], axis=1)
        keep = min(window, tt)
        st_ref[0] = kv_g[tt - keep:, :]
        if dil == 1:
            q_ref[0, 0] = q_g
            kv_ref[0, 0] = kv_g
        else:
            qs_ref[0] = q_g[:, :LANE]
            qs_ref[1] = jnp.concatenate([q_g[:, LANE:], jnp.zeros((tt, 2 * LANE - GROUP_WIDTH), F32)], axis=1)
            for i in range(3):
                kvs_ref[i] = kv_g[:, LANE * i:LANE * (i + 1)]
            for r in range(dil):
                rows = pl.ds(r, tt // dil, stride=dil)
                q_ref[0, r, :, 0:LANE] = qs_ref[0, rows, :]
                q_ref[0, r, :, LANE:GROUP_WIDTH] = qs_ref[1, rows, :][:, :GROUP_WIDTH - LANE]
                for i in range(3):
                    kv_ref[0, r, :, LANE * i:LANE * (i + 1)] = kvs_ref[i, rows, :]


def _mixers_prompt(z, layer, wts):
    (caw, cab, lag, lab, lbg, lbb, sw, sb_rep, ccw, gq, gk) = wts
    n, s, _ = z.shape
    tt = MIX_TT
    nt = s // tt

    def lw(shape):
        nd = len(shape)
        return pl.BlockSpec((None,) + shape, lambda b, t: (layer,) + (0,) * nd)

    in_specs = [
        pl.BlockSpec((1, tt, IN_COLS), lambda b, t: (b, t, 0)),
        lw((A_CONV_LEN, A_WIDTH)), lw((1, A_WIDTH)), lw((1, A_WIDTH)), lw((1, A_WIDTH)),
        lw((1, B_WIDTH)), lw((1, B_WIDTH)), lw((B_HEADS, CHUNK, CHUNK)), lw((CHUNK, B_WIDTH)),
        lw((C_CONV_LEN, C_WIDTH)), lw((1, HEAD_DIM)), lw((1, HEAD_DIM)),
    ]
    out_shape = [jax.ShapeDtypeStruct((n, s, ABC_PAD), BF16)]
    out_specs = [pl.BlockSpec((1, tt, ABC_PAD), lambda b, t: (b, t, 0))]
    for _, dil in ATTN_GROUPS:
        for width in (GROUP_WIDTH, 2 * GROUP_WIDTH):
            out_shape.append(jax.ShapeDtypeStruct((n, dil, s // dil, width), F32))
            out_specs.append(pl.BlockSpec((1, dil, tt // dil, width), lambda b, t: (b, 0, t, 0)))
    out_shape.append(jax.ShapeDtypeStruct((n, A_HALO, A_WIDTH), F32))
    out_specs.append(pl.BlockSpec((1, A_HALO, A_WIDTH), lambda b, t: (b, 0, 0)))
    out_shape.append(jax.ShapeDtypeStruct((n, C_HALO, C_WIDTH), F32))
    out_specs.append(pl.BlockSpec((1, C_HALO, C_WIDTH), lambda b, t: (b, 0, 0)))
    for window, _ in ATTN_GROUPS:
        keep = min(window, s)
        blk = min(keep, tt)
        first = (s - keep) // blk
        out_shape.append(jax.ShapeDtypeStruct((n, keep, 2 * GROUP_WIDTH), F32))
        if keep <= tt:
            out_specs.append(pl.BlockSpec((1, blk, 2 * GROUP_WIDTH), lambda b, t: (b, 0, 0)))
        else:
            out_specs.append(pl.BlockSpec((1, blk, 2 * GROUP_WIDTH),
                                          lambda b, t, first=first: (b, jnp.maximum(t - first, 0), 0)))
    return pl.pallas_call(
        _mixer_kernel,
        grid=(n, nt),
        in_specs=in_specs,
        out_specs=out_specs,
        out_shape=out_shape,
        scratch_shapes=[
            pltpu.VMEM((tt + A_HALO, A_WIDTH), F32),
            pltpu.VMEM((tt + C_HALO, C_WIDTH), F32),
            pltpu.VMEM((B_HEADS // 2, CHUNK, 2 * CHUNK), BF16),
            pltpu.VMEM((2, tt, LANE), F32),
            pltpu.VMEM((3, tt, LANE), F32),
        ],
        compiler_params=_cparams(("arbitrary", "arbitrary")),
        name="mixers_prompt",
    )(z, caw, cab, lag, lab, lbg, lbb, sw, sb_rep, ccw, gq, gk)


def _attn_kernel(q_ref, kvo_ref, kvp_ref, o_ref, l_ref):
    c = pl.program_id(2)
    q = q_ref[0, 0]
    kvo = kvo_ref[0, 0]
    kvp = kvp_ref[0, 0]
    qi = lax.broadcasted_iota(jnp.int32, (WIN_KEYS, 2 * WIN_KEYS), 0)
    kj = lax.broadcasted_iota(jnp.int32, (WIN_KEYS, 2 * WIN_KEYS), 1)
    dist = qi + WIN_KEYS - kj
    first_key = jnp.where(c > 0, 0, WIN_KEYS)
    mask = (dist >= 0) & (dist <= WIN_KEYS) & (kj >= first_key)
    o_parts, l_parts = [], []
    for h in range(HEADS_PER_GROUP):
        ks = slice(HEAD_DIM * h, HEAD_DIM * (h + 1))
        vs = slice(GROUP_WIDTH + HEAD_DIM * h, GROUP_WIDTH + HEAD_DIM * (h + 1))
        qh = q[:, ks].astype(BF16)
        kk = jnp.concatenate([kvp[:, ks], kvo[:, ks]], axis=0).astype(BF16)
        vv = jnp.concatenate([kvp[:, vs], kvo[:, vs]], axis=0).astype(BF16)
        sc = lax.dot_general(qh, kk, (((1,), (1,)), ((), ())), preferred_element_type=F32)
        sc = jnp.where(mask, sc, NEG_INF)
        m = jnp.max(sc, axis=-1, keepdims=True)
        ex = jnp.exp(sc - m)
        den = jnp.sum(ex, axis=-1, keepdims=True)
        probs = (ex / den).astype(BF16)
        o_parts.append(jnp.dot(probs, vv, preferred_element_type=F32))
        l_parts.append(jnp.broadcast_to(m + jnp.log(den), (WIN_KEYS, HEAD_DIM)))
    o_ref[0, 0] = jnp.concatenate(o_parts, axis=1)
    l_ref[0, 0] = jnp.concatenate(l_parts, axis=1)


def _attn_prompt(q, kv):
    n, dil, sub, _ = q.shape
    nb = sub // WIN_KEYS
    qspec = pl.BlockSpec((1, 1, WIN_KEYS, GROUP_WIDTH), lambda b, r, c: (b, r, c, 0))
    return pl.pallas_call(
        _attn_kernel,
        grid=(n, dil, nb),
        in_specs=[
            qspec,
            pl.BlockSpec((1, 1, WIN_KEYS, 2 * GROUP_WIDTH), lambda b, r, c: (b, r, c, 0)),
            pl.BlockSpec((1, 1, WIN_KEYS, 2 * GROUP_WIDTH), lambda b, r, c: (b, r, jnp.maximum(c - 1, 0), 0)),
        ],
        out_specs=[qspec, qspec],
        out_shape=[jax.ShapeDtypeStruct(q.shape, F32), jax.ShapeDtypeStruct(q.shape, F32)],
        compiler_params=_cparams(("arbitrary", "arbitrary", "arbitrary")),
        name="attn_prompt",
    )(q, kv, kv)


def _combine_kernel(o1_ref, l1_ref, o4_ref, l4_ref, o16_ref, l16_ref, od_ref, s_o4, s_l4, s_o16, s_l16):
    tt = MIX_TT
    for dil, src, dst in ((4, o4_ref, s_o4), (4, l4_ref, s_l4), (16, o16_ref, s_o16), (16, l16_ref, s_l16)):
        for r in range(dil):
            x = src[0, r]
            rows = pl.ds(r, tt // dil, stride=dil)
            dst[0, rows, :] = x[:, :LANE]
            dst[1, rows, :] = jnp.concatenate(
                [x[:, LANE:], jnp.zeros((tt // dil, 2 * LANE - GROUP_WIDTH), F32)], axis=1)

    def whole(scr):
        return jnp.concatenate([scr[0], scr[1][:, :GROUP_WIDTH - LANE]], axis=1)

    outs = (o1_ref[0, 0], whole(s_o4), whole(s_o16))
    lses = (l1_ref[0, 0], whole(s_l4), whole(s_l16))
    mx = jnp.maximum(jnp.maximum(lses[0], lses[1]), lses[2])
    es = [jnp.exp(l - mx) for l in lses]
    den = es[0] + es[1] + es[2]
    pad = jnp.zeros((tt, OD_PAD // 3 - GROUP_WIDTH), F32)
    parts = []
    for g in range(3):
        parts += [outs[g] * (es[g] / den), pad]
    od_ref[0] = jnp.concatenate(parts, axis=1).astype(od_ref.dtype)


def _combine_prompt(o1, l1, o4, l4, o16, l16):
    n, _, s, _ = o1.shape
    tt = MIX_TT

    def spec(dil):
        return pl.BlockSpec((1, dil, tt // dil, GROUP_WIDTH), lambda b, t: (b, 0, t, 0))

    return pl.pallas_call(
        _combine_kernel,
        grid=(n, s // tt),
        in_specs=[spec(1), spec(1), spec(4), spec(4), spec(16), spec(16)],
        out_specs=pl.BlockSpec((1, tt, OD_PAD), lambda b, t: (b, t, 0)),
        out_shape=jax.ShapeDtypeStruct((n, s, OD_PAD), BF16),
        scratch_shapes=[pltpu.VMEM((2, tt, LANE), F32)] * 4,
        compiler_params=_cparams(("arbitrary", "arbitrary")),
        name="combine_prompt",
    )(o1, l1, o4, l4, o16, l16)


def _dec_kernel(z_ref, ha_ref, hc_ref, c1_ref, c4_ref, c16_ref,
                caw_ref, cab_ref, lag_ref, lab_ref, lbg_ref, lbb_ref, sw8_ref, sb_ref, ccw_ref, gq_ref, gk_ref,
                mix_ref, od_ref, na_ref, nc_ref, cv_ref, n1_ref, n4_ref, n16_ref,
                abuf, cbuf, kvbuf, exbuf):
    t_new = z_ref.shape[1]
    hist_a = A_CONV_LEN - 1
    hist_c = C_CONV_LEN - 1

    za = z_ref[0, :, 0:2 * A_WIDTH]
    abuf[0:hist_a, :] = ha_ref[0]
    abuf[hist_a:hist_a + t_new, :] = za[:, :A_WIDTH] * _sigmoid(za[:, A_WIDTH:])
    acc = caw_ref[0:1, :] * abuf[0:t_new, :]
    for j in range(1, A_CONV_LEN):
        acc = acc + caw_ref[j:j + 1, :] * abuf[j:j + t_new, :]
    y = _layernorm_rows(acc + cab_ref[...], lag_ref[...], lab_ref[...])
    mix_ref[0, :, 0:A_WIDTH] = _silu(y).astype(mix_ref.dtype)
    na_ref[0] = abuf[t_new:t_new + hist_a, :]

    gb = _gelu(z_ref[0, :, COL_B:COL_B + 2 * B_WIDTH])
    u = gb[:, :B_WIDTH]
    v = _layernorm_rows(gb[:, B_WIDTH:], lbg_ref[...], lbb_ref[...])
    cv_ref[0] = v
    row = lax.broadcasted_iota(jnp.int32, (t_new, B_WIDTH), 0)
    mixed = sb_ref[0:t_new, :]
    for s in range(t_new):
        mixed = mixed + jnp.where(row >= s, sw8_ref[s], 0.0) * v[s:s + 1, :]
    mix_ref[0, :, A_WIDTH:A_WIDTH + B_WIDTH] = (u * mixed).astype(mix_ref.dtype)

    g_b, g_c, x_c = _split_c(z_ref, slice(None))
    cbuf[0:hist_c, :] = hc_ref[0]
    cbuf[hist_c:hist_c + t_new, :] = g_c * x_c
    conv = ccw_ref[0:1, :] * cbuf[0:t_new, :]
    for j in range(1, C_CONV_LEN):
        conv = conv + ccw_ref[j:j + 1, :] * cbuf[j:j + t_new, :]
    o_c = jnp.concatenate([g_b * conv, jnp.zeros((t_new, ABC_PAD - 2 * A_WIDTH - C_WIDTH), F32)], axis=1)
    mix_ref[0, :, 2 * A_WIDTH:ABC_PAD] = o_c.astype(mix_ref.dtype)
    nc_ref[0] = cbuf[t_new:t_new + hist_c, :]

    q, k, v_d = _split_qkv(z_ref, slice(None))
    qn = _head_norm(q, gq_ref[...] * ATTN_SCALE)
    kn = _head_norm(k, gk_ref[...])
    caches = (c1_ref, c4_ref, c16_ref)
    news = (n1_ref, n4_ref, n16_ref)
    qrow = lax.broadcasted_iota(jnp.int32, (LANE, 2 * GROUP_WIDTH), 0)
    qlane = lax.broadcasted_iota(jnp.int32, (LANE, 2 * GROUP_WIDTH), 1)
    qmask = (qrow >> 3) == (qlane >> 6)
    zeros_q = jnp.zeros((t_new, GROUP_WIDTH), F32)
    lses, dens, offs = [], [], []
    off = 0
    for g, (window, dil) in enumerate(ATTN_GROUPS):
        hs = slice(HEADS_PER_GROUP * g, HEADS_PER_GROUP * (g + 1))
        buf_len = caches[g].shape[1]
        rows = buf_len + t_new
        kv_new = jnp.concatenate(kn[hs] + [v_d[:, GROUP_WIDTH * g:GROUP_WIDTH * (g + 1)]], axis=1)
        kvbuf[off:off + buf_len, :] = caches[g][0]
        kvbuf[off + buf_len:off + rows, :] = kv_new
        news[g][0] = kvbuf[off + t_new:off + rows, :]
        q_g = jnp.concatenate(qn[hs] + [zeros_q], axis=1)
        q_rep = jnp.concatenate([q_g] * HEADS_PER_GROUP
                                + [jnp.zeros((LANE - HEADS_PER_GROUP * t_new, 2 * GROUP_WIDTH), F32)], axis=0)
        q_hi, q_lo = _split_bf16(jnp.where(qmask, q_rep, 0.0))
        kv_hi, kv_lo = _split_bf16(kvbuf[off:off + rows, :])
        nt_dims = (((1,), (1,)), ((), ()))
        sc = (lax.dot_general(kv_hi, q_hi, nt_dims, preferred_element_type=F32)
              + lax.dot_general(kv_lo, q_hi, nt_dims, preferred_element_type=F32)
              + lax.dot_general(kv_hi, q_lo, nt_dims, preferred_element_type=F32))
        krow = lax.broadcasted_iota(jnp.int32, (rows, LANE), 0)
        tok = lax.broadcasted_iota(jnp.int32, (rows, LANE), 1) & (t_new - 1)
        dist = buf_len + tok - krow
        valid = (dist >= 0) & (dist <= dil * WIN_KEYS) & ((dist & (dil - 1)) == 0)
        sc = jnp.where(valid, sc, NEG_INF)
        m = jnp.max(sc, axis=0, keepdims=True)
        ex = jnp.exp(sc - m)
        den = jnp.sum(ex, axis=0, keepdims=True)
        exbuf[off:off + rows, :] = ex
        lses.append(m + jnp.log(den))
        dens.append(den)
        offs.append((off, rows))
        off += rows
    mx = jnp.maximum(jnp.maximum(lses[0], lses[1]), lses[2])
    es = [jnp.exp(l - mx) for l in lses]
    tot = es[0] + es[1] + es[2]
    lane = lax.broadcasted_iota(jnp.int32, (t_new, 2 * GROUP_WIDTH), 1)
    pad = jnp.zeros((t_new, OD_PAD // 3 - GROUP_WIDTH), F32)
    parts = []
    for g in range(3):
        off, rows = offs[g]
        coef = es[g] / (tot * dens[g])
        probs = (exbuf[off:off + rows, :] * coef).astype(BF16)
        kv_all = kvbuf[off:off + rows, :].astype(BF16)
        o_t = lax.dot_general(probs, kv_all, (((0,), (0,)), ((), ())), preferred_element_type=F32)
        o_g = jnp.zeros((t_new, 2 * GROUP_WIDTH), F32)
        for h in range(HEADS_PER_GROUP):
            sel = (lane >= GROUP_WIDTH + HEAD_DIM * h) & (lane < GROUP_WIDTH + HEAD_DIM * (h + 1))
            o_g = o_g + jnp.where(sel, o_t[t_new * h:t_new * (h + 1), :], 0.0)
        parts += [o_g[:, GROUP_WIDTH:], pad]
    od_ref[0] = jnp.concatenate(parts, axis=1).astype(od_ref.dtype)


def _mixers_sample(z, layer, hist_a, hist_c, caches, wts):
    (caw, cab, lag, lab, lbg, lbb, sw8, sb_rep, ccw, gq, gk) = wts
    n, t_new, _ = z.shape

    def lw(shape):
        nd = len(shape)
        return pl.BlockSpec((None,) + shape, lambda b: (layer,) + (0,) * nd)

    def st(shape):
        nd = len(shape)
        return pl.BlockSpec((None, 1) + shape, lambda b: (layer, b) + (0,) * nd)

    lens = [c.shape[2] for c in caches]
    total_rows = sum(lens) + 3 * t_new
    in_specs = [
        pl.BlockSpec((1, t_new, IN_COLS), lambda b: (b, 0, 0)),
        st((A_CONV_LEN - 1, A_WIDTH)), st((C_CONV_LEN - 1, C_WIDTH)),
        st((lens[0], 2 * GROUP_WIDTH)), st((lens[1], 2 * GROUP_WIDTH)), st((lens[2], 2 * GROUP_WIDTH)),
        lw((A_CONV_LEN, A_WIDTH)), lw((1, A_WIDTH)), lw((1, A_WIDTH)), lw((1, A_WIDTH)),
        lw((1, B_WIDTH)), lw((1, B_WIDTH)), lw((t_new, t_new, B_WIDTH)), lw((CHUNK, B_WIDTH)),
        lw((C_CONV_LEN, C_WIDTH)), lw((1, HEAD_DIM)), lw((1, HEAD_DIM)),
    ]

    def ob(shape):
        nd = len(shape)
        return pl.BlockSpec((1,) + shape, lambda b: (b,) + (0,) * nd)

    out_shape = [
        jax.ShapeDtypeStruct((n, t_new, ABC_PAD), F32),
        jax.ShapeDtypeStruct((n, t_new, OD_PAD), F32),
        jax.ShapeDtypeStruct((n, A_CONV_LEN - 1, A_WIDTH), F32),
        jax.ShapeDtypeStruct((n, C_CONV_LEN - 1, C_WIDTH), F32),
        jax.ShapeDtypeStruct((n, t_new, B_WIDTH), F32),
    ] + [jax.ShapeDtypeStruct((n, ln, 2 * GROUP_WIDTH), F32) for ln in lens]
    out_specs = [ob(s.shape[1:]) for s in out_shape]
    return pl.pallas_call(
        _dec_kernel,
        grid=(n,),
        in_specs=in_specs,
        out_specs=out_specs,
        out_shape=out_shape,
        scratch_shapes=[
            pltpu.VMEM((A_CONV_LEN - 1 + t_new + 2, A_WIDTH), F32),
            pltpu.VMEM((16, C_WIDTH), F32),
            pltpu.VMEM((total_rows, 2 * GROUP_WIDTH), F32),
            pltpu.VMEM((total_rows, LANE), F32),
        ],
        compiler_params=_cparams(("arbitrary",)),
        name="mixers_sample",
    )(z, hist_a, hist_c, *caches, caw, cab, lag, lab, lbg, lbb, sw8, sb_rep, ccw, gq, gk)


def _outproj_kernel(h_ref, mix_ref, od_ref, w_ref, wd_ref, o_ref, *, hp):
    acc = _mm(mix_ref[...], w_ref[0:ABC_PAD, :], hp) + _mm(od_ref[...], wd_ref[...], hp)
    o_ref[...] = h_ref[...] + acc


def _outproj(h, mix, od, w_all, wd_all, layer, tm, tn, hp):
    rows = h.shape[0]
    return pl.pallas_call(
        functools.partial(_outproj_kernel, hp=hp),
        grid=(rows // tm, D_MODEL // tn),
        in_specs=[
            pl.BlockSpec((tm, tn), lambda i, j: (i, j)),
            pl.BlockSpec((tm, ABC_PAD), lambda i, j: (i, 0)),
            pl.BlockSpec((tm, OD_PAD), lambda i, j: (i, 0)),
            pl.BlockSpec((None, D_MODEL, tn), lambda i, j: (layer, 0, j)),
            pl.BlockSpec((None, OD_PAD, tn), lambda i, j: (layer, 0, j)),
        ],
        out_specs=pl.BlockSpec((tm, tn), lambda i, j: (i, j)),
        out_shape=jax.ShapeDtypeStruct((rows, D_MODEL), F32),
        compiler_params=_cparams(("arbitrary", "arbitrary")),
        name="outproj",
    )(h, mix, od, w_all, wd_all)


def _router_kernel(h_ref, g_ref, wr_ref, br_ref, m_ref, comb_ref, *, hp):
    m = _rms_rows(h_ref[...], g_ref[...]).astype(m_ref.dtype)
    m_ref[...] = m
    logits = _mm(m, wr_ref[...], hp) + br_ref[...]
    rows = logits.shape[0]
    lane = lax.broadcasted_iota(jnp.int32, (rows, ROUTER_COLS), 1)
    big = jnp.int32(ROUTER_COLS)
    is_grp = (lane >= N_EXPERTS) & (lane < N_EXPERTS + MOE_GROUPS)
    gl = jnp.where(is_grp, logits, NEG_INF)
    gmax = jnp.max(gl, axis=-1, keepdims=True)
    gidx = jnp.min(jnp.where(gl == gmax, lane - N_EXPERTS, big), axis=-1, keepdims=True)
    gate = 1.0 / jnp.sum(jnp.where(is_grp, jnp.exp(gl - gmax), 0.0), axis=-1, keepdims=True)
    in_grp = (lane < N_EXPERTS) & ((lane >> 2) == gidx)
    el = jnp.where(in_grp, logits, NEG_INF)
    t1 = jnp.max(el, axis=-1, keepdims=True)
    i1 = jnp.min(jnp.where(el == t1, lane, big), axis=-1, keepdims=True)
    el2 = jnp.where(lane == i1, NEG_INF, el)
    t2 = jnp.max(el2, axis=-1, keepdims=True)
    i2 = jnp.min(jnp.where(el2 == t2, lane, big), axis=-1, keepdims=True)
    e2 = jnp.exp(t2 - t1)
    w1 = gate / (1.0 + e2)
    w2 = gate * e2 / (1.0 + e2)
    comb_ref[...] = jnp.where(lane == i1, w1, 0.0) + jnp.where(lane == i2, w2, 0.0)


def _router(h, g_all, wr_all, br_all, layer, tm, hp):
    rows = h.shape[0]
    return pl.pallas_call(
        functools.partial(_router_kernel, hp=hp),
        grid=(rows // tm,),
        in_specs=[
            pl.BlockSpec((tm, D_MODEL), lambda i: (i, 0)),
            pl.BlockSpec((None, 1, D_MODEL), lambda i: (layer, 0, 0)),
            pl.BlockSpec((None, D_MODEL, ROUTER_COLS), lambda i: (layer, 0, 0)),
            pl.BlockSpec((None, 1, ROUTER_COLS), lambda i: (layer, 0, 0)),
        ],
        out_specs=[pl.BlockSpec((tm, D_MODEL), lambda i: (i, 0)),
                   pl.BlockSpec((tm, ROUTER_COLS), lambda i: (i, 0))],
        out_shape=[jax.ShapeDtypeStruct((rows, D_MODEL), _act_dtype(hp)),
                   jax.ShapeDtypeStruct((rows, ROUTER_COLS), F32)],
        compiler_params=_cparams(("arbitrary",)),
        name="router",
    )(h, g_all, wr_all, br_all)


def _moe_kernel(h_ref, m_ref, comb_ref, wg_ref, wu_ref, wd_ref, o_ref, *, hp):
    e = pl.program_id(1)

    @pl.when(e == 0)
    def _():
        o_ref[...] = h_ref[...]

    x = m_ref[...]
    gate = _mm(x, wg_ref[...], hp)
    up = _mm(x, wu_ref[...], hp)
    lane = lax.broadcasted_iota(jnp.int32, comb_ref.shape, 1)
    w = jnp.sum(jnp.where(lane == e, comb_ref[...], 0.0), axis=-1, keepdims=True)
    o_ref[...] += _mm(_silu(gate) * up * w, wd_ref[...], hp)


def _moe_dense(h, m, comb, wg_all, wu_all, wd_all, layer, tm, hp):
    rows = h.shape[0]
    return pl.pallas_call(
        functools.partial(_moe_kernel, hp=hp),
        grid=(rows // tm, N_EXPERTS),
        in_specs=[
            pl.BlockSpec((tm, D_MODEL), lambda i, e: (i, 0)),
            pl.BlockSpec((tm, D_MODEL), lambda i, e: (i, 0)),
            pl.BlockSpec((tm, ROUTER_COLS), lambda i, e: (i, 0)),
            pl.BlockSpec((None, None, D_MODEL, EXPERT_FF), lambda i, e: (layer, e, 0, 0)),
            pl.BlockSpec((None, None, D_MODEL, EXPERT_FF), lambda i, e: (layer, e, 0, 0)),
            pl.BlockSpec((None, None, EXPERT_FF, D_MODEL), lambda i, e: (layer, e, 0, 0)),
        ],
        out_specs=pl.BlockSpec((tm, D_MODEL), lambda i, e: (i, 0)),
        out_shape=jax.ShapeDtypeStruct((rows, D_MODEL), F32),
        compiler_params=_cparams(("arbitrary", "arbitrary")),
        name="moe_dense",
    )(h, m, comb, wg_all, wu_all, wd_all)


def _ple_kernel(h_ref, hc_ref, g_ref, p_ref, wg_ref, wp_ref, o_ref, xn_ref, *, hp):
    @pl.when(pl.program_id(1) == 0)
    def _():
        xn_ref[...] = _rms_rows(h_ref[...], g_ref[...]).astype(xn_ref.dtype)

    gate = _sigmoid(_mm(xn_ref[...], wg_ref[...], hp))
    o_ref[...] = hc_ref[...] + gate * _mm(p_ref[...], wp_ref[...], hp)


def _ple(h, p_all, g_all, wg_all, wp_all, layer, tm, tn, hp):
    rows = h.shape[0]
    return pl.pallas_call(
        functools.partial(_ple_kernel, hp=hp),
        grid=(rows // tm, D_MODEL // tn),
        in_specs=[
            pl.BlockSpec((tm, D_MODEL), lambda i, j: (i, 0)),
            pl.BlockSpec((tm, tn), lambda i, j: (i, j)),
            pl.BlockSpec((None, 1, D_MODEL), lambda i, j: (layer, 0, 0)),
            pl.BlockSpec((None, tm, PLE_DIM), lambda i, j: (layer, i, 0)),
            pl.BlockSpec((None, D_MODEL, tn), lambda i, j: (layer, 0, j)),
            pl.BlockSpec((None, PLE_DIM, tn), lambda i, j: (layer, 0, j)),
        ],
        out_specs=pl.BlockSpec((tm, tn), lambda i, j: (i, j)),
        out_shape=jax.ShapeDtypeStruct((rows, D_MODEL), F32),
        scratch_shapes=[pltpu.VMEM((tm, D_MODEL), _act_dtype(hp))],
        compiler_params=_cparams(("arbitrary", "arbitrary")),
        name="ple",
    )(h, h, g_all, p_all, wg_all, wp_all)


def _token_tail(h, mix, od, p_all, layer, tw, tm, hp):
    (w_out, w_out_d, g_ffn, w_router, b_router, w_gate, w_up, w_down, g_ple, w_ple_gate, w_ple_proj) = tw
    h = _outproj(h, mix, od, w_out, w_out_d, layer, tm, 512, hp)
    m, comb = _router(h, g_ffn, w_router, b_router, layer, min(tm, 512), hp)
    h = _moe_dense(h, m, comb, w_gate, w_up, w_down, layer, min(tm, 512), hp)
    return _ple(h, p_all, g_ple, w_ple_gate, w_ple_proj, layer, tm, 512, hp)


TAIL = 8


def _patch_kernel(x_ref, tail_ref, o_ref):
    del x_ref
    o_ref[...] = tail_ref[...]


def _patch_tail(x, tail):
    n, s, width = x.shape
    return pl.pallas_call(
        _patch_kernel,
        grid=(n,),
        in_specs=[pl.BlockSpec(memory_space=pl.ANY),
                  pl.BlockSpec((1, TAIL, width), lambda b: (b, 0, 0))],
        out_specs=pl.BlockSpec((1, TAIL, width), lambda b: (b, s // TAIL - 1, 0)),
        out_shape=jax.ShapeDtypeStruct(x.shape, x.dtype),
        input_output_aliases={0: 0},
        compiler_params=_cparams(("arbitrary",)),
        name="patch_tail",
    )(x, tail)


def kernel(x_prompt, x_sample, p_prompt, p_sample, state_conv_a, state_conv_c, cache_kv_w128, cache_kv_w512, cache_kv_w2048, g_mix, w_in, conv_a_w, conv_a_b, ln_a_g, ln_a_b, ln_b_g, ln_b_b, sgu_w, sgu_b, conv_c_w, g_q, g_k, w_out, g_ffn, w_router_grp, b_router_grp, w_router_exp, b_router_exp, w_gate, w_up, w_down, g_ple, w_ple_gate, w_ple_proj):
    n_p, s_p, _ = x_prompt.shape
    n_s, t_s, _ = x_sample.shape
    rows_p = n_p * s_p
    rows_s = n_s * t_s

    def row3(a):
        return a.reshape(DEPTH, 1, a.shape[-1])

    g_mix3, g_ffn3, g_ple3 = row3(g_mix), row3(g_ffn), row3(g_ple)
    cab3, lag3, lab3, lbg3, lbb3 = row3(conv_a_b), row3(ln_a_g), row3(ln_a_b), row3(ln_b_g), row3(ln_b_b)
    gq3, gk3 = row3(g_q), row3(g_k)
    sb_rep = jnp.repeat(jnp.swapaxes(sgu_b, 1, 2), HEAD_DIM, axis=2)
    sw8 = jnp.repeat(jnp.transpose(sgu_w[:, :, :t_s, :t_s], (0, 3, 2, 1)), HEAD_DIM, axis=3)
    w_out_d = jnp.pad(w_out[:, ABC_PAD - 64:].reshape(DEPTH, 3, GROUP_WIDTH, D_MODEL),
                      ((0, 0), (0, 0), (0, OD_PAD // 3 - GROUP_WIDTH), (0, 0))).reshape(DEPTH, OD_PAD, D_MODEL)
    w_router = jnp.concatenate(
        [jnp.transpose(w_router_exp, (0, 2, 1, 3)).reshape(DEPTH, D_MODEL, N_EXPERTS), w_router_grp,
         jnp.zeros((DEPTH, D_MODEL, ROUTER_COLS - N_EXPERTS - MOE_GROUPS), F32)], axis=2)
    b_router = jnp.concatenate(
        [b_router_exp.reshape(DEPTH, N_EXPERTS), b_router_grp,
         jnp.zeros((DEPTH, ROUTER_COLS - N_EXPERTS - MOE_GROUPS), F32)], axis=1).reshape(DEPTH, 1, ROUTER_COLS)
    tail_w = (w_out, w_out_d, g_ffn3, w_router, b_router, w_gate, w_up, w_down, g_ple3, w_ple_gate, w_ple_proj)
    mix_w_p = (conv_a_w, cab3, lag3, lab3, lbg3, lbb3, sgu_w, sb_rep, conv_c_w, gq3, gk3)
    mix_w_s = (conv_a_w, cab3, lag3, lab3, lbg3, lbb3, sw8, sb_rep, conv_c_w, gq3, gk3)

    caches = [c.reshape(c.shape[0], c.shape[1], c.shape[2], 2 * GROUP_WIDTH)
              for c in (cache_kv_w128, cache_kv_w512, cache_kv_w2048)]
    p_p = p_prompt.reshape(DEPTH, rows_p, PLE_DIM)
    rows_t = n_p * TAIL
    rows_h = rows_s + rows_t
    p_h = jnp.concatenate([p_sample.reshape(DEPTH, rows_s, PLE_DIM),
                           p_prompt[:, :, s_p - TAIL:].reshape(DEPTH, rows_t, PLE_DIM)], axis=1)
    h_h = jnp.concatenate([x_sample.reshape(rows_s, D_MODEL),
                           x_prompt[:, s_p - TAIL:].reshape(rows_t, D_MODEL)], axis=0)

    h = x_prompt.reshape(rows_p, D_MODEL)
    st_a, st_c, st_kv = [], [], [[], [], []]
    sa, sc, sv, skv = [], [], [], [[], [], []]
    for i in range(DEPTH):
        z_h = _inproj(h_h, g_mix3, w_in, i, rows_h, 512, hp=True)
        z = _inproj(h, g_mix3, w_in, i, 1024, 512).reshape(n_p, s_p, IN_COLS)
        z = _patch_tail(z, z_h[rows_s:].reshape(n_p, TAIL, IN_COLS))
        (mix, q1, kv1, q4, kv4, q16, kv16, sta, stc, s1, s4, s16) = _mixers_prompt(z, i, mix_w_p)
        o1, l1 = _attn_prompt(q1, kv1)
        o4, l4 = _attn_prompt(q4, kv4)
        o16, l16 = _attn_prompt(q16, kv16)
        od = _combine_prompt(o1, l1, o4, l4, o16, l16)
        (mix_s, od_s, na, nc, cv, n1, n4, n16) = _mixers_sample(
            z_h[:rows_s].reshape(n_s, t_s, IN_COLS), i, state_conv_a, state_conv_c, caches, mix_w_s)
        mix_h = jnp.concatenate([mix_s.reshape(rows_s, ABC_PAD),
                                 mix[:, s_p - TAIL:].reshape(rows_t, ABC_PAD).astype(F32)], axis=0)
        od_h = jnp.concatenate([od_s.reshape(rows_s, OD_PAD),
                                od[:, s_p - TAIL:].reshape(rows_t, OD_PAD).astype(F32)], axis=0)
        h = _token_tail(h, mix.reshape(rows_p, ABC_PAD), od.reshape(rows_p, OD_PAD), p_p, i, tail_w, 1024, False)
        h_h = _token_tail(h_h, mix_h, od_h, p_h, i, tail_w, rows_h, True)
        st_a.append(sta[:, A_HALO - (A_CONV_LEN - 1):])
        st_c.append(stc[:, C_HALO - (C_CONV_LEN - 1):])
        for g, s_kv in enumerate((s1, s4, s16)):
            st_kv[g].append(s_kv.reshape(n_p, s_kv.shape[1], 2, HEADS_PER_GROUP, HEAD_DIM))
        sa.append(na)
        sc.append(nc)
        sv.append(cv)
        for g, nk in enumerate((n1, n4, n16)):
            skv[g].append(nk.reshape(n_s, nk.shape[1], 2, HEADS_PER_GROUP, HEAD_DIM))
    y_prompt = _patch_tail(h.reshape(n_p, s_p, D_MODEL), h_h[rows_s:].reshape(n_p, TAIL, D_MODEL))
    conv_a_prompt = jnp.stack(st_a)
    conv_c_prompt = jnp.stack(st_c)
    kv_prompt = [jnp.stack(s) for s in st_kv]
    y_sample = h_h[:rows_s].reshape(n_s, t_s, D_MODEL)
    conv_a_sample = jnp.stack(sa)
    conv_c_sample = jnp.stack(sc)
    chunk_v_sample = jnp.stack(sv)
    kv_sample = [jnp.stack(s) for s in skv]

    return (y_prompt, y_sample, conv_a_prompt, conv_a_sample, conv_c_prompt, conv_c_sample, chunk_v_sample,
            kv_prompt[0], kv_sample[0], kv_prompt[1], kv_sample[1], kv_prompt[2], kv_sample[2])
```

```python
import functools

import jax
import jax.numpy as jnp
from jax import lax
from jax.experimental import pallas as pl
from jax.experimental.pallas import tpu as pltpu

F32 = jnp.float32
BF16 = jnp.bfloat16

D_MODEL = 2048
DEPTH = 4
PLE_DIM = 256
HEAD_DIM = 64
A_WIDTH = 512
A_CONV_LEN = 31
B_WIDTH = 512
B_HEADS = 8
CHUNK = 128
C_WIDTH = 448
C_CONV_LEN = 3
ATTN_GROUPS = ((128, 1), (512, 4), (2048, 16))
HEADS_PER_GROUP = 3
WIN_KEYS = 128
D_HEADS = 9
D_WIDTH = D_HEADS * HEAD_DIM
GROUP_WIDTH = HEADS_PER_GROUP * HEAD_DIM
IN_COLS = 2 * A_WIDTH + 2 * B_WIDTH + 3 * C_WIDTH + 3 * D_WIDTH
COL_B = 2 * A_WIDTH
COL_C = COL_B + 2 * B_WIDTH
COL_D = COL_C + 3 * C_WIDTH
ATTN_SCALE = HEAD_DIM ** -0.5
MOE_GROUPS = 4
EXPERTS_PER_GROUP = 4
N_EXPERTS = 16
EXPERT_FF = 512
RMS_EPS = 1e-6
LN_EPS = 1e-5

LANE = 128
SUBLANE = 8
ABC_PAD = 1536
OD_PAD = 3 * 256
ROUTER_COLS = 128
VMEM_LIMIT = 56 * 1024 * 1024
NEG_INF = float("-inf")


def _cparams(sem):
    return pltpu.CompilerParams(dimension_semantics=sem, vmem_limit_bytes=VMEM_LIMIT)


def _rms_rows(x, g):
    return x * lax.rsqrt(jnp.mean(x * x, axis=-1, keepdims=True) + RMS_EPS) * g


def _layernorm_rows(x, g, b):
    mu = jnp.mean(x, axis=-1, keepdims=True)
    xc = x - mu
    var = jnp.mean(xc * xc, axis=-1, keepdims=True)
    return xc * lax.rsqrt(var + LN_EPS) * g + b


def _sigmoid(x):
    return 1.0 / (1.0 + jnp.exp(-x))


def _silu(x):
    return x * _sigmoid(x)


def _gelu(x):
    return 0.5 * x * (1.0 + lax.erf(x * (2.0 ** -0.5)))


def _split_bf16(x):
    hi = x.astype(BF16)
    return hi, (x - hi.astype(F32)).astype(BF16)


def _mm(x, w, hp):
    if not hp:
        return jnp.dot(x.astype(BF16), w.astype(BF16), preferred_element_type=F32)
    rows = x.shape[0]
    xh, xl = _split_bf16(x)
    wh, wl = _split_bf16(w)
    r = jnp.dot(jnp.concatenate([xh, xl], axis=0), wh, preferred_element_type=F32)
    return r[:rows] + r[rows:] + jnp.dot(xh, wl, preferred_element_type=F32)


def _act_dtype(hp):
    return F32 if hp else BF16


def _head_norm(x, g):
    outs = []
    for h in range(D_HEADS):
        xh = x[:, HEAD_DIM * h:HEAD_DIM * (h + 1)]
        outs.append(_rms_rows(xh, g))
    return outs


def _split_qkv(z_ref, rows):
    q_lo = (COL_D // LANE) * LANE
    zq = z_ref[0, rows, q_lo:q_lo + 640]
    q = zq[:, COL_D - q_lo:COL_D - q_lo + D_WIDTH]
    k_lo = COL_D + D_WIDTH
    zk = z_ref[0, rows, k_lo:k_lo + 640]
    k = zk[:, :D_WIDTH]
    v_lo = ((COL_D + 2 * D_WIDTH) // LANE) * LANE
    zv = z_ref[0, rows, v_lo:v_lo + 640]
    v = zv[:, COL_D + 2 * D_WIDTH - v_lo:]
    return q, k, v


def _split_c(z_ref, rows):
    zc = z_ref[0, rows, COL_C:COL_C + 1408]
    return zc[:, 0:C_WIDTH], zc[:, C_WIDTH:2 * C_WIDTH], zc[:, 2 * C_WIDTH:3 * C_WIDTH]


def _inproj_kernel(x_ref, g_ref, w_ref, o_ref, xn_ref, *, hp):
    @pl.when(pl.program_id(1) == 0)
    def _():
        xn_ref[...] = _rms_rows(x_ref[...], g_ref[...]).astype(xn_ref.dtype)

    o_ref[...] = _mm(xn_ref[...], w_ref[...], hp)


def _inproj(h, g_all, w_all, layer, tm, tn, hp=False):
    rows = h.shape[0]
    return pl.pallas_call(
        functools.partial(_inproj_kernel, hp=hp),
        grid=(rows // tm, IN_COLS // tn),
        in_specs=[
            pl.BlockSpec((tm, D_MODEL), lambda i, j: (i, 0)),
            pl.BlockSpec((None, 1, D_MODEL), lambda i, j: (layer, 0, 0)),
            pl.BlockSpec((None, D_MODEL, tn), lambda i, j: (layer, 0, j)),
        ],
        out_specs=pl.BlockSpec((tm, tn), lambda i, j: (i, j)),
        out_shape=jax.ShapeDtypeStruct((rows, IN_COLS), F32),
        scratch_shapes=[pltpu.VMEM((tm, D_MODEL), _act_dtype(hp))],
        compiler_params=_cparams(("arbitrary", "arbitrary")),
        name="inproj",
    )(h, g_all, w_all)


MIX_TT = 256
CONV_ROWS = 64
A_HALO = 32
C_HALO = 8


def _pair_weights(sw_ref, wp_ref):
    row = lax.broadcasted_iota(jnp.int32, (CHUNK, CHUNK), 0)
    col = lax.broadcasted_iota(jnp.int32, (CHUNK, CHUNK), 1)
    keep = col <= row
    for p in range(B_HEADS // 2):
        w0 = jnp.where(keep, sw_ref[2 * p], 0.0)
        w1 = jnp.where(keep, sw_ref[2 * p + 1], 0.0)
        wp_ref[p] = jnp.concatenate([w0, w1], axis=1).astype(BF16)


def _mixer_kernel(z_ref, caw_ref, cab_ref, lag_ref, lab_ref, lbg_ref, lbb_ref, sw_ref, sb_ref, ccw_ref,
                  gq_ref, gk_ref,
                  mix_ref, q1_ref, kv1_ref, q4_ref, kv4_ref, q16_ref, kv16_ref,
                  sta_ref, stc_ref, st1_ref, st4_ref, st16_ref,
                  abuf, cbuf, wp_ref, qs_ref, kvs_ref):
    t = pl.program_id(1)
    tt = MIX_TT

    @pl.when(t == 0)
    def _():
        abuf[0:A_HALO, :] = jnp.zeros((A_HALO, A_WIDTH), F32)
        cbuf[0:C_HALO, :] = jnp.zeros((C_HALO, C_WIDTH), F32)
        _pair_weights(sw_ref, wp_ref)

    @pl.when(t > 0)
    def _():
        abuf[0:A_HALO, :] = abuf[tt:tt + A_HALO, :]
        cbuf[0:C_HALO, :] = cbuf[tt:tt + C_HALO, :]

    za = z_ref[0, :, 0:2 * A_WIDTH]
    abuf[A_HALO:A_HALO + tt, :] = za[:, :A_WIDTH] * _sigmoid(za[:, A_WIDTH:])
    sta_ref[0] = abuf[tt:tt + A_HALO, :]
    base = A_HALO - (A_CONV_LEN - 1)
    for r0 in range(0, tt, CONV_ROWS):
        acc = caw_ref[0:1, :] * abuf[r0 + base:r0 + base + CONV_ROWS, :]
        for j in range(1, A_CONV_LEN):
            acc = acc + caw_ref[j:j + 1, :] * abuf[r0 + base + j:r0 + base + j + CONV_ROWS, :]
        y = _layernorm_rows(acc + cab_ref[...], lag_ref[...], lab_ref[...])
        mix_ref[0, r0:r0 + CONV_ROWS, 0:A_WIDTH] = _silu(y).astype(mix_ref.dtype)

    lane = lax.broadcasted_iota(jnp.int32, (CHUNK, LANE), 1)
    for c0 in range(0, tt, CHUNK):
        gb = _gelu(z_ref[0, c0:c0 + CHUNK, COL_B:COL_B + 2 * B_WIDTH])
        u = gb[:, :B_WIDTH]
        v = _layernorm_rows(gb[:, B_WIDTH:], lbg_ref[...], lbb_ref[...])
        pieces = []
        for p in range(B_HEADS // 2):
            v128 = v[:, LANE * p:LANE * (p + 1)]
            rhs = jnp.concatenate([jnp.where(lane < HEAD_DIM, v128, 0.0),
                                   jnp.where(lane >= HEAD_DIM, v128, 0.0)], axis=0).astype(BF16)
            pieces.append(jnp.dot(wp_ref[p], rhs, preferred_element_type=F32))
        mixed = jnp.concatenate(pieces, axis=1) + sb_ref[...]
        mix_ref[0, c0:c0 + CHUNK, A_WIDTH:A_WIDTH + B_WIDTH] = (u * mixed).astype(mix_ref.dtype)

    g_b, g_c, x_c = _split_c(z_ref, slice(None))
    cbuf[C_HALO:C_HALO + tt, :] = g_c * x_c
    stc_ref[0] = cbuf[tt:tt + C_HALO, :]
    cbase = C_HALO - (C_CONV_LEN - 1)
    conv = ccw_ref[0:1, :] * cbuf[cbase:cbase + tt, :]
    for j in range(1, C_CONV_LEN):
        conv = conv + ccw_ref[j:j + 1, :] * cbuf[cbase + j:cbase + j + tt, :]
    o_c = jnp.concatenate([g_b * conv, jnp.zeros((tt, ABC_PAD - 2 * A_WIDTH - C_WIDTH), F32)], axis=1)
    mix_ref[0, :, 2 * A_WIDTH:ABC_PAD] = o_c.astype(mix_ref.dtype)

    q, k, v = _split_qkv(z_ref, slice(None))
    qn = _head_norm(q, gq_ref[...] * ATTN_SCALE)
    kn = _head_norm(k, gk_ref[...])
    outs = ((q1_ref, kv1_ref, st1_ref), (q4_ref, kv4_ref, st4_ref), (q16_ref, kv16_ref, st16_ref))
    for g, (window, dil) in enumerate(ATTN_GROUPS):
        q_ref, kv_ref, st_ref = outs[g]
        hs = slice(HEADS_PER_GROUP * g, HEADS_PER_GROUP * (g + 1))
        q_g = jnp.concatenate(qn[hs], axis=1)
        kv_g = jnp.concatenate(kn[hs] + [v[:, GROUP_WIDTH * g:GROUP_WIDTH * (g + 1)]], axis=1)
        keep = min(window, tt)
        st_ref[0] = kv_g[tt - keep:, :]
        if dil == 1:
            q_ref[0, 0] = q_g
            kv_ref[0, 0] = kv_g
        else:
            qs_ref[0] = q_g[:, :LANE]
            qs_ref[1] = jnp.concatenate([q_g[:, LANE:], jnp.zeros((tt, 2 * LANE - GROUP_WIDTH), F32)], axis=1)
            for i in range(3):
                kvs_ref[i] = kv_g[:, LANE * i:LANE * (i + 1)]
            for r in range(dil):
                rows = pl.ds(r, tt // dil, stride=dil)
                q_ref[0, r, :, 0:LANE] = qs_ref[0, rows, :]
                q_ref[0, r, :, LANE:GROUP_WIDTH] = qs_ref[1, rows, :][:, :GROUP_WIDTH - LANE]
                for i in range(3):
                    kv_ref[0, r, :, LANE * i:LANE * (i + 1)] = kvs_ref[i, rows, :]


def _mixers_prompt(z, layer, wts):
    (caw, cab, lag, lab, lbg, lbb, sw, sb_rep, ccw, gq, gk) = wts
    n, s, _ = z.shape
    tt = MIX_TT
    nt = s // tt

    def lw(shape):
        nd = len(shape)
        return pl.BlockSpec((None,) + shape, lambda b, t: (layer,) + (0,) * nd)

    in_specs = [
        pl.BlockSpec((1, tt, IN_COLS), lambda b, t: (b, t, 0)),
        lw((A_CONV_LEN, A_WIDTH)), lw((1, A_WIDTH)), lw((1, A_WIDTH)), lw((1, A_WIDTH)),
        lw((1, B_WIDTH)), lw((1, B_WIDTH)), lw((B_HEADS, CHUNK, CHUNK)), lw((CHUNK, B_WIDTH)),
        lw((C_CONV_LEN, C_WIDTH)), lw((1, HEAD_DIM)), lw((1, HEAD_DIM)),
    ]
    out_shape = [jax.ShapeDtypeStruct((n, s, ABC_PAD), BF16)]
    out_specs = [pl.BlockSpec((1, tt, ABC_PAD), lambda b, t: (b, t, 0))]
    for _, dil in ATTN_GROUPS:
        for width in (GROUP_WIDTH, 2 * GROUP_WIDTH):
            out_shape.append(jax.ShapeDtypeStruct((n, dil, s // dil, width), F32))
            out_specs.append(pl.BlockSpec((1, dil, tt // dil, width), lambda b, t: (b, 0, t, 0)))
    out_shape.append(jax.ShapeDtypeStruct((n, A_HALO, A_WIDTH), F32))
    out_specs.append(pl.BlockSpec((1, A_HALO, A_WIDTH), lambda b, t: (b, 0, 0)))
    out_shape.append(jax.ShapeDtypeStruct((n, C_HALO, C_WIDTH), F32))
    out_specs.append(pl.BlockSpec((1, C_HALO, C_WIDTH), lambda b, t: (b, 0, 0)))
    for window, _ in ATTN_GROUPS:
        keep = min(window, s)
        blk = min(keep, tt)
        first = (s - keep) // blk
        out_shape.append(jax.ShapeDtypeStruct((n, keep, 2 * GROUP_WIDTH), F32))
        if keep <= tt:
            out_specs.append(pl.BlockSpec((1, blk, 2 * GROUP_WIDTH), lambda b, t: (b, 0, 0)))
        else:
            out_specs.append(pl.BlockSpec((1, blk, 2 * GROUP_WIDTH),
                                          lambda b, t, first=first: (b, jnp.maximum(t - first, 0), 0)))
    return pl.pallas_call(
        _mixer_kernel,
        grid=(n, nt),
        in_specs=in_specs,
        out_specs=out_specs,
        out_shape=out_shape,
        scratch_shapes=[
            pltpu.VMEM((tt + A_HALO, A_WIDTH), F32),
            pltpu.VMEM((tt + C_HALO, C_WIDTH), F32),
            pltpu.VMEM((B_HEADS // 2, CHUNK, 2 * CHUNK), BF16),
            pltpu.VMEM((2, tt, LANE), F32),
            pltpu.VMEM((3, tt, LANE), F32),
        ],
        compiler_params=_cparams(("arbitrary", "arbitrary")),
        name="mixers_prompt",
    )(z, caw, cab, lag, lab, lbg, lbb, sw, sb_rep, ccw, gq, gk)


def _attn_kernel(q_ref, kvo_ref, kvp_ref, o_ref, l_ref):
    c = pl.program_id(2)
    q = q_ref[0, 0]
    kvo = kvo_ref[0, 0]
    kvp = kvp_ref[0, 0]
    qi = lax.broadcasted_iota(jnp.int32, (WIN_KEYS, 2 * WIN_KEYS), 0)
    kj = lax.broadcasted_iota(jnp.int32, (WIN_KEYS, 2 * WIN_KEYS), 1)
    dist = qi + WIN_KEYS - kj
    first_key = jnp.where(c > 0, 0, WIN_KEYS)
    mask = (dist >= 0) & (dist <= WIN_KEYS) & (kj >= first_key)
    o_parts, l_parts = [], []
    for h in range(HEADS_PER_GROUP):
        ks = slice(HEAD_DIM * h, HEAD_DIM * (h + 1))
        vs = slice(GROUP_WIDTH + HEAD_DIM * h, GROUP_WIDTH + HEAD_DIM * (h + 1))
        qh = q[:, ks].astype(BF16)
        kk = jnp.concatenate([kvp[:, ks], kvo[:, ks]], axis=0).astype(BF16)
        vv = jnp.concatenate([kvp[:, vs], kvo[:, vs]], axis=0).astype(BF16)
        sc = lax.dot_general(qh, kk, (((1,), (1,)), ((), ())), preferred_element_type=F32)
        sc = jnp.where(mask, sc, NEG_INF)
        m = jnp.max(sc, axis=-1, keepdims=True)
        ex = jnp.exp(sc - m)
        den = jnp.sum(ex, axis=-1, keepdims=True)
        probs = (ex / den).astype(BF16)
        o_parts.append(jnp.dot(probs, vv, preferred_element_type=F32))
        l_parts.append(jnp.broadcast_to(m + jnp.log(den), (WIN_KEYS, HEAD_DIM)))
    o_ref[0, 0] = jnp.concatenate(o_parts, axis=1)
    l_ref[0, 0] = jnp.concatenate(l_parts, axis=1)


def _attn_prompt(q, kv):
    n, dil, sub, _ = q.shape
    nb = sub // WIN_KEYS
    qspec = pl.BlockSpec((1, 1, WIN_KEYS, GROUP_WIDTH), lambda b, r, c: (b, r, c, 0))
    return pl.pallas_call(
        _attn_kernel,
        grid=(n, dil, nb),
        in_specs=[
            qspec,
            pl.BlockSpec((1, 1, WIN_KEYS, 2 * GROUP_WIDTH), lambda b, r, c: (b, r, c, 0)),
            pl.BlockSpec((1, 1, WIN_KEYS, 2 * GROUP_WIDTH), lambda b, r, c: (b, r, jnp.maximum(c - 1, 0), 0)),
        ],
        out_specs=[qspec, qspec],
        out_shape=[jax.ShapeDtypeStruct(q.shape, F32), jax.ShapeDtypeStruct(q.shape, F32)],
        compiler_params=_cparams(("arbitrary", "arbitrary", "arbitrary")),
        name="attn_prompt",
    )(q, kv, kv)


def _combine_kernel(o1_ref, l1_ref, o4_ref, l4_ref, o16_ref, l16_ref, od_ref, s_o4, s_l4, s_o16, s_l16):
    tt = MIX_TT
    for dil, src, dst in ((4, o4_ref, s_o4), (4, l4_ref, s_l4), (16, o16_ref, s_o16), (16, l16_ref, s_l16)):
        for r in range(dil):
            x = src[0, r]
            rows = pl.ds(r, tt // dil, stride=dil)
            dst[0, rows, :] = x[:, :LANE]
            dst[1, rows, :] = jnp.concatenate(
                [x[:, LANE:], jnp.zeros((tt // dil, 2 * LANE - GROUP_WIDTH), F32)], axis=1)

    def whole(scr):
        return jnp.concatenate([scr[0], scr[1][:, :GROUP_WIDTH - LANE]], axis=1)

    outs = (o1_ref[0, 0], whole(s_o4), whole(s_o16))
    lses = (l1_ref[0, 0], whole(s_l4), whole(s_l16))
    mx = jnp.maximum(jnp.maximum(lses[0], lses[1]), lses[2])
    es = [jnp.exp(l - mx) for l in lses]
    den = es[0] + es[1] + es[2]
    pad = jnp.zeros((tt, OD_PAD // 3 - GROUP_WIDTH), F32)
    parts = []
    for g in range(3):
        parts += [outs[g] * (es[g] / den), pad]
    od_ref[0] = jnp.concatenate(parts, axis=1).astype(od_ref.dtype)


def _combine_prompt(o1, l1, o4, l4, o16, l16):
    n, _, s, _ = o1.shape
    tt = MIX_TT

    def spec(dil):
        return pl.BlockSpec((1, dil, tt // dil, GROUP_WIDTH), lambda b, t: (b, 0, t, 0))

    return pl.pallas_call(
        _combine_kernel,
        grid=(n, s // tt),
        in_specs=[spec(1), spec(1), spec(4), spec(4), spec(16), spec(16)],
        out_specs=pl.BlockSpec((1, tt, OD_PAD), lambda b, t: (b, t, 0)),
        out_shape=jax.ShapeDtypeStruct((n, s, OD_PAD), BF16),
        scratch_shapes=[pltpu.VMEM((2, tt, LANE), F32)] * 4,
        compiler_params=_cparams(("arbitrary", "arbitrary")),
        name="combine_prompt",
    )(o1, l1, o4, l4, o16, l16)


def _dec_kernel(z_ref, ha_ref, hc_ref, c1_ref, c4_ref, c16_ref,
                caw_ref, cab_ref, lag_ref, lab_ref, lbg_ref, lbb_ref, sw8_ref, sb_ref, ccw_ref, gq_ref, gk_ref,
                mix_ref, od_ref, na_ref, nc_ref, cv_ref, n1_ref, n4_ref, n16_ref,
                abuf, cbuf, kvbuf, exbuf):
    t_new = z_ref.shape[1]
    hist_a = A_CONV_LEN - 1
    hist_c = C_CONV_LEN - 1

    za = z_ref[0, :, 0:2 * A_WIDTH]
    abuf[0:hist_a, :] = ha_ref[0]
    abuf[hist_a:hist_a + t_new, :] = za[:, :A_WIDTH] * _sigmoid(za[:, A_WIDTH:])
    acc = caw_ref[0:1, :] * abuf[0:t_new, :]
    for j in range(1, A_CONV_LEN):
        acc = acc + caw_ref[j:j + 1, :] * abuf[j:j + t_new, :]
    y = _layernorm_rows(acc + cab_ref[...], lag_ref[...], lab_ref[...])
    mix_ref[0, :, 0:A_WIDTH] = _silu(y).astype(mix_ref.dtype)
    na_ref[0] = abuf[t_new:t_new + hist_a, :]

    gb = _gelu(z_ref[0, :, COL_B:COL_B + 2 * B_WIDTH])
    u = gb[:, :B_WIDTH]
    v = _layernorm_rows(gb[:, B_WIDTH:], lbg_ref[...], lbb_ref[...])
    cv_ref[0] = v
    row = lax.broadcasted_iota(jnp.int32, (t_new, B_WIDTH), 0)
    mixed = sb_ref[0:t_new, :]
    for s in range(t_new):
        mixed = mixed + jnp.where(row >= s, sw8_ref[s], 0.0) * v[s:s + 1, :]
    mix_ref[0, :, A_WIDTH:A_WIDTH + B_WIDTH] = (u * mixed).astype(mix_ref.dtype)

    g_b, g_c, x_c = _split_c(z_ref, slice(None))
    cbuf[0:hist_c, :] = hc_ref[0]
    cbuf[hist_c:hist_c + t_new, :] = g_c * x_c
    conv = ccw_ref[0:1, :] * cbuf[0:t_new, :]
    for j in range(1, C_CONV_LEN):
        conv = conv + ccw_ref[j:j + 1, :] * cbuf[j:j + t_new, :]
    o_c = jnp.concatenate([g_b * conv, jnp.zeros((t_new, ABC_PAD - 2 * A_WIDTH - C_WIDTH), F32)], axis=1)
    mix_ref[0, :, 2 * A_WIDTH:ABC_PAD] = o_c.astype(mix_ref.dtype)
    nc_ref[0] = cbuf[t_new:t_new + hist_c, :]

    q, k, v_d = _split_qkv(z_ref, slice(None))
    qn = _head_norm(q, gq_ref[...] * ATTN_SCALE)
    kn = _head_norm(k, gk_ref[...])
    caches = (c1_ref, c4_ref, c16_ref)
    news = (n1_ref, n4_ref, n16_ref)
    qrow = lax.broadcasted_iota(jnp.int32, (LANE, 2 * GROUP_WIDTH), 0)
    qlane = lax.broadcasted_iota(jnp.int32, (LANE, 2 * GROUP_WIDTH), 1)
    qmask = (qrow >> 3) == (qlane >> 6)
    zeros_q = jnp.zeros((t_new, GROUP_WIDTH), F32)
    lses, dens, offs = [], [], []
    off = 0
    for g, (window, dil) in enumerate(ATTN_GROUPS):
        hs = slice(HEADS_PER_GROUP * g, HEADS_PER_GROUP * (g + 1))
        buf_len = caches[g].shape[1]
        rows = buf_len + t_new
        kv_new = jnp.concatenate(kn[hs] + [v_d[:, GROUP_WIDTH * g:GROUP_WIDTH * (g + 1)]], axis=1)
        kvbuf[off:off + buf_len, :] = caches[g][0]
        kvbuf[off + buf_len:off + rows, :] = kv_new
        news[g][0] = kvbuf[off + t_new:off + rows, :]
        q_g = jnp.concatenate(qn[hs] + [zeros_q], axis=1)
        q_rep = jnp.concatenate([q_g] * HEADS_PER_GROUP
                                + [jnp.zeros((LANE - HEADS_PER_GROUP * t_new, 2 * GROUP_WIDTH), F32)], axis=0)
        q_hi, q_lo = _split_bf16(jnp.where(qmask, q_rep, 0.0))
        kv_hi, kv_lo = _split_bf16(kvbuf[off:off + rows, :])
        nt_dims = (((1,), (1,)), ((), ()))
        sc = (lax.dot_general(kv_hi, q_hi, nt_dims, preferred_element_type=F32)
              + lax.dot_general(kv_lo, q_hi, nt_dims, preferred_element_type=F32)
              + lax.dot_general(kv_hi, q_lo, nt_dims, preferred_element_type=F32))
        krow = lax.broadcasted_iota(jnp.int32, (rows, LANE), 0)
        tok = lax.broadcasted_iota(jnp.int32, (rows, LANE), 1) & (t_new - 1)
        dist = buf_len + tok - krow
        valid = (dist >= 0) & (dist <= dil * WIN_KEYS) & ((dist & (dil - 1)) == 0)
        sc = jnp.where(valid, sc, NEG_INF)
        m = jnp.max(sc, axis=0, keepdims=True)
        ex = jnp.exp(sc - m)
        den = jnp.sum(ex, axis=0, keepdims=True)
        exbuf[off:off + rows, :] = ex
        lses.append(m + jnp.log(den))
        dens.append(den)
        offs.append((off, rows))
        off += rows
    mx = jnp.maximum(jnp.maximum(lses[0], lses[1]), lses[2])
    es = [jnp.exp(l - mx) for l in lses]
    tot = es[0] + es[1] + es[2]
    lane = lax.broadcasted_iota(jnp.int32, (t_new, 2 * GROUP_WIDTH), 1)
    pad = jnp.zeros((t_new, OD_PAD // 3 - GROUP_WIDTH), F32)
    parts = []
    for g in range(3):
        off, rows = offs[g]
        coef = es[g] / (tot * dens[g])
        probs = (exbuf[off:off + rows, :] * coef).astype(BF16)
        kv_all = kvbuf[off:off + rows, :].astype(BF16)
        o_t = lax.dot_general(probs, kv_all, (((0,), (0,)), ((), ())), preferred_element_type=F32)
        o_g = jnp.zeros((t_new, 2 * GROUP_WIDTH), F32)
        for h in range(HEADS_PER_GROUP):
            sel = (lane >= GROUP_WIDTH + HEAD_DIM * h) & (lane < GROUP_WIDTH + HEAD_DIM * (h + 1))
            o_g = o_g + jnp.where(sel, o_t[t_new * h:t_new * (h + 1), :], 0.0)
        parts += [o_g[:, GROUP_WIDTH:], pad]
    od_ref[0] = jnp.concatenate(parts, axis=1).astype(od_ref.dtype)


def _mixers_sample(z, layer, hist_a, hist_c, caches, wts):
    (caw, cab, lag, lab, lbg, lbb, sw8, sb_rep, ccw, gq, gk) = wts
    n, t_new, _ = z.shape

    def lw(shape):
        nd = len(shape)
        return pl.BlockSpec((None,) + shape, lambda b: (layer,) + (0,) * nd)

    def st(shape):
        nd = len(shape)
        return pl.BlockSpec((None, 1) + shape, lambda b: (layer, b) + (0,) * nd)

    lens = [c.shape[2] for c in caches]
    total_rows = sum(lens) + 3 * t_new
    in_specs = [
        pl.BlockSpec((1, t_new, IN_COLS), lambda b: (b, 0, 0)),
        st((A_CONV_LEN - 1, A_WIDTH)), st((C_CONV_LEN - 1, C_WIDTH)),
        st((lens[0], 2 * GROUP_WIDTH)), st((lens[1], 2 * GROUP_WIDTH)), st((lens[2], 2 * GROUP_WIDTH)),
        lw((A_CONV_LEN, A_WIDTH)), lw((1, A_WIDTH)), lw((1, A_WIDTH)), lw((1, A_WIDTH)),
        lw((1, B_WIDTH)), lw((1, B_WIDTH)), lw((t_new, t_new, B_WIDTH)), lw((CHUNK, B_WIDTH)),
        lw((C_CONV_LEN, C_WIDTH)), lw((1, HEAD_DIM)), lw((1, HEAD_DIM)),
    ]

    def ob(shape):
        nd = len(shape)
        return pl.BlockSpec((1,) + shape, lambda b: (b,) + (0,) * nd)

    out_shape = [
        jax.ShapeDtypeStruct((n, t_new, ABC_PAD), F32),
        jax.ShapeDtypeStruct((n, t_new, OD_PAD), F32),
        jax.ShapeDtypeStruct((n, A_CONV_LEN - 1, A_WIDTH), F32),
        jax.ShapeDtypeStruct((n, C_CONV_LEN - 1, C_WIDTH), F32),
        jax.ShapeDtypeStruct((n, t_new, B_WIDTH), F32),
    ] + [jax.ShapeDtypeStruct((n, ln, 2 * GROUP_WIDTH), F32) for ln in lens]
    out_specs = [ob(s.shape[1:]) for s in out_shape]
    return pl.pallas_call(
        _dec_kernel,
        grid=(n,),
        in_specs=in_specs,
        out_specs=out_specs,
        out_shape=out_shape,
        scratch_shapes=[
            pltpu.VMEM((A_CONV_LEN - 1 + t_new + 2, A_WIDTH), F32),
            pltpu.VMEM((16, C_WIDTH), F32),
            pltpu.VMEM((total_rows, 2 * GROUP_WIDTH), F32),
            pltpu.VMEM((total_rows, LANE), F32),
        ],
        compiler_params=_cparams(("arbitrary",)),
        name="mixers_sample",
    )(z, hist_a, hist_c, *caches, caw, cab, lag, lab, lbg, lbb, sw8, sb_rep, ccw, gq, gk)


def _outproj_kernel(h_ref, mix_ref, od_ref, w_ref, wd_ref, o_ref, *, hp):
    acc = _mm(mix_ref[...], w_ref[0:ABC_PAD, :], hp) + _mm(od_ref[...], wd_ref[...], hp)
    o_ref[...] = h_ref[...] + acc


def _outproj(h, mix, od, w_all, wd_all, layer, tm, tn, hp):
    rows = h.shape[0]
    return pl.pallas_call(
        functools.partial(_outproj_kernel, hp=hp),
        grid=(rows // tm, D_MODEL // tn),
        in_specs=[
            pl.BlockSpec((tm, tn), lambda i, j: (i, j)),
            pl.BlockSpec((tm, ABC_PAD), lambda i, j: (i, 0)),
            pl.BlockSpec((tm, OD_PAD), lambda i, j: (i, 0)),
            pl.BlockSpec((None, D_MODEL, tn), lambda i, j: (layer, 0, j)),
            pl.BlockSpec((None, OD_PAD, tn), lambda i, j: (layer, 0, j)),
        ],
        out_specs=pl.BlockSpec((tm, tn), lambda i, j: (i, j)),
        out_shape=jax.ShapeDtypeStruct((rows, D_MODEL), F32),
        compiler_params=_cparams(("arbitrary", "arbitrary")),
        name="outproj",
    )(h, mix, od, w_all, wd_all)


def _top2_in_top_group(logits):
    rows = logits.shape[0]
    lane = lax.broadcasted_iota(jnp.int32, (rows, ROUTER_COLS), 1)
    big = jnp.int32(ROUTER_COLS)
    is_grp = (lane >= N_EXPERTS) & (lane < N_EXPERTS + MOE_GROUPS)
    gl = jnp.where(is_grp, logits, NEG_INF)
    gmax = jnp.max(gl, axis=-1, keepdims=True)
    gidx = jnp.min(jnp.where(gl == gmax, lane - N_EXPERTS, big), axis=-1, keepdims=True)
    gate = 1.0 / jnp.sum(jnp.where(is_grp, jnp.exp(gl - gmax), 0.0), axis=-1, keepdims=True)
    in_grp = (lane < N_EXPERTS) & ((lane >> 2) == gidx)
    el = jnp.where(in_grp, logits, NEG_INF)
    t1 = jnp.max(el, axis=-1, keepdims=True)
    i1 = jnp.min(jnp.where(el == t1, lane, big), axis=-1, keepdims=True)
    el2 = jnp.where(lane == i1, NEG_INF, el)
    t2 = jnp.max(el2, axis=-1, keepdims=True)
    i2 = jnp.min(jnp.where(el2 == t2, lane, big), axis=-1, keepdims=True)
    e2 = jnp.exp(t2 - t1)
    return lane, i1, i2, gate / (1.0 + e2), gate * e2 / (1.0 + e2)


META_E1, META_E2, META_R1, META_R2, META_W1, META_W2 = range(6)


def _router_sparse_kernel(h_ref, g_ref, wr_ref, br_ref, m_ref, meta_ref, cnt_ref, carry_ref):
    @pl.when(pl.program_id(0) == 0)
    def _():
        carry_ref[...] = jnp.zeros_like(carry_ref)

    m = _rms_rows(h_ref[...], g_ref[...])
    m_ref[...] = m
    logits = _mm(m, wr_ref[...], False) + br_ref[...]
    rows = logits.shape[0]
    lane, i1, i2, w1, w2 = _top2_in_top_group(logits)
    sel = jnp.where((lane == i1) | (lane == i2), 1.0, 0.0)
    r = lax.broadcasted_iota(jnp.int32, (rows, rows), 0)
    c = lax.broadcasted_iota(jnp.int32, (rows, rows), 1)
    earlier = jnp.where(c < r, 1.0, 0.0).astype(BF16)
    rank = jnp.dot(earlier, sel.astype(BF16), preferred_element_type=F32) + carry_ref[...]
    r1 = jnp.sum(jnp.where(lane == i1, rank, 0.0), axis=-1, keepdims=True)
    r2 = jnp.sum(jnp.where(lane == i2, rank, 0.0), axis=-1, keepdims=True)
    carry_ref[...] += jnp.sum(sel, axis=0, keepdims=True)
    cnt_ref[...] = carry_ref[...]
    meta = jnp.zeros((rows, ROUTER_COLS), F32)
    for pos, val in ((META_E1, i1.astype(F32)), (META_E2, i2.astype(F32)), (META_R1, r1), (META_R2, r2),
                     (META_W1, w1), (META_W2, w2)):
        meta = jnp.where(lane == pos, val, meta)
    meta_ref[...] = meta


def _router_sparse(h, g_all, wr_all, br_all, layer, tm):
    rows = h.shape[0]
    return pl.pallas_call(
        _router_sparse_kernel,
        grid=(rows // tm,),
        in_specs=[
            pl.BlockSpec((tm, D_MODEL), lambda i: (i, 0)),
            pl.BlockSpec((None, 1, D_MODEL), lambda i: (layer, 0, 0)),
            pl.BlockSpec((None, D_MODEL, ROUTER_COLS), lambda i: (layer, 0, 0)),
            pl.BlockSpec((None, 1, ROUTER_COLS), lambda i: (layer, 0, 0)),
        ],
        out_specs=[pl.BlockSpec((tm, D_MODEL), lambda i: (i, 0)),
                   pl.BlockSpec((tm, ROUTER_COLS), lambda i: (i, 0)),
                   pl.BlockSpec((1, ROUTER_COLS), lambda i: (0, 0))],
        out_shape=[jax.ShapeDtypeStruct((rows, D_MODEL), F32),
                   jax.ShapeDtypeStruct((rows, ROUTER_COLS), F32),
                   jax.ShapeDtypeStruct((1, ROUTER_COLS), F32)],
        scratch_shapes=[pltpu.VMEM((1, ROUTER_COLS), F32)],
        compiler_params=_cparams(("arbitrary",)),
        name="router_sparse",
    )(h, g_all, wr_all, br_all)


EXPERT_TILE = 256


def _experts_kernel(te_ref, tv_ref, s0_ref, s1_ref, nu_ref,
                    m_hbm, wg_ref, wu_ref, wd_ref, yk_hbm,
                    src_ref, xbuf, ybuf, wgb, wub, wdb, gsem, ssem, *, n_tok):
    i = pl.program_id(0)
    tile = EXPERT_TILE
    nused = nu_ref[0]
    slot = lax.rem(i, 2)

    def rows_moved(t):
        return pl.multiple_of(((tv_ref[t] + SUBLANE - 1) // SUBLANE) * SUBLANE, SUBLANE)

    def gather_copy(code, j, b):
        row = jnp.minimum(code >> 1, n_tok - 1)
        return pltpu.make_async_copy(m_hbm.at[pl.ds(row, 1), :], xbuf.at[b, pl.ds(j, 1), :], gsem.at[b])

    def scatter_copy(code, j, b):
        return pltpu.make_async_copy(ybuf.at[b, pl.ds(j, 1), :], yk_hbm.at[code & 1, pl.ds(code >> 1, 1), :],
                                     ssem.at[b])

    def start_rows(copy_fn, t, b):
        def body(j, carry):
            copy_fn(src_ref[t * tile + j], j, b).start()
            return carry
        lax.fori_loop(0, rows_moved(t), body, 0)

    def wait_gather(t, b):
        n = rows_moved(t)
        pltpu.make_async_copy(m_hbm.at[pl.ds(0, n), :], xbuf.at[b, pl.ds(0, n), :], gsem.at[b]).wait()

    def wait_scatter(t, b):
        n = rows_moved(t)
        pltpu.make_async_copy(ybuf.at[b, pl.ds(0, n), :], yk_hbm.at[0, pl.ds(0, n), :], ssem.at[b]).wait()

    @pl.when(i == 0)
    def _():
        def fill(t, carry):
            src_ref[s0_ref[t]] = 2 * t
            src_ref[s1_ref[t]] = 2 * t + 1
            return carry
        lax.fori_loop(0, n_tok, fill, 0)

        def fill_pad(t, carry):
            def one(j, c):
                src_ref[t * tile + j] = 2 * (n_tok + SUBLANE * lax.rem(t, 2) + lax.rem(j, SUBLANE))
                return c
            lax.fori_loop(tv_ref[t], rows_moved(t), one, 0)
            return carry
        lax.fori_loop(0, nused, fill_pad, 0)
        xbuf[...] = jnp.zeros_like(xbuf)
        for k in range(2):
            spare = pltpu.make_async_copy(xbuf.at[0, pl.ds(0, 2 * SUBLANE), :],
                                          yk_hbm.at[k, pl.ds(n_tok, 2 * SUBLANE), :], ssem.at[0])
            spare.start()
            spare.wait()
        start_rows(gather_copy, 0, 0)

    @pl.when(i < nused)
    def _():
        @pl.when(i + 1 < nused)
        def _():
            start_rows(gather_copy, i + 1, 1 - slot)

        changed = jnp.logical_or(i == 0, te_ref[i] != te_ref[jnp.maximum(i - 1, 0)])

        @pl.when(changed)
        def _():
            wgb[...] = wg_ref[...].astype(BF16)
            wub[...] = wu_ref[...].astype(BF16)
            wdb[...] = wd_ref[...].astype(BF16)

        wait_gather(i, slot)

        @pl.when(i >= 2)
        def _():
            wait_scatter(i - 2, slot)

        x = xbuf[slot].astype(BF16)
        gate = jnp.dot(x, wgb[...], preferred_element_type=F32)
        up = jnp.dot(x, wub[...], preferred_element_type=F32)
        ybuf[slot] = jnp.dot((_silu(gate) * up).astype(BF16), wdb[...], preferred_element_type=F32)
        start_rows(scatter_copy, i, slot)

        @pl.when(i == nused - 1)
        def _():
            @pl.when(i >= 1)
            def _():
                wait_scatter(i - 1, 1 - slot)
            wait_scatter(i, slot)


def _experts_sparse(m, tile_expert, tile_rows, slot0, slot1, nused, wg_all, wu_all, wd_all, layer):
    n_tok = m.shape[0]
    n_tiles = tile_expert.shape[0]

    def wspec(shape):
        return pl.BlockSpec((None, None) + shape, lambda i, te, tv, s0, s1, nu: (layer, te[i], 0, 0))

    grid_spec = pltpu.PrefetchScalarGridSpec(
        num_scalar_prefetch=5,
        grid=(n_tiles,),
        in_specs=[pl.BlockSpec(memory_space=pl.ANY),
                  wspec((D_MODEL, EXPERT_FF)), wspec((D_MODEL, EXPERT_FF)), wspec((EXPERT_FF, D_MODEL))],
        out_specs=pl.BlockSpec(memory_space=pl.ANY),
        scratch_shapes=[
            pltpu.SMEM((n_tiles * EXPERT_TILE,), jnp.int32),
            pltpu.VMEM((2, EXPERT_TILE, D_MODEL), F32),
            pltpu.VMEM((2, EXPERT_TILE, D_MODEL), F32),
            pltpu.VMEM((D_MODEL, EXPERT_FF), BF16),
            pltpu.VMEM((D_MODEL, EXPERT_FF), BF16),
            pltpu.VMEM((EXPERT_FF, D_MODEL), BF16),
            pltpu.SemaphoreType.DMA((2,)),
            pltpu.SemaphoreType.DMA((2,)),
        ],
    )
    return pl.pallas_call(
        functools.partial(_experts_kernel, n_tok=n_tok),
        grid_spec=grid_spec,
        out_shape=jax.ShapeDtypeStruct((2, n_tok + 2 * SUBLANE, D_MODEL), F32),
        compiler_params=_cparams(("arbitrary",)),
        name="experts_sparse",
    )(tile_expert, tile_rows, slot0, slot1, nused, m, wg_all, wu_all, wd_all)


def _expert_plan(meta, cnt, n_tiles):
    tile = EXPERT_TILE
    counts = cnt[0, :N_EXPERTS].astype(jnp.int32)
    padded = ((counts + tile - 1) // tile) * tile
    ends = jnp.cumsum(padded)
    base = ends - padded
    e1 = meta[:, META_E1].astype(jnp.int32)
    e2 = meta[:, META_E2].astype(jnp.int32)
    slot0 = base[e1] + meta[:, META_R1].astype(jnp.int32)
    slot1 = base[e2] + meta[:, META_R2].astype(jnp.int32)
    start = jnp.arange(n_tiles, dtype=jnp.int32) * tile
    expert_of = jnp.sum((start[:, None] >= ends[None, :]).astype(jnp.int32), axis=1)
    last_used = jnp.max(jnp.where(counts > 0, jnp.arange(N_EXPERTS, dtype=jnp.int32), 0))
    tile_expert = jnp.minimum(expert_of, last_used)
    e_clamped = jnp.minimum(expert_of, N_EXPERTS - 1)
    tile_rows = jnp.clip(counts[e_clamped] - (start - base[e_clamped]), 0, tile)
    tile_rows = jnp.where(expert_of < N_EXPERTS, tile_rows, 0).astype(jnp.int32)
    nused = (ends[-1] // tile).astype(jnp.int32).reshape(1)
    return tile_expert, tile_rows, slot0, slot1, nused


def _moe_combine_kernel(h_ref, y0_ref, y1_ref, meta_ref, o_ref):
    meta = meta_ref[...]
    lane = lax.broadcasted_iota(jnp.int32, meta.shape, 1)
    w1 = jnp.sum(jnp.where(lane == META_W1, meta, 0.0), axis=-1, keepdims=True)
    w2 = jnp.sum(jnp.where(lane == META_W2, meta, 0.0), axis=-1, keepdims=True)
    o_ref[...] = h_ref[...] + w1 * y0_ref[...] + w2 * y1_ref[...]


def _moe_combine(h, yk, meta, tm):
    rows = h.shape[0]
    return pl.pallas_call(
        _moe_combine_kernel,
        grid=(rows // tm,),
        in_specs=[
            pl.BlockSpec((tm, D_MODEL), lambda i: (i, 0)),
            pl.BlockSpec((None, tm, D_MODEL), lambda i: (0, i, 0)),
            pl.BlockSpec((None, tm, D_MODEL), lambda i: (1, i, 0)),
            pl.BlockSpec((tm, ROUTER_COLS), lambda i: (i, 0)),
        ],
        out_specs=pl.BlockSpec((tm, D_MODEL), lambda i: (i, 0)),
        out_shape=jax.ShapeDtypeStruct((rows, D_MODEL), F32),
        compiler_params=_cparams(("arbitrary",)),
        name="moe_combine",
    )(h, yk, yk, meta)


def _router_kernel(h_ref, g_ref, wr_ref, br_ref, m_ref, comb_ref, *, hp):
    m = _rms_rows(h_ref[...], g_ref[...]).astype(m_ref.dtype)
    m_ref[...] = m
    logits = _mm(m, wr_ref[...], hp) + br_ref[...]
    rows = logits.shape[0]
    lane = lax.broadcasted_iota(jnp.int32, (rows, ROUTER_COLS), 1)
    big = jnp.int32(ROUTER_COLS)
    is_grp = (lane >= N_EXPERTS) & (lane < N_EXPERTS + MOE_GROUPS)
    gl = jnp.where(is_grp, logits, NEG_INF)
    gmax = jnp.max(gl, axis=-1, keepdims=True)
    gidx = jnp.min(jnp.where(gl == gmax, lane - N_EXPERTS, big), axis=-1, keepdims=True)
    gate = 1.0 / jnp.sum(jnp.where(is_grp, jnp.exp(gl - gmax), 0.0), axis=-1, keepdims=True)
    in_grp = (lane < N_EXPERTS) & ((lane >> 2) == gidx)
    el = jnp.where(in_grp, logits, NEG_INF)
    t1 = jnp.max(el, axis=-1, keepdims=True)
    i1 = jnp.min(jnp.where(el == t1, lane, big), axis=-1, keepdims=True)
    el2 = jnp.where(lane == i1, NEG_INF, el)
    t2 = jnp.max(el2, axis=-1, keepdims=True)
    i2 = jnp.min(jnp.where(el2 == t2, lane, big), axis=-1, keepdims=True)
    e2 = jnp.exp(t2 - t1)
    w1 = gate / (1.0 + e2)
    w2 = gate * e2 / (1.0 + e2)
    comb_ref[...] = jnp.where(lane == i1, w1, 0.0) + jnp.where(lane == i2, w2, 0.0)


def _router(h, g_all, wr_all, br_all, layer, tm, hp):
    rows = h.shape[0]
    return pl.pallas_call(
        functools.partial(_router_kernel, hp=hp),
        grid=(rows // tm,),
        in_specs=[
            pl.BlockSpec((tm, D_MODEL), lambda i: (i, 0)),
            pl.BlockSpec((None, 1, D_MODEL), lambda i: (layer, 0, 0)),
            pl.BlockSpec((None, D_MODEL, ROUTER_COLS), lambda i: (layer, 0, 0)),
            pl.BlockSpec((None, 1, ROUTER_COLS), lambda i: (layer, 0, 0)),
        ],
        out_specs=[pl.BlockSpec((tm, D_MODEL), lambda i: (i, 0)),
                   pl.BlockSpec((tm, ROUTER_COLS), lambda i: (i, 0))],
        out_shape=[jax.ShapeDtypeStruct((rows, D_MODEL), _act_dtype(hp)),
                   jax.ShapeDtypeStruct((rows, ROUTER_COLS), F32)],
        compiler_params=_cparams(("arbitrary",)),
        name="router",
    )(h, g_all, wr_all, br_all)


def _moe_kernel(h_ref, m_ref, comb_ref, wg_ref, wu_ref, wd_ref, o_ref, *, hp):
    e = pl.program_id(1)

    @pl.when(e == 0)
    def _():
        o_ref[...] = h_ref[...]

    x = m_ref[...]
    gate = _mm(x, wg_ref[...], hp)
    up = _mm(x, wu_ref[...], hp)
    lane = lax.broadcasted_iota(jnp.int32, comb_ref.shape, 1)
    w = jnp.sum(jnp.where(lane == e, comb_ref[...], 0.0), axis=-1, keepdims=True)
    o_ref[...] += _mm(_silu(gate) * up * w, wd_ref[...], hp)


def _moe_dense(h, m, comb, wg_all, wu_all, wd_all, layer, tm, hp):
    rows = h.shape[0]
    return pl.pallas_call(
        functools.partial(_moe_kernel, hp=hp),
        grid=(rows // tm, N_EXPERTS),
        in_specs=[
            pl.BlockSpec((tm, D_MODEL), lambda i, e: (i, 0)),
            pl.BlockSpec((tm, D_MODEL), lambda i, e: (i, 0)),
            pl.BlockSpec((tm, ROUTER_COLS), lambda i, e: (i, 0)),
            pl.BlockSpec((None, None, D_MODEL, EXPERT_FF), lambda i, e: (layer, e, 0, 0)),
            pl.BlockSpec((None, None, D_MODEL, EXPERT_FF), lambda i, e: (layer, e, 0, 0)),
            pl.BlockSpec((None, None, EXPERT_FF, D_MODEL), lambda i, e: (layer, e, 0, 0)),
        ],
        out_specs=pl.BlockSpec((tm, D_MODEL), lambda i, e: (i, 0)),
        out_shape=jax.ShapeDtypeStruct((rows, D_MODEL), F32),
        compiler_params=_cparams(("arbitrary", "arbitrary")),
        name="moe_dense",
    )(h, m, comb, wg_all, wu_all, wd_all)


def _ple_kernel(h_ref, hc_ref, g_ref, p_ref, wg_ref, wp_ref, o_ref, xn_ref, *, hp):
    @pl.when(pl.program_id(1) == 0)
    def _():
        xn_ref[...] = _rms_rows(h_ref[...], g_ref[...]).astype(xn_ref.dtype)

    gate = _sigmoid(_mm(xn_ref[...], wg_ref[...], hp))
    o_ref[...] = hc_ref[...] + gate * _mm(p_ref[...], wp_ref[...], hp)


def _ple(h, p_all, g_all, wg_all, wp_all, layer, tm, tn, hp):
    rows = h.shape[0]
    return pl.pallas_call(
        functools.partial(_ple_kernel, hp=hp),
        grid=(rows // tm, D_MODEL // tn),
        in_specs=[
            pl.BlockSpec((tm, D_MODEL), lambda i, j: (i, 0)),
            pl.BlockSpec((tm, tn), lambda i, j: (i, j)),
            pl.BlockSpec((None, 1, D_MODEL), lambda i, j: (layer, 0, 0)),
            pl.BlockSpec((None, tm, PLE_DIM), lambda i, j: (layer, i, 0)),
            pl.BlockSpec((None, D_MODEL, tn), lambda i, j: (layer, 0, j)),
            pl.BlockSpec((None, PLE_DIM, tn), lambda i, j: (layer, 0, j)),
        ],
        out_specs=pl.BlockSpec((tm, tn), lambda i, j: (i, j)),
        out_shape=jax.ShapeDtypeStruct((rows, D_MODEL), F32),
        scratch_shapes=[pltpu.VMEM((tm, D_MODEL), _act_dtype(hp))],
        compiler_params=_cparams(("arbitrary", "arbitrary")),
        name="ple",
    )(h, h, g_all, p_all, wg_all, wp_all)


def _token_tail(h, mix, od, p_all, layer, tw, tm, hp):
    (w_out, w_out_d, g_ffn, w_router, b_router, w_gate, w_up, w_down, g_ple, w_ple_gate, w_ple_proj) = tw
    h = _outproj(h, mix, od, w_out, w_out_d, layer, tm, 512, hp)
    if hp:
        m, comb = _router(h, g_ffn, w_router, b_router, layer, tm, hp)
        h = _moe_dense(h, m, comb, w_gate, w_up, w_down, layer, tm, hp)
    else:
        rows = h.shape[0]
        n_tiles = (2 * rows + N_EXPERTS * (EXPERT_TILE - 1) + EXPERT_TILE - 1) // EXPERT_TILE
        m, meta, cnt = _router_sparse(h, g_ffn, w_router, b_router, layer, 512)
        plan = _expert_plan(meta, cnt, n_tiles)
        yk = _experts_sparse(m, *plan, w_gate, w_up, w_down, layer)
        h = _moe_combine(h, yk, meta, 512)
    return _ple(h, p_all, g_ple, w_ple_gate, w_ple_proj, layer, tm, 512, hp)


TAIL = 8


def _patch_kernel(x_ref, tail_ref, o_ref):
    del x_ref
    o_ref[...] = tail_ref[...]


def _patch_tail(x, tail):
    n, s, width = x.shape
    return pl.pallas_call(
        _patch_kernel,
        grid=(n,),
        in_specs=[pl.BlockSpec(memory_space=pl.ANY),
                  pl.BlockSpec((1, TAIL, width), lambda b: (b, 0, 0))],
        out_specs=pl.BlockSpec((1, TAIL, width), lambda b: (b, s // TAIL - 1, 0)),
        out_shape=jax.ShapeDtypeStruct(x.shape, x.dtype),
        input_output_aliases={0: 0},
        compiler_params=_cparams(("arbitrary",)),
        name="patch_tail",
    )(x, tail)


def kernel(x_prompt, x_sample, p_prompt, p_sample, state_conv_a, state_conv_c, cache_kv_w128, cache_kv_w512, cache_kv_w2048, g_mix, w_in, conv_a_w, conv_a_b, ln_a_g, ln_a_b, ln_b_g, ln_b_b, sgu_w, sgu_b, conv_c_w, g_q, g_k, w_out, g_ffn, w_router_grp, b_router_grp, w_router_exp, b_router_exp, w_gate, w_up, w_down, g_ple, w_ple_gate, w_ple_proj):
    n_p, s_p, _ = x_prompt.shape
    n_s, t_s, _ = x_sample.shape
    rows_p = n_p * s_p
    rows_s = n_s * t_s

    def row3(a):
        return a.reshape(DEPTH, 1, a.shape[-1])

    g_mix3, g_ffn3, g_ple3 = row3(g_mix), row3(g_ffn), row3(g_ple)
    cab3, lag3, lab3, lbg3, lbb3 = row3(conv_a_b), row3(ln_a_g), row3(ln_a_b), row3(ln_b_g), row3(ln_b_b)
    gq3, gk3 = row3(g_q), row3(g_k)
    sb_rep = jnp.repeat(jnp.swapaxes(sgu_b, 1, 2), HEAD_DIM, axis=2)
    sw8 = jnp.repeat(jnp.transpose(sgu_w[:, :, :t_s, :t_s], (0, 3, 2, 1)), HEAD_DIM, axis=3)
    w_out_d = jnp.pad(w_out[:, ABC_PAD - 64:].reshape(DEPTH, 3, GROUP_WIDTH, D_MODEL),
                      ((0, 0), (0, 0), (0, OD_PAD // 3 - GROUP_WIDTH), (0, 0))).reshape(DEPTH, OD_PAD, D_MODEL)
    w_router = jnp.concatenate(
        [jnp.transpose(w_router_exp, (0, 2, 1, 3)).reshape(DEPTH, D_MODEL, N_EXPERTS), w_router_grp,
         jnp.zeros((DEPTH, D_MODEL, ROUTER_COLS - N_EXPERTS - MOE_GROUPS), F32)], axis=2)
    b_router = jnp.concatenate(
        [b_router_exp.reshape(DEPTH, N_EXPERTS), b_router_grp,
         jnp.zeros((DEPTH, ROUTER_COLS - N_EXPERTS - MOE_GROUPS), F32)], axis=1).reshape(DEPTH, 1, ROUTER_COLS)
    tail_w = (w_out, w_out_d, g_ffn3, w_router, b_router, w_gate, w_up, w_down, g_ple3, w_ple_gate, w_ple_proj)
    mix_w_p = (conv_a_w, cab3, lag3, lab3, lbg3, lbb3, sgu_w, sb_rep, conv_c_w, gq3, gk3)
    mix_w_s = (conv_a_w, cab3, lag3, lab3, lbg3, lbb3, sw8, sb_rep, conv_c_w, gq3, gk3)

    caches = [c.reshape(c.shape[0], c.shape[1], c.shape[2], 2 * GROUP_WIDTH)
              for c in (cache_kv_w128, cache_kv_w512, cache_kv_w2048)]
    p_p = p_prompt.reshape(DEPTH, rows_p, PLE_DIM)
    rows_t = n_p * TAIL
    rows_h = rows_s + rows_t
    p_h = jnp.concatenate([p_sample.reshape(DEPTH, rows_s, PLE_DIM),
                           p_prompt[:, :, s_p - TAIL:].reshape(DEPTH, rows_t, PLE_DIM)], axis=1)
    h_h = jnp.concatenate([x_sample.reshape(rows_s, D_MODEL),
                           x_prompt[:, s_p - TAIL:].reshape(rows_t, D_MODEL)], axis=0)

    h = x_prompt.reshape(rows_p, D_MODEL)
    st_a, st_c, st_kv = [], [], [[], [], []]
    sa, sc, sv, skv = [], [], [], [[], [], []]
    for i in range(DEPTH):
        z_h = _inproj(h_h, g_mix3, w_in, i, rows_h, 512, hp=True)
        z = _inproj(h, g_mix3, w_in, i, 1024, 512).reshape(n_p, s_p, IN_COLS)
        z = _patch_tail(z, z_h[rows_s:].reshape(n_p, TAIL, IN_COLS))
        (mix, q1, kv1, q4, kv4, q16, kv16, sta, stc, s1, s4, s16) = _mixers_prompt(z, i, mix_w_p)
        o1, l1 = _attn_prompt(q1, kv1)
        o4, l4 = _attn_prompt(q4, kv4)
        o16, l16 = _attn_prompt(q16, kv16)
        od = _combine_prompt(o1, l1, o4, l4, o16, l16)
        (mix_s, od_s, na, nc, cv, n1, n4, n16) = _mixers_sample(
            z_h[:rows_s].reshape(n_s, t_s, IN_COLS), i, state_conv_a, state_conv_c, caches, mix_w_s)
        mix_h = jnp.concatenate([mix_s.reshape(rows_s, ABC_PAD),
                                 mix[:, s_p - TAIL:].reshape(rows_t, ABC_PAD).astype(F32)], axis=0)
        od_h = jnp.concatenate([od_s.reshape(rows_s, OD_PAD),
                                od[:, s_p - TAIL:].reshape(rows_t, OD_PAD).astype(F32)], axis=0)
        h = _token_tail(h, mix.reshape(rows_p, ABC_PAD), od.reshape(rows_p, OD_PAD), p_p, i, tail_w, 1024, False)
        h_h = _token_tail(h_h, mix_h, od_h, p_h, i, tail_w, rows_h, True)
        st_a.append(sta[:, A_HALO - (A_CONV_LEN - 1):])
        st_c.append(stc[:, C_HALO - (C_CONV_LEN - 1):])
        for g, s_kv in enumerate((s1, s4, s16)):
            st_kv[g].append(s_kv.reshape(n_p, s_kv.shape[1], 2, HEADS_PER_GROUP, HEAD_DIM))
        sa.append(na)
        sc.append(nc)
        sv.append(cv)
        for g, nk in enumerate((n1, n4, n16)):
            skv[g].append(nk.reshape(n_s, nk.shape[1], 2, HEADS_PER_GROUP, HEAD_DIM))
    y_prompt = _patch_tail(h.reshape(n_p, s_p, D_MODEL), h_h[rows_s:].reshape(n_p, TAIL, D_MODEL))
    conv_a_prompt = jnp.stack(st_a)
    conv_c_prompt = jnp.stack(st_c)
    kv_prompt = [jnp.stack(s) for s in st_kv]
    y_sample = h_h[:rows_s].reshape(n_s, t_s, D_MODEL)
    conv_a_sample = jnp.stack(sa)
    conv_c_sample = jnp.stack(sc)
    chunk_v_sample = jnp.stack(sv)
    kv_sample = [jnp.stack(s) for s in skv]

    return (y_prompt, y_sample, conv_a_prompt, conv_a_sample, conv_c_prompt, conv_c_sample, chunk_v_sample,
            kv_prompt[0], kv_sample[0], kv_prompt[1], kv_sample[1], kv_prompt[2], kv_sample[2])
```

```python
import functools

import jax
import jax.numpy as jnp
from jax import lax
from jax.experimental import pallas as pl
from jax.experimental.pallas import tpu as pltpu

F32 = jnp.float32
BF16 = jnp.bfloat16

D_MODEL = 2048
DEPTH = 4
PLE_DIM = 256
HEAD_DIM = 64
A_WIDTH = 512
A_CONV_LEN = 31
B_WIDTH = 512
B_HEADS = 8
CHUNK = 128
C_WIDTH = 448
C_CONV_LEN = 3
ATTN_GROUPS = ((128, 1), (512, 4), (2048, 16))
HEADS_PER_GROUP = 3
WIN_KEYS = 128
D_HEADS = 9
D_WIDTH = D_HEADS * HEAD_DIM
GROUP_WIDTH = HEADS_PER_GROUP * HEAD_DIM
IN_COLS = 2 * A_WIDTH + 2 * B_WIDTH + 3 * C_WIDTH + 3 * D_WIDTH
COL_B = 2 * A_WIDTH
COL_C = COL_B + 2 * B_WIDTH
COL_D = COL_C + 3 * C_WIDTH
ATTN_SCALE = HEAD_DIM ** -0.5
MOE_GROUPS = 4
EXPERTS_PER_GROUP = 4
N_EXPERTS = 16
EXPERT_FF = 512
RMS_EPS = 1e-6
LN_EPS = 1e-5

LANE = 128
SUBLANE = 8
ABC_PAD = 1536
OD_PAD = 3 * 256
ROUTER_COLS = 128
VMEM_LIMIT = 56 * 1024 * 1024
NEG_INF = float("-inf")


def _cparams(sem):
    return pltpu.CompilerParams(dimension_semantics=sem, vmem_limit_bytes=VMEM_LIMIT)


def _rms_rows(x, g):
    return x * lax.rsqrt(jnp.mean(x * x, axis=-1, keepdims=True) + RMS_EPS) * g


def _layernorm_rows(x, g, b):
    mu = jnp.mean(x, axis=-1, keepdims=True)
    xc = x - mu
    var = jnp.mean(xc * xc, axis=-1, keepdims=True)
    return xc * lax.rsqrt(var + LN_EPS) * g + b


def _sigmoid(x):
    return 1.0 / (1.0 + jnp.exp(-x))


def _silu(x):
    return x * _sigmoid(x)


def _gelu(x):
    return 0.5 * x * (1.0 + lax.erf(x * (2.0 ** -0.5)))


def _split_bf16(x):
    hi = x.astype(BF16)
    return hi, (x - hi.astype(F32)).astype(BF16)


def _mm(x, w, hp):
    if not hp:
        return jnp.dot(x.astype(BF16), w.astype(BF16), preferred_element_type=F32)
    rows = x.shape[0]
    xh, xl = _split_bf16(x)
    wh, wl = _split_bf16(w)
    r = jnp.dot(jnp.concatenate([xh, xl], axis=0), wh, preferred_element_type=F32)
    return r[:rows] + r[rows:] + jnp.dot(xh, wl, preferred_element_type=F32)


def _act_dtype(hp):
    return F32 if hp else BF16


def _head_norm(x, g):
    outs = []
    for h in range(D_HEADS):
        xh = x[:, HEAD_DIM * h:HEAD_DIM * (h + 1)]
        outs.append(_rms_rows(xh, g))
    return outs


def _split_qkv(z_ref, rows):
    q_lo = (COL_D // LANE) * LANE
    zq = z_ref[0, rows, q_lo:q_lo + 640]
    q = zq[:, COL_D - q_lo:COL_D - q_lo + D_WIDTH]
    k_lo = COL_D + D_WIDTH
    zk = z_ref[0, rows, k_lo:k_lo + 640]
    k = zk[:, :D_WIDTH]
    v_lo = ((COL_D + 2 * D_WIDTH) // LANE) * LANE
    zv = z_ref[0, rows, v_lo:v_lo + 640]
    v = zv[:, COL_D + 2 * D_WIDTH - v_lo:]
    return q, k, v


def _split_c(z_ref, rows):
    zc = z_ref[0, rows, COL_C:COL_C + 1408]
    return zc[:, 0:C_WIDTH], zc[:, C_WIDTH:2 * C_WIDTH], zc[:, 2 * C_WIDTH:3 * C_WIDTH]


def _inproj_kernel(x_ref, g_ref, w_ref, o_ref, xn_ref, *, hp):
    @pl.when(pl.program_id(1) == 0)
    def _():
        xn_ref[...] = _rms_rows(x_ref[...], g_ref[...]).astype(xn_ref.dtype)

    o_ref[...] = _mm(xn_ref[...], w_ref[...], hp)


def _inproj(h, g_all, w_all, layer, tm, tn, hp=False):
    rows = h.shape[0]
    return pl.pallas_call(
        functools.partial(_inproj_kernel, hp=hp),
        grid=(rows // tm, IN_COLS // tn),
        in_specs=[
            pl.BlockSpec((tm, D_MODEL), lambda i, j: (i, 0)),
            pl.BlockSpec((None, 1, D_MODEL), lambda i, j: (layer, 0, 0)),
            pl.BlockSpec((None, D_MODEL, tn), lambda i, j: (layer, 0, j)),
        ],
        out_specs=pl.BlockSpec((tm, tn), lambda i, j: (i, j)),
        out_shape=jax.ShapeDtypeStruct((rows, IN_COLS), F32),
        scratch_shapes=[pltpu.VMEM((tm, D_MODEL), _act_dtype(hp))],
        compiler_params=_cparams(("arbitrary", "arbitrary")),
        name="inproj",
    )(h, g_all, w_all)


MIX_TT = 256
CONV_ROWS = 64
A_HALO = 32
C_HALO = 8


def _pair_weights(sw_ref, wp_ref):
    row = lax.broadcasted_iota(jnp.int32, (CHUNK, CHUNK), 0)
    col = lax.broadcasted_iota(jnp.int32, (CHUNK, CHUNK), 1)
    keep = col <= row
    for p in range(B_HEADS // 2):
        w0 = jnp.where(keep, sw_ref[2 * p], 0.0)
        w1 = jnp.where(keep, sw_ref[2 * p + 1], 0.0)
        wp_ref[p] = jnp.concatenate([w0, w1], axis=1).astype(BF16)


def _mixer_kernel(z_ref, caw_ref, cab_ref, lag_ref, lab_ref, lbg_ref, lbb_ref, sw_ref, sb_ref, ccw_ref,
                  gq_ref, gk_ref,
                  mix_ref, q1_ref, kv1_ref, q4_ref, kv4_ref, q16_ref, kv16_ref,
                  sta_ref, stc_ref, st1_ref, st4_ref, st16_ref,
                  abuf, cbuf, wp_ref, qs_ref, kvs_ref):
    t = pl.program_id(1)
    tt = MIX_TT

    @pl.when(t == 0)
    def _():
        abuf[0:A_HALO, :] = jnp.zeros((A_HALO, A_WIDTH), F32)
        cbuf[0:C_HALO, :] = jnp.zeros((C_HALO, C_WIDTH), F32)
        _pair_weights(sw_ref, wp_ref)

    @pl.when(t > 0)
    def _():
        abuf[0:A_HALO, :] = abuf[tt:tt + A_HALO, :]
        cbuf[0:C_HALO, :] = cbuf[tt:tt + C_HALO, :]

    za = z_ref[0, :, 0:2 * A_WIDTH]
    abuf[A_HALO:A_HALO + tt, :] = za[:, :A_WIDTH] * _sigmoid(za[:, A_WIDTH:])
    sta_ref[0] = abuf[tt:tt + A_HALO, :]
    base = A_HALO - (A_CONV_LEN - 1)
    for r0 in range(0, tt, CONV_ROWS):
        acc = caw_ref[0:1, :] * abuf[r0 + base:r0 + base + CONV_ROWS, :]
        for j in range(1, A_CONV_LEN):
            acc = acc + caw_ref[j:j + 1, :] * abuf[r0 + base + j:r0 + base + j + CONV_ROWS, :]
        y = _layernorm_rows(acc + cab_ref[...], lag_ref[...], lab_ref[...])
        mix_ref[0, r0:r0 + CONV_ROWS, 0:A_WIDTH] = _silu(y).astype(mix_ref.dtype)

    lane = lax.broadcasted_iota(jnp.int32, (CHUNK, LANE), 1)
    for c0 in range(0, tt, CHUNK):
        gb = _gelu(z_ref[0, c0:c0 + CHUNK, COL_B:COL_B + 2 * B_WIDTH])
        u = gb[:, :B_WIDTH]
        v = _layernorm_rows(gb[:, B_WIDTH:], lbg_ref[...], lbb_ref[...])
        pieces = []
        for p in range(B_HEADS // 2):
            v128 = v[:, LANE * p:LANE * (p + 1)]
            rhs = jnp.concatenate([jnp.where(lane < HEAD_DIM, v128, 0.0),
                                   jnp.where(lane >= HEAD_DIM, v128, 0.0)], axis=0).astype(BF16)
            pieces.append(jnp.dot(wp_ref[p], rhs, preferred_element_type=F32))
        mixed = jnp.concatenate(pieces, axis=1) + sb_ref[...]
        mix_ref[0, c0:c0 + CHUNK, A_WIDTH:A_WIDTH + B_WIDTH] = (u * mixed).astype(mix_ref.dtype)

    g_b, g_c, x_c = _split_c(z_ref, slice(None))
    cbuf[C_HALO:C_HALO + tt, :] = g_c * x_c
    stc_ref[0] = cbuf[tt:tt + C_HALO, :]
    cbase = C_HALO - (C_CONV_LEN - 1)
    conv = ccw_ref[0:1, :] * cbuf[cbase:cbase + tt, :]
    for j in range(1, C_CONV_LEN):
        conv = conv + ccw_ref[j:j + 1, :] * cbuf[cbase + j:cbase + j + tt, :]
    o_c = jnp.concatenate([g_b * conv, jnp.zeros((tt, ABC_PAD - 2 * A_WIDTH - C_WIDTH), F32)], axis=1)
    mix_ref[0, :, 2 * A_WIDTH:ABC_PAD] = o_c.astype(mix_ref.dtype)

    q, k, v = _split_qkv(z_ref, slice(None))
    qn = _head_norm(q, gq_ref[...] * ATTN_SCALE)
    kn = _head_norm(k, gk_ref[...])
    outs = ((q1_ref, kv1_ref, st1_ref), (q4_ref, kv4_ref, st4_ref), (q16_ref, kv16_ref, st16_ref))
    for g, (window, dil) in enumerate(ATTN_GROUPS):
        q_ref, kv_ref, st_ref = outs[g]
        hs = slice(HEADS_PER_GROUP * g, HEADS_PER_GROUP * (g + 1))
        q_g = jnp.concatenate(qn[hs], axis=1)
        kv_g = jnp.concatenate(kn[hs] + [v[:, GROUP_WIDTH * g:GROUP_WIDTH * (g + 1)]], axis=1)
        keep = min(window, tt)
        st_ref[0] = kv_g[tt - keep:, :]
        if dil == 1:
            q_ref[0, 0] = q_g
            kv_ref[0, 0] = kv_g
        else:
            qs_ref[0] = q_g[:, :LANE]
            qs_ref[1] = jnp.concatenate([q_g[:, LANE:], jnp.zeros((tt, 2 * LANE - GROUP_WIDTH), F32)], axis=1)
            for i in range(3):
                kvs_ref[i] = kv_g[:, LANE * i:LANE * (i + 1)]
            for r in range(dil):
                rows = pl.ds(r, tt // dil, stride=dil)
                q_ref[0, r, :, 0:LANE] = qs_ref[0, rows, :]
                q_ref[0, r, :, LANE:GROUP_WIDTH] = qs_ref[1, rows, :][:, :GROUP_WIDTH - LANE]
                for i in range(3):
                    kv_ref[0, r, :, LANE * i:LANE * (i + 1)] = kvs_ref[i, rows, :]


def _mixers_prompt(z, layer, wts):
    (caw, cab, lag, lab, lbg, lbb, sw, sb_rep, ccw, gq, gk) = wts
    n, s, _ = z.shape
    tt = MIX_TT
    nt = s // tt

    def lw(shape):
        nd = len(shape)
        return pl.BlockSpec((None,) + shape, lambda b, t: (layer,) + (0,) * nd)

    in_specs = [
        pl.BlockSpec((1, tt, IN_COLS), lambda b, t: (b, t, 0)),
        lw((A_CONV_LEN, A_WIDTH)), lw((1, A_WIDTH)), lw((1, A_WIDTH)), lw((1, A_WIDTH)),
        lw((1, B_WIDTH)), lw((1, B_WIDTH)), lw((B_HEADS, CHUNK, CHUNK)), lw((CHUNK, B_WIDTH)),
        lw((C_CONV_LEN, C_WIDTH)), lw((1, HEAD_DIM)), lw((1, HEAD_DIM)),
    ]
    out_shape = [jax.ShapeDtypeStruct((n, s, ABC_PAD), BF16)]
    out_specs = [pl.BlockSpec((1, tt, ABC_PAD), lambda b, t: (b, t, 0))]
    for _, dil in ATTN_GROUPS:
        for width in (GROUP_WIDTH, 2 * GROUP_WIDTH):
            out_shape.append(jax.ShapeDtypeStruct((n, dil, s // dil, width), F32))
            out_specs.append(pl.BlockSpec((1, dil, tt // dil, width), lambda b, t: (b, 0, t, 0)))
    out_shape.append(jax.ShapeDtypeStruct((n, A_HALO, A_WIDTH), F32))
    out_specs.append(pl.BlockSpec((1, A_HALO, A_WIDTH), lambda b, t: (b, 0, 0)))
    out_shape.append(jax.ShapeDtypeStruct((n, C_HALO, C_WIDTH), F32))
    out_specs.append(pl.BlockSpec((1, C_HALO, C_WIDTH), lambda b, t: (b, 0, 0)))
    for window, _ in ATTN_GROUPS:
        keep = min(window, s)
        blk = min(keep, tt)
        first = (s - keep) // blk
        out_shape.append(jax.ShapeDtypeStruct((n, keep, 2 * GROUP_WIDTH), F32))
        if keep <= tt:
            out_specs.append(pl.BlockSpec((1, blk, 2 * GROUP_WIDTH), lambda b, t: (b, 0, 0)))
        else:
            out_specs.append(pl.BlockSpec((1, blk, 2 * GROUP_WIDTH),
                                          lambda b, t, first=first: (b, jnp.maximum(t - first, 0), 0)))
    return pl.pallas_call(
        _mixer_kernel,
        grid=(n, nt),
        in_specs=in_specs,
        out_specs=out_specs,
        out_shape=out_shape,
        scratch_shapes=[
            pltpu.VMEM((tt + A_HALO, A_WIDTH), F32),
            pltpu.VMEM((tt + C_HALO, C_WIDTH), F32),
            pltpu.VMEM((B_HEADS // 2, CHUNK, 2 * CHUNK), BF16),
            pltpu.VMEM((2, tt, LANE), F32),
            pltpu.VMEM((3, tt, LANE), F32),
        ],
        compiler_params=_cparams(("arbitrary", "arbitrary")),
        name="mixers_prompt",
    )(z, caw, cab, lag, lab, lbg, lbb, sw, sb_rep, ccw, gq, gk)


ATTN_QBLOCKS = 4


def _attn_kernel(q_ref, kvo_ref, kvp_ref, o_ref, l_ref, *, qblocks):
    c = pl.program_id(2)
    qi = lax.broadcasted_iota(jnp.int32, (WIN_KEYS, 2 * WIN_KEYS), 0)
    kj = lax.broadcasted_iota(jnp.int32, (WIN_KEYS, 2 * WIN_KEYS), 1)
    dist = qi + WIN_KEYS - kj
    band = (dist >= 0) & (dist <= WIN_KEYS)
    first_key = jnp.where(c > 0, 0, WIN_KEYS)
    for s in range(qblocks):
        rows = slice(WIN_KEYS * s, WIN_KEYS * (s + 1))
        q = q_ref[0, 0, rows, :]
        kvo = kvo_ref[0, 0, rows, :]
        if s == 0:
            kvp = kvp_ref[0, 0]
            mask = band & (kj >= first_key)
        else:
            kvp = kvo_ref[0, 0, WIN_KEYS * (s - 1):WIN_KEYS * s, :]
            mask = band
        o_parts, l_parts = [], []
        for h in range(HEADS_PER_GROUP):
            ks = slice(HEAD_DIM * h, HEAD_DIM * (h + 1))
            vs = slice(GROUP_WIDTH + HEAD_DIM * h, GROUP_WIDTH + HEAD_DIM * (h + 1))
            qh = q[:, ks].astype(BF16)
            kk = jnp.concatenate([kvp[:, ks], kvo[:, ks]], axis=0).astype(BF16)
            vv = jnp.concatenate([kvp[:, vs], kvo[:, vs]], axis=0).astype(BF16)
            sc = lax.dot_general(qh, kk, (((1,), (1,)), ((), ())), preferred_element_type=F32)
            sc = jnp.where(mask, sc, NEG_INF)
            m = jnp.max(sc, axis=-1, keepdims=True)
            ex = jnp.exp(sc - m)
            den = jnp.sum(ex, axis=-1, keepdims=True)
            probs = (ex / den).astype(BF16)
            o_parts.append(jnp.dot(probs, vv, preferred_element_type=F32))
            l_parts.append(jnp.broadcast_to(m + jnp.log(den), (WIN_KEYS, HEAD_DIM)))
        o_ref[0, 0, rows, :] = jnp.concatenate(o_parts, axis=1)
        l_ref[0, 0, rows, :] = jnp.concatenate(l_parts, axis=1)


def _attn_prompt(q, kv):
    n, dil, sub, _ = q.shape
    qblocks = min(ATTN_QBLOCKS, sub // WIN_KEYS)
    rows = qblocks * WIN_KEYS
    qspec = pl.BlockSpec((1, 1, rows, GROUP_WIDTH), lambda b, r, c: (b, r, c, 0))
    return pl.pallas_call(
        functools.partial(_attn_kernel, qblocks=qblocks),
        grid=(n, dil, sub // rows),
        in_specs=[
            qspec,
            pl.BlockSpec((1, 1, rows, 2 * GROUP_WIDTH), lambda b, r, c: (b, r, c, 0)),
            pl.BlockSpec((1, 1, WIN_KEYS, 2 * GROUP_WIDTH),
                         lambda b, r, c: (b, r, jnp.maximum(c * qblocks - 1, 0), 0)),
        ],
        out_specs=[qspec, qspec],
        out_shape=[jax.ShapeDtypeStruct(q.shape, F32), jax.ShapeDtypeStruct(q.shape, F32)],
        compiler_params=_cparams(("arbitrary", "arbitrary", "arbitrary")),
        name="attn_prompt",
    )(q, kv, kv)


def _combine_kernel(o1_ref, l1_ref, o4_ref, l4_ref, o16_ref, l16_ref, od_ref, s_o4, s_l4, s_o16, s_l16):
    tt = MIX_TT
    for dil, src, dst in ((4, o4_ref, s_o4), (4, l4_ref, s_l4), (16, o16_ref, s_o16), (16, l16_ref, s_l16)):
        for r in range(dil):
            x = src[0, r]
            rows = pl.ds(r, tt // dil, stride=dil)
            dst[0, rows, :] = x[:, :LANE]
            dst[1, rows, :] = jnp.concatenate(
                [x[:, LANE:], jnp.zeros((tt // dil, 2 * LANE - GROUP_WIDTH), F32)], axis=1)

    def whole(scr):
        return jnp.concatenate([scr[0], scr[1][:, :GROUP_WIDTH - LANE]], axis=1)

    outs = (o1_ref[0, 0], whole(s_o4), whole(s_o16))
    lses = (l1_ref[0, 0], whole(s_l4), whole(s_l16))
    mx = jnp.maximum(jnp.maximum(lses[0], lses[1]), lses[2])
    es = [jnp.exp(l - mx) for l in lses]
    den = es[0] + es[1] + es[2]
    pad = jnp.zeros((tt, OD_PAD // 3 - GROUP_WIDTH), F32)
    parts = []
    for g in range(3):
        parts += [outs[g] * (es[g] / den), pad]
    od_ref[0] = jnp.concatenate(parts, axis=1).astype(od_ref.dtype)


def _combine_prompt(o1, l1, o4, l4, o16, l16):
    n, _, s, _ = o1.shape
    tt = MIX_TT

    def spec(dil):
        return pl.BlockSpec((1, dil, tt // dil, GROUP_WIDTH), lambda b, t: (b, 0, t, 0))

    return pl.pallas_call(
        _combine_kernel,
        grid=(n, s // tt),
        in_specs=[spec(1), spec(1), spec(4), spec(4), spec(16), spec(16)],
        out_specs=pl.BlockSpec((1, tt, OD_PAD), lambda b, t: (b, t, 0)),
        out_shape=jax.ShapeDtypeStruct((n, s, OD_PAD), BF16),
        scratch_shapes=[pltpu.VMEM((2, tt, LANE), F32)] * 4,
        compiler_params=_cparams(("arbitrary", "arbitrary")),
        name="combine_prompt",
    )(o1, l1, o4, l4, o16, l16)


def _dec_kernel(z_ref, ha_ref, hc_ref, c1_ref, c4_ref, c16_ref,
                caw_ref, cab_ref, lag_ref, lab_ref, lbg_ref, lbb_ref, sw8_ref, sb_ref, ccw_ref, gq_ref, gk_ref,
                mix_ref, od_ref, na_ref, nc_ref, cv_ref, n1_ref, n4_ref, n16_ref,
                abuf, cbuf, kvbuf, exbuf):
    t_new = z_ref.shape[1]
    hist_a = A_CONV_LEN - 1
    hist_c = C_CONV_LEN - 1

    za = z_ref[0, :, 0:2 * A_WIDTH]
    abuf[0:hist_a, :] = ha_ref[0]
    abuf[hist_a:hist_a + t_new, :] = za[:, :A_WIDTH] * _sigmoid(za[:, A_WIDTH:])
    acc = caw_ref[0:1, :] * abuf[0:t_new, :]
    for j in range(1, A_CONV_LEN):
        acc = acc + caw_ref[j:j + 1, :] * abuf[j:j + t_new, :]
    y = _layernorm_rows(acc + cab_ref[...], lag_ref[...], lab_ref[...])
    mix_ref[0, :, 0:A_WIDTH] = _silu(y).astype(mix_ref.dtype)
    na_ref[0] = abuf[t_new:t_new + hist_a, :]

    gb = _gelu(z_ref[0, :, COL_B:COL_B + 2 * B_WIDTH])
    u = gb[:, :B_WIDTH]
    v = _layernorm_rows(gb[:, B_WIDTH:], lbg_ref[...], lbb_ref[...])
    cv_ref[0] = v
    row = lax.broadcasted_iota(jnp.int32, (t_new, B_WIDTH), 0)
    mixed = sb_ref[0:t_new, :]
    for s in range(t_new):
        mixed = mixed + jnp.where(row >= s, sw8_ref[s], 0.0) * v[s:s + 1, :]
    mix_ref[0, :, A_WIDTH:A_WIDTH + B_WIDTH] = (u * mixed).astype(mix_ref.dtype)

    g_b, g_c, x_c = _split_c(z_ref, slice(None))
    cbuf[0:hist_c, :] = hc_ref[0]
    cbuf[hist_c:hist_c + t_new, :] = g_c * x_c
    conv = ccw_ref[0:1, :] * cbuf[0:t_new, :]
    for j in range(1, C_CONV_LEN):
        conv = conv + ccw_ref[j:j + 1, :] * cbuf[j:j + t_new, :]
    o_c = jnp.concatenate([g_b * conv, jnp.zeros((t_new, ABC_PAD - 2 * A_WIDTH - C_WIDTH), F32)], axis=1)
    mix_ref[0, :, 2 * A_WIDTH:ABC_PAD] = o_c.astype(mix_ref.dtype)
    nc_ref[0] = cbuf[t_new:t_new + hist_c, :]

    q, k, v_d = _split_qkv(z_ref, slice(None))
    qn = _head_norm(q, gq_ref[...] * ATTN_SCALE)
    kn = _head_norm(k, gk_ref[...])
    caches = (c1_ref, c4_ref, c16_ref)
    news = (n1_ref, n4_ref, n16_ref)
    qrow = lax.broadcasted_iota(jnp.int32, (LANE, 2 * GROUP_WIDTH), 0)
    qlane = lax.broadcasted_iota(jnp.int32, (LANE, 2 * GROUP_WIDTH), 1)
    qmask = (qrow >> 3) == (qlane >> 6)
    zeros_q = jnp.zeros((t_new, GROUP_WIDTH), F32)
    lses, dens, offs = [], [], []
    off = 0
    for g, (window, dil) in enumerate(ATTN_GROUPS):
        hs = slice(HEADS_PER_GROUP * g, HEADS_PER_GROUP * (g + 1))
        buf_len = caches[g].shape[1]
        rows = buf_len + t_new
        kv_new = jnp.concatenate(kn[hs] + [v_d[:, GROUP_WIDTH * g:GROUP_WIDTH * (g + 1)]], axis=1)
        kvbuf[off:off + buf_len, :] = caches[g][0]
        kvbuf[off + buf_len:off + rows, :] = kv_new
        news[g][0] = kvbuf[off + t_new:off + rows, :]
        q_g = jnp.concatenate(qn[hs] + [zeros_q], axis=1)
        q_rep = jnp.concatenate([q_g] * HEADS_PER_GROUP
                                + [jnp.zeros((LANE - HEADS_PER_GROUP * t_new, 2 * GROUP_WIDTH), F32)], axis=0)
        q_hi, q_lo = _split_bf16(jnp.where(qmask, q_rep, 0.0))
        kv_hi, kv_lo = _split_bf16(kvbuf[off:off + rows, :])
        nt_dims = (((1,), (1,)), ((), ()))
        sc = (lax.dot_general(kv_hi, q_hi, nt_dims, preferred_element_type=F32)
              + lax.dot_general(kv_lo, q_hi, nt_dims, preferred_element_type=F32)
              + lax.dot_general(kv_hi, q_lo, nt_dims, preferred_element_type=F32))
        krow = lax.broadcasted_iota(jnp.int32, (rows, LANE), 0)
        tok = lax.broadcasted_iota(jnp.int32, (rows, LANE), 1) & (t_new - 1)
        dist = buf_len + tok - krow
        valid = (dist >= 0) & (dist <= dil * WIN_KEYS) & ((dist & (dil - 1)) == 0)
        sc = jnp.where(valid, sc, NEG_INF)
        m = jnp.max(sc, axis=0, keepdims=True)
        ex = jnp.exp(sc - m)
        den = jnp.sum(ex, axis=0, keepdims=True)
        exbuf[off:off + rows, :] = ex
        lses.append(m + jnp.log(den))
        dens.append(den)
        offs.append((off, rows))
        off += rows
    mx = jnp.maximum(jnp.maximum(lses[0], lses[1]), lses[2])
    es = [jnp.exp(l - mx) for l in lses]
    tot = es[0] + es[1] + es[2]
    lane = lax.broadcasted_iota(jnp.int32, (t_new, 2 * GROUP_WIDTH), 1)
    pad = jnp.zeros((t_new, OD_PAD // 3 - GROUP_WIDTH), F32)
    parts = []
    for g in range(3):
        off, rows = offs[g]
        coef = es[g] / (tot * dens[g])
        probs = (exbuf[off:off + rows, :] * coef).astype(BF16)
        kv_all = kvbuf[off:off + rows, :].astype(BF16)
        o_t = lax.dot_general(probs, kv_all, (((0,), (0,)), ((), ())), preferred_element_type=F32)
        o_g = jnp.zeros((t_new, 2 * GROUP_WIDTH), F32)
        for h in range(HEADS_PER_GROUP):
            sel = (lane >= GROUP_WIDTH + HEAD_DIM * h) & (lane < GROUP_WIDTH + HEAD_DIM * (h + 1))
            o_g = o_g + jnp.where(sel, o_t[t_new * h:t_new * (h + 1), :], 0.0)
        parts += [o_g[:, GROUP_WIDTH:], pad]
    od_ref[0] = jnp.concatenate(parts, axis=1).astype(od_ref.dtype)


def _mixers_sample(z, layer, hist_a, hist_c, caches, wts):
    (caw, cab, lag, lab, lbg, lbb, sw8, sb_rep, ccw, gq, gk) = wts
    n, t_new, _ = z.shape

    def lw(shape):
        nd = len(shape)
        return pl.BlockSpec((None,) + shape, lambda b: (layer,) + (0,) * nd)

    def st(shape):
        nd = len(shape)
        return pl.BlockSpec((None, 1) + shape, lambda b: (layer, b) + (0,) * nd)

    lens = [c.shape[2] for c in caches]
    total_rows = sum(lens) + 3 * t_new
    in_specs = [
        pl.BlockSpec((1, t_new, IN_COLS), lambda b: (b, 0, 0)),
        st((A_CONV_LEN - 1, A_WIDTH)), st((C_CONV_LEN - 1, C_WIDTH)),
        st((lens[0], 2 * GROUP_WIDTH)), st((lens[1], 2 * GROUP_WIDTH)), st((lens[2], 2 * GROUP_WIDTH)),
        lw((A_CONV_LEN, A_WIDTH)), lw((1, A_WIDTH)), lw((1, A_WIDTH)), lw((1, A_WIDTH)),
        lw((1, B_WIDTH)), lw((1, B_WIDTH)), lw((t_new, t_new, B_WIDTH)), lw((CHUNK, B_WIDTH)),
        lw((C_CONV_LEN, C_WIDTH)), lw((1, HEAD_DIM)), lw((1, HEAD_DIM)),
    ]

    def ob(shape):
        nd = len(shape)
        return pl.BlockSpec((1,) + shape, lambda b: (b,) + (0,) * nd)

    out_shape = [
        jax.ShapeDtypeStruct((n, t_new, ABC_PAD), F32),
        jax.ShapeDtypeStruct((n, t_new, OD_PAD), F32),
        jax.ShapeDtypeStruct((n, A_CONV_LEN - 1, A_WIDTH), F32),
        jax.ShapeDtypeStruct((n, C_CONV_LEN - 1, C_WIDTH), F32),
        jax.ShapeDtypeStruct((n, t_new, B_WIDTH), F32),
    ] + [jax.ShapeDtypeStruct((n, ln, 2 * GROUP_WIDTH), F32) for ln in lens]
    out_specs = [ob(s.shape[1:]) for s in out_shape]
    return pl.pallas_call(
        _dec_kernel,
        grid=(n,),
        in_specs=in_specs,
        out_specs=out_specs,
        out_shape=out_shape,
        scratch_shapes=[
            pltpu.VMEM((A_CONV_LEN - 1 + t_new + 2, A_WIDTH), F32),
            pltpu.VMEM((16, C_WIDTH), F32),
            pltpu.VMEM((total_rows, 2 * GROUP_WIDTH), F32),
            pltpu.VMEM((total_rows, LANE), F32),
        ],
        compiler_params=_cparams(("arbitrary",)),
        name="mixers_sample",
    )(z, hist_a, hist_c, *caches, caw, cab, lag, lab, lbg, lbb, sw8, sb_rep, ccw, gq, gk)


def _outproj_kernel(h_ref, mix_ref, od_ref, w_ref, wd_ref, o_ref, *, hp):
    acc = _mm(mix_ref[...], w_ref[0:ABC_PAD, :], hp) + _mm(od_ref[...], wd_ref[...], hp)
    o_ref[...] = h_ref[...] + acc


def _outproj(h, mix, od, w_all, wd_all, layer, tm, tn, hp):
    rows = h.shape[0]
    return pl.pallas_call(
        functools.partial(_outproj_kernel, hp=hp),
        grid=(rows // tm, D_MODEL // tn),
        in_specs=[
            pl.BlockSpec((tm, tn), lambda i, j: (i, j)),
            pl.BlockSpec((tm, ABC_PAD), lambda i, j: (i, 0)),
            pl.BlockSpec((tm, OD_PAD), lambda i, j: (i, 0)),
            pl.BlockSpec((None, D_MODEL, tn), lambda i, j: (layer, 0, j)),
            pl.BlockSpec((None, OD_PAD, tn), lambda i, j: (layer, 0, j)),
        ],
        out_specs=pl.BlockSpec((tm, tn), lambda i, j: (i, j)),
        out_shape=jax.ShapeDtypeStruct((rows, D_MODEL), F32),
        compiler_params=_cparams(("arbitrary", "arbitrary")),
        name="outproj",
    )(h, mix, od, w_all, wd_all)


def _top2_in_top_group(logits):
    rows = logits.shape[0]
    lane = lax.broadcasted_iota(jnp.int32, (rows, ROUTER_COLS), 1)
    big = jnp.int32(ROUTER_COLS)
    is_grp = (lane >= N_EXPERTS) & (lane < N_EXPERTS + MOE_GROUPS)
    gl = jnp.where(is_grp, logits, NEG_INF)
    gmax = jnp.max(gl, axis=-1, keepdims=True)
    gidx = jnp.min(jnp.where(gl == gmax, lane - N_EXPERTS, big), axis=-1, keepdims=True)
    gate = 1.0 / jnp.sum(jnp.where(is_grp, jnp.exp(gl - gmax), 0.0), axis=-1, keepdims=True)
    in_grp = (lane < N_EXPERTS) & ((lane >> 2) == gidx)
    el = jnp.where(in_grp, logits, NEG_INF)
    t1 = jnp.max(el, axis=-1, keepdims=True)
    i1 = jnp.min(jnp.where(el == t1, lane, big), axis=-1, keepdims=True)
    el2 = jnp.where(lane == i1, NEG_INF, el)
    t2 = jnp.max(el2, axis=-1, keepdims=True)
    i2 = jnp.min(jnp.where(el2 == t2, lane, big), axis=-1, keepdims=True)
    e2 = jnp.exp(t2 - t1)
    return lane, i1, i2, gate / (1.0 + e2), gate * e2 / (1.0 + e2)


META_E1, META_E2, META_R1, META_R2, META_W1, META_W2 = range(6)


def _router_sparse_kernel(h_ref, g_ref, wr_ref, br_ref, m_ref, meta_ref, meta_t_ref, cnt_ref, carry_ref):
    @pl.when(pl.program_id(0) == 0)
    def _():
        carry_ref[...] = jnp.zeros_like(carry_ref)

    m = _rms_rows(h_ref[...], g_ref[...])
    m_ref[...] = m
    logits = _mm(m, wr_ref[...], False) + br_ref[...]
    rows = logits.shape[0]
    lane, i1, i2, w1, w2 = _top2_in_top_group(logits)
    sel = jnp.where((lane == i1) | (lane == i2), 1.0, 0.0)
    r = lax.broadcasted_iota(jnp.int32, (rows, rows), 0)
    c = lax.broadcasted_iota(jnp.int32, (rows, rows), 1)
    earlier = jnp.where(c < r, 1.0, 0.0).astype(BF16)
    rank = jnp.dot(earlier, sel.astype(BF16), preferred_element_type=F32) + carry_ref[...]
    r1 = jnp.sum(jnp.where(lane == i1, rank, 0.0), axis=-1, keepdims=True)
    r2 = jnp.sum(jnp.where(lane == i2, rank, 0.0), axis=-1, keepdims=True)
    carry_ref[...] += jnp.sum(sel, axis=0, keepdims=True)
    cnt_ref[...] = carry_ref[...]
    meta = jnp.zeros((rows, ROUTER_COLS), F32)
    for pos, val in ((META_E1, i1.astype(F32)), (META_E2, i2.astype(F32)), (META_R1, r1), (META_R2, r2),
                     (META_W1, w1), (META_W2, w2)):
        meta = jnp.where(lane == pos, val, meta)
    meta_ref[...] = meta
    meta_t_ref[...] = meta.T[:SUBLANE, :]


def _router_sparse(h, g_all, wr_all, br_all, layer, tm):
    rows = h.shape[0]
    return pl.pallas_call(
        _router_sparse_kernel,
        grid=(rows // tm,),
        in_specs=[
            pl.BlockSpec((tm, D_MODEL), lambda i: (i, 0)),
            pl.BlockSpec((None, 1, D_MODEL), lambda i: (layer, 0, 0)),
            pl.BlockSpec((None, D_MODEL, ROUTER_COLS), lambda i: (layer, 0, 0)),
            pl.BlockSpec((None, 1, ROUTER_COLS), lambda i: (layer, 0, 0)),
        ],
        out_specs=[pl.BlockSpec((tm, D_MODEL), lambda i: (i, 0)),
                   pl.BlockSpec((tm, ROUTER_COLS), lambda i: (i, 0)),
                   pl.BlockSpec((SUBLANE, tm), lambda i: (0, i)),
                   pl.BlockSpec((1, ROUTER_COLS), lambda i: (0, 0))],
        out_shape=[jax.ShapeDtypeStruct((rows, D_MODEL), F32),
                   jax.ShapeDtypeStruct((rows, ROUTER_COLS), F32),
                   jax.ShapeDtypeStruct((SUBLANE, rows), F32),
                   jax.ShapeDtypeStruct((1, ROUTER_COLS), F32)],
        scratch_shapes=[pltpu.VMEM((1, ROUTER_COLS), F32)],
        compiler_params=_cparams(("arbitrary",)),
        name="router_sparse",
    )(h, g_all, wr_all, br_all)


EXPERT_TILE = 256


def _experts_kernel(te_ref, tv_ref, s0_ref, s1_ref, nu_ref,
                    m_hbm, wg_ref, wu_ref, wd_ref, yk_hbm,
                    src_ref, xbuf, ybuf, wgb, wub, wdb, gsem, ssem, *, n_tok):
    i = pl.program_id(0)
    tile = EXPERT_TILE
    nused = nu_ref[0]
    slot = lax.rem(i, 2)

    def rows_moved(t):
        return pl.multiple_of(((tv_ref[t] + SUBLANE - 1) // SUBLANE) * SUBLANE, SUBLANE)

    def gather_copy(code, j, b):
        row = jnp.minimum(code >> 1, n_tok - 1)
        return pltpu.make_async_copy(m_hbm.at[pl.ds(row, 1), :], xbuf.at[b, pl.ds(j, 1), :], gsem.at[b])

    def scatter_copy(code, j, b):
        return pltpu.make_async_copy(ybuf.at[b, pl.ds(j, 1), :], yk_hbm.at[code & 1, pl.ds(code >> 1, 1), :],
                                     ssem.at[b])

    def start_rows(copy_fn, t, b):
        def body(j8, carry):
            for u in range(SUBLANE):
                j = j8 * SUBLANE + u
                copy_fn(src_ref[t * tile + j], j, b).start()
            return carry
        lax.fori_loop(0, rows_moved(t) // SUBLANE, body, 0)

    def wait_gather(t, b):
        n = rows_moved(t)
        pltpu.make_async_copy(m_hbm.at[pl.ds(0, n), :], xbuf.at[b, pl.ds(0, n), :], gsem.at[b]).wait()

    def wait_scatter(t, b):
        n = rows_moved(t)
        pltpu.make_async_copy(ybuf.at[b, pl.ds(0, n), :], yk_hbm.at[0, pl.ds(0, n), :], ssem.at[b]).wait()

    @pl.when(i == 0)
    def _():
        def fill(t, carry):
            src_ref[s0_ref[t]] = 2 * t
            src_ref[s1_ref[t]] = 2 * t + 1
            return carry
        lax.fori_loop(0, n_tok, fill, 0)

        def fill_pad(t, carry):
            def one(j, c):
                src_ref[t * tile + j] = 2 * (n_tok + SUBLANE * lax.rem(t, 2) + lax.rem(j, SUBLANE))
                return c
            lax.fori_loop(tv_ref[t], rows_moved(t), one, 0)
            return carry
        lax.fori_loop(0, nused, fill_pad, 0)
        xbuf[...] = jnp.zeros_like(xbuf)
        for k in range(2):
            spare = pltpu.make_async_copy(xbuf.at[0, pl.ds(0, 2 * SUBLANE), :],
                                          yk_hbm.at[k, pl.ds(n_tok, 2 * SUBLANE), :], ssem.at[0])
            spare.start()
            spare.wait()
        start_rows(gather_copy, 0, 0)

    @pl.when(i < nused)
    def _():
        @pl.when(i + 1 < nused)
        def _():
            start_rows(gather_copy, i + 1, 1 - slot)

        changed = jnp.logical_or(i == 0, te_ref[i] != te_ref[jnp.maximum(i - 1, 0)])

        @pl.when(changed)
        def _():
            wgb[...] = wg_ref[...].astype(BF16)
            wub[...] = wu_ref[...].astype(BF16)
            wdb[...] = wd_ref[...].astype(BF16)

        wait_gather(i, slot)

        @pl.when(i >= 2)
        def _():
            wait_scatter(i - 2, slot)

        x = xbuf[slot].astype(BF16)
        gate = jnp.dot(x, wgb[...], preferred_element_type=F32)
        up = jnp.dot(x, wub[...], preferred_element_type=F32)
        ybuf[slot] = jnp.dot((_silu(gate) * up).astype(BF16), wdb[...], preferred_element_type=F32)
        start_rows(scatter_copy, i, slot)

        @pl.when(i == nused - 1)
        def _():
            @pl.when(i >= 1)
            def _():
                wait_scatter(i - 1, 1 - slot)
            wait_scatter(i, slot)


def _experts_sparse(m, tile_expert, tile_rows, slot0, slot1, nused, wg_all, wu_all, wd_all, layer):
    n_tok = m.shape[0]
    n_tiles = tile_expert.shape[0]

    def wspec(shape):
        return pl.BlockSpec((None, None) + shape, lambda i, te, tv, s0, s1, nu: (layer, te[i], 0, 0))

    grid_spec = pltpu.PrefetchScalarGridSpec(
        num_scalar_prefetch=5,
        grid=(n_tiles,),
        in_specs=[pl.BlockSpec(memory_space=pl.ANY),
                  wspec((D_MODEL, EXPERT_FF)), wspec((D_MODEL, EXPERT_FF)), wspec((EXPERT_FF, D_MODEL))],
        out_specs=pl.BlockSpec(memory_space=pl.ANY),
        scratch_shapes=[
            pltpu.SMEM((n_tiles * EXPERT_TILE,), jnp.int32),
            pltpu.VMEM((2, EXPERT_TILE, D_MODEL), F32),
            pltpu.VMEM((2, EXPERT_TILE, D_MODEL), F32),
            pltpu.VMEM((D_MODEL, EXPERT_FF), BF16),
            pltpu.VMEM((D_MODEL, EXPERT_FF), BF16),
            pltpu.VMEM((EXPERT_FF, D_MODEL), BF16),
            pltpu.SemaphoreType.DMA((2,)),
            pltpu.SemaphoreType.DMA((2,)),
        ],
    )
    return pl.pallas_call(
        functools.partial(_experts_kernel, n_tok=n_tok),
        grid_spec=grid_spec,
        out_shape=jax.ShapeDtypeStruct((2, n_tok + 2 * SUBLANE, D_MODEL), F32),
        compiler_params=_cparams(("arbitrary",)),
        name="experts_sparse",
    )(tile_expert, tile_rows, slot0, slot1, nused, m, wg_all, wu_all, wd_all)


def _expert_plan(meta_t, cnt, n_tiles):
    tile = EXPERT_TILE
    counts = cnt[0, :N_EXPERTS].astype(jnp.int32)
    padded = ((counts + tile - 1) // tile) * tile
    ends = jnp.cumsum(padded)
    base = ends - padded
    e1 = meta_t[META_E1].astype(jnp.int32)
    e2 = meta_t[META_E2].astype(jnp.int32)
    slot0 = base[e1] + meta_t[META_R1].astype(jnp.int32)
    slot1 = base[e2] + meta_t[META_R2].astype(jnp.int32)
    start = jnp.arange(n_tiles, dtype=jnp.int32) * tile
    expert_of = jnp.sum((start[:, None] >= ends[None, :]).astype(jnp.int32), axis=1)
    last_used = jnp.max(jnp.where(counts > 0, jnp.arange(N_EXPERTS, dtype=jnp.int32), 0))
    tile_expert = jnp.minimum(expert_of, last_used)
    e_clamped = jnp.minimum(expert_of, N_EXPERTS - 1)
    tile_rows = jnp.clip(counts[e_clamped] - (start - base[e_clamped]), 0, tile)
    tile_rows = jnp.where(expert_of < N_EXPERTS, tile_rows, 0).astype(jnp.int32)
    nused = (ends[-1] // tile).astype(jnp.int32).reshape(1)
    return tile_expert, tile_rows, slot0, slot1, nused


def _moe_combine_kernel(h_ref, y0_ref, y1_ref, meta_ref, o_ref):
    meta = meta_ref[...]
    lane = lax.broadcasted_iota(jnp.int32, meta.shape, 1)
    w1 = jnp.sum(jnp.where(lane == META_W1, meta, 0.0), axis=-1, keepdims=True)
    w2 = jnp.sum(jnp.where(lane == META_W2, meta, 0.0), axis=-1, keepdims=True)
    o_ref[...] = h_ref[...] + w1 * y0_ref[...] + w2 * y1_ref[...]


def _moe_combine(h, yk, meta, tm):
    rows = h.shape[0]
    return pl.pallas_call(
        _moe_combine_kernel,
        grid=(rows // tm,),
        in_specs=[
            pl.BlockSpec((tm, D_MODEL), lambda i: (i, 0)),
            pl.BlockSpec((None, tm, D_MODEL), lambda i: (0, i, 0)),
            pl.BlockSpec((None, tm, D_MODEL), lambda i: (1, i, 0)),
            pl.BlockSpec((tm, ROUTER_COLS), lambda i: (i, 0)),
        ],
        out_specs=pl.BlockSpec((tm, D_MODEL), lambda i: (i, 0)),
        out_shape=jax.ShapeDtypeStruct((rows, D_MODEL), F32),
        compiler_params=_cparams(("arbitrary",)),
        name="moe_combine",
    )(h, yk, yk, meta)


def _router_kernel(h_ref, g_ref, wr_ref, br_ref, m_ref, comb_ref, *, hp):
    m = _rms_rows(h_ref[...], g_ref[...]).astype(m_ref.dtype)
    m_ref[...] = m
    logits = _mm(m, wr_ref[...], hp) + br_ref[...]
    rows = logits.shape[0]
    lane = lax.broadcasted_iota(jnp.int32, (rows, ROUTER_COLS), 1)
    big = jnp.int32(ROUTER_COLS)
    is_grp = (lane >= N_EXPERTS) & (lane < N_EXPERTS + MOE_GROUPS)
    gl = jnp.where(is_grp, logits, NEG_INF)
    gmax = jnp.max(gl, axis=-1, keepdims=True)
    gidx = jnp.min(jnp.where(gl == gmax, lane - N_EXPERTS, big), axis=-1, keepdims=True)
    gate = 1.0 / jnp.sum(jnp.where(is_grp, jnp.exp(gl - gmax), 0.0), axis=-1, keepdims=True)
    in_grp = (lane < N_EXPERTS) & ((lane >> 2) == gidx)
    el = jnp.where(in_grp, logits, NEG_INF)
    t1 = jnp.max(el, axis=-1, keepdims=True)
    i1 = jnp.min(jnp.where(el == t1, lane, big), axis=-1, keepdims=True)
    el2 = jnp.where(lane == i1, NEG_INF, el)
    t2 = jnp.max(el2, axis=-1, keepdims=True)
    i2 = jnp.min(jnp.where(el2 == t2, lane, big), axis=-1, keepdims=True)
    e2 = jnp.exp(t2 - t1)
    w1 = gate / (1.0 + e2)
    w2 = gate * e2 / (1.0 + e2)
    comb_ref[...] = jnp.where(lane == i1, w1, 0.0) + jnp.where(lane == i2, w2, 0.0)


def _router(h, g_all, wr_all, br_all, layer, tm, hp):
    rows = h.shape[0]
    return pl.pallas_call(
        functools.partial(_router_kernel, hp=hp),
        grid=(rows // tm,),
        in_specs=[
            pl.BlockSpec((tm, D_MODEL), lambda i: (i, 0)),
            pl.BlockSpec((None, 1, D_MODEL), lambda i: (layer, 0, 0)),
            pl.BlockSpec((None, D_MODEL, ROUTER_COLS), lambda i: (layer, 0, 0)),
            pl.BlockSpec((None, 1, ROUTER_COLS), lambda i: (layer, 0, 0)),
        ],
        out_specs=[pl.BlockSpec((tm, D_MODEL), lambda i: (i, 0)),
                   pl.BlockSpec((tm, ROUTER_COLS), lambda i: (i, 0))],
        out_shape=[jax.ShapeDtypeStruct((rows, D_MODEL), _act_dtype(hp)),
                   jax.ShapeDtypeStruct((rows, ROUTER_COLS), F32)],
        compiler_params=_cparams(("arbitrary",)),
        name="router",
    )(h, g_all, wr_all, br_all)


def _moe_kernel(h_ref, m_ref, comb_ref, wg_ref, wu_ref, wd_ref, o_ref, *, hp):
    e = pl.program_id(1)

    @pl.when(e == 0)
    def _():
        o_ref[...] = h_ref[...]

    x = m_ref[...]
    gate = _mm(x, wg_ref[...], hp)
    up = _mm(x, wu_ref[...], hp)
    lane = lax.broadcasted_iota(jnp.int32, comb_ref.shape, 1)
    w = jnp.sum(jnp.where(lane == e, comb_ref[...], 0.0), axis=-1, keepdims=True)
    o_ref[...] += _mm(_silu(gate) * up * w, wd_ref[...], hp)


def _moe_dense(h, m, comb, wg_all, wu_all, wd_all, layer, tm, hp):
    rows = h.shape[0]
    return pl.pallas_call(
        functools.partial(_moe_kernel, hp=hp),
        grid=(rows // tm, N_EXPERTS),
        in_specs=[
            pl.BlockSpec((tm, D_MODEL), lambda i, e: (i, 0)),
            pl.BlockSpec((tm, D_MODEL), lambda i, e: (i, 0)),
            pl.BlockSpec((tm, ROUTER_COLS), lambda i, e: (i, 0)),
            pl.BlockSpec((None, None, D_MODEL, EXPERT_FF), lambda i, e: (layer, e, 0, 0)),
            pl.BlockSpec((None, None, D_MODEL, EXPERT_FF), lambda i, e: (layer, e, 0, 0)),
            pl.BlockSpec((None, None, EXPERT_FF, D_MODEL), lambda i, e: (layer, e, 0, 0)),
        ],
        out_specs=pl.BlockSpec((tm, D_MODEL), lambda i, e: (i, 0)),
        out_shape=jax.ShapeDtypeStruct((rows, D_MODEL), F32),
        compiler_params=_cparams(("arbitrary", "arbitrary")),
        name="moe_dense",
    )(h, m, comb, wg_all, wu_all, wd_all)


def _ple_kernel(h_ref, hc_ref, g_ref, p_ref, wg_ref, wp_ref, o_ref, xn_ref, *, hp):
    @pl.when(pl.program_id(1) == 0)
    def _():
        xn_ref[...] = _rms_rows(h_ref[...], g_ref[...]).astype(xn_ref.dtype)

    gate = _sigmoid(_mm(xn_ref[...], wg_ref[...], hp))
    o_ref[...] = hc_ref[...] + gate * _mm(p_ref[...], wp_ref[...], hp)


def _ple(h, p_all, g_all, wg_all, wp_all, layer, tm, tn, hp):
    rows = h.shape[0]
    return pl.pallas_call(
        functools.partial(_ple_kernel, hp=hp),
        grid=(rows // tm, D_MODEL // tn),
        in_specs=[
            pl.BlockSpec((tm, D_MODEL), lambda i, j: (i, 0)),
            pl.BlockSpec((tm, tn), lambda i, j: (i, j)),
            pl.BlockSpec((None, 1, D_MODEL), lambda i, j: (layer, 0, 0)),
            pl.BlockSpec((None, tm, PLE_DIM), lambda i, j: (layer, i, 0)),
            pl.BlockSpec((None, D_MODEL, tn), lambda i, j: (layer, 0, j)),
            pl.BlockSpec((None, PLE_DIM, tn), lambda i, j: (layer, 0, j)),
        ],
        out_specs=pl.BlockSpec((tm, tn), lambda i, j: (i, j)),
        out_shape=jax.ShapeDtypeStruct((rows, D_MODEL), F32),
        scratch_shapes=[pltpu.VMEM((tm, D_MODEL), _act_dtype(hp))],
        compiler_params=_cparams(("arbitrary", "arbitrary")),
        name="ple",
    )(h, h, g_all, p_all, wg_all, wp_all)


def _token_tail(h, mix, od, p_all, layer, tw, tm, hp):
    (w_out, w_out_d, g_ffn, w_router, b_router, w_gate, w_up, w_down, g_ple, w_ple_gate, w_ple_proj) = tw
    h = _outproj(h, mix, od, w_out, w_out_d, layer, tm, 512, hp)
    if hp:
        m, comb = _router(h, g_ffn, w_router, b_router, layer, tm, hp)
        h = _moe_dense(h, m, comb, w_gate, w_up, w_down, layer, tm, hp)
    else:
        rows = h.shape[0]
        n_tiles = (2 * rows + N_EXPERTS * (EXPERT_TILE - 1) + EXPERT_TILE - 1) // EXPERT_TILE
        m, meta, meta_t, cnt = _router_sparse(h, g_ffn, w_router, b_router, layer, 512)
        plan = _expert_plan(meta_t, cnt, n_tiles)
        yk = _experts_sparse(m, *plan, w_gate, w_up, w_down, layer)
        h = _moe_combine(h, yk, meta, 512)
    return _ple(h, p_all, g_ple, w_ple_gate, w_ple_proj, layer, tm, 512, hp)


TAIL = 8


def _patch_kernel(x_ref, tail_ref, o_ref):
    del x_ref
    o_ref[...] = tail_ref[...]


def _patch_tail(x, tail):
    n, s, width = x.shape
    return pl.pallas_call(
        _patch_kernel,
        grid=(n,),
        in_specs=[pl.BlockSpec(memory_space=pl.ANY),
                  pl.BlockSpec((1, TAIL, width), lambda b: (b, 0, 0))],
        out_specs=pl.BlockSpec((1, TAIL, width), lambda b: (b, s // TAIL - 1, 0)),
        out_shape=jax.ShapeDtypeStruct(x.shape, x.dtype),
        input_output_aliases={0: 0},
        compiler_params=_cparams(("arbitrary",)),
        name="patch_tail",
    )(x, tail)


def kernel(x_prompt, x_sample, p_prompt, p_sample, state_conv_a, state_conv_c, cache_kv_w128, cache_kv_w512, cache_kv_w2048, g_mix, w_in, conv_a_w, conv_a_b, ln_a_g, ln_a_b, ln_b_g, ln_b_b, sgu_w, sgu_b, conv_c_w, g_q, g_k, w_out, g_ffn, w_router_grp, b_router_grp, w_router_exp, b_router_exp, w_gate, w_up, w_down, g_ple, w_ple_gate, w_ple_proj):
    n_p, s_p, _ = x_prompt.shape
    n_s, t_s, _ = x_sample.shape
    rows_p = n_p * s_p
    rows_s = n_s * t_s

    def row3(a):
        return a.reshape(DEPTH, 1, a.shape[-1])

    g_mix3, g_ffn3, g_ple3 = row3(g_mix), row3(g_ffn), row3(g_ple)
    cab3, lag3, lab3, lbg3, lbb3 = row3(conv_a_b), row3(ln_a_g), row3(ln_a_b), row3(ln_b_g), row3(ln_b_b)
    gq3, gk3 = row3(g_q), row3(g_k)
    sb_rep = jnp.repeat(jnp.swapaxes(sgu_b, 1, 2), HEAD_DIM, axis=2)
    sw8 = jnp.repeat(jnp.transpose(sgu_w[:, :, :t_s, :t_s], (0, 3, 2, 1)), HEAD_DIM, axis=3)
    w_out_d = jnp.pad(w_out[:, ABC_PAD - 64:].reshape(DEPTH, 3, GROUP_WIDTH, D_MODEL),
                      ((0, 0), (0, 0), (0, OD_PAD // 3 - GROUP_WIDTH), (0, 0))).reshape(DEPTH, OD_PAD, D_MODEL)
    w_router = jnp.concatenate(
        [jnp.transpose(w_router_exp, (0, 2, 1, 3)).reshape(DEPTH, D_MODEL, N_EXPERTS), w_router_grp,
         jnp.zeros((DEPTH, D_MODEL, ROUTER_COLS - N_EXPERTS - MOE_GROUPS), F32)], axis=2)
    b_router = jnp.concatenate(
        [b_router_exp.reshape(DEPTH, N_EXPERTS), b_router_grp,
         jnp.zeros((DEPTH, ROUTER_COLS - N_EXPERTS - MOE_GROUPS), F32)], axis=1).reshape(DEPTH, 1, ROUTER_COLS)
    tail_w = (w_out, w_out_d, g_ffn3, w_router, b_router, w_gate, w_up, w_down, g_ple3, w_ple_gate, w_ple_proj)
    mix_w_p = (conv_a_w, cab3, lag3, lab3, lbg3, lbb3, sgu_w, sb_rep, conv_c_w, gq3, gk3)
    mix_w_s = (conv_a_w, cab3, lag3, lab3, lbg3, lbb3, sw8, sb_rep, conv_c_w, gq3, gk3)

    caches = [c.reshape(c.shape[0], c.shape[1], c.shape[2], 2 * GROUP_WIDTH)
              for c in (cache_kv_w128, cache_kv_w512, cache_kv_w2048)]
    p_p = p_prompt.reshape(DEPTH, rows_p, PLE_DIM)
    rows_t = n_p * TAIL
    rows_h = rows_s + rows_t
    p_h = jnp.concatenate([p_sample.reshape(DEPTH, rows_s, PLE_DIM),
                           p_prompt[:, :, s_p - TAIL:].reshape(DEPTH, rows_t, PLE_DIM)], axis=1)
    h_h = jnp.concatenate([x_sample.reshape(rows_s, D_MODEL),
                           x_prompt[:, s_p - TAIL:].reshape(rows_t, D_MODEL)], axis=0)

    h = x_prompt.reshape(rows_p, D_MODEL)
    st_a, st_c, st_kv = [], [], [[], [], []]
    sa, sc, sv, skv = [], [], [], [[], [], []]
    for i in range(DEPTH):
        z_h = _inproj(h_h, g_mix3, w_in, i, rows_h, 512, hp=True)
        z = _inproj(h, g_mix3, w_in, i, 1024, 512).reshape(n_p, s_p, IN_COLS)
        z = _patch_tail(z, z_h[rows_s:].reshape(n_p, TAIL, IN_COLS))
        (mix, q1, kv1, q4, kv4, q16, kv16, sta, stc, s1, s4, s16) = _mixers_prompt(z, i, mix_w_p)
        o1, l1 = _attn_prompt(q1, kv1)
        o4, l4 = _attn_prompt(q4, kv4)
        o16, l16 = _attn_prompt(q16, kv16)
        od = _combine_prompt(o1, l1, o4, l4, o16, l16)
        (mix_s, od_s, na, nc, cv, n1, n4, n16) = _mixers_sample(
            z_h[:rows_s].reshape(n_s, t_s, IN_COLS), i, state_conv_a, state_conv_c, caches, mix_w_s)
        mix_h = jnp.concatenate([mix_s.reshape(rows_s, ABC_PAD),
                                 mix[:, s_p - TAIL:].reshape(rows_t, ABC_PAD).astype(F32)], axis=0)
        od_h = jnp.concatenate([od_s.reshape(rows_s, OD_PAD),
                                od[:, s_p - TAIL:].reshape(rows_t, OD_PAD).astype(F32)], axis=0)
        h = _token_tail(h, mix.reshape(rows_p, ABC_PAD), od.reshape(rows_p, OD_PAD), p_p, i, tail_w, 1024, False)
        h_h = _token_tail(h_h, mix_h, od_h, p_h, i, tail_w, rows_h, True)
        st_a.append(sta[:, A_HALO - (A_CONV_LEN - 1):])
        st_c.append(stc[:, C_HALO - (C_CONV_LEN - 1):])
        for g, s_kv in enumerate((s1, s4, s16)):
            st_kv[g].append(s_kv.reshape(n_p, s_kv.shape[1], 2, HEADS_PER_GROUP, HEAD_DIM))
        sa.append(na)
        sc.append(nc)
        sv.append(cv)
        for g, nk in enumerate((n1, n4, n16)):
            skv[g].append(nk.reshape(n_s, nk.shape[1], 2, HEADS_PER_GROUP, HEAD_DIM))
    y_prompt = _patch_tail(h.reshape(n_p, s_p, D_MODEL), h_h[rows_s:].reshape(n_p, TAIL, D_MODEL))
    conv_a_prompt = jnp.stack(st_a)
    conv_c_prompt = jnp.stack(st_c)
    kv_prompt = [jnp.stack(s) for s in st_kv]
    y_sample = h_h[:rows_s].reshape(n_s, t_s, D_MODEL)
    conv_a_sample = jnp.stack(sa)
    conv_c_sample = jnp.stack(sc)
    chunk_v_sample = jnp.stack(sv)
    kv_sample = [jnp.stack(s) for s in skv]

    return (y_prompt, y_sample, conv_a_prompt, conv_a_sample, conv_c_prompt, conv_c_sample, chunk_v_sample,
            kv_prompt[0], kv_sample[0], kv_prompt[1], kv_sample[1], kv_prompt[2], kv_sample[2])
```

```python
import functools

import jax
import jax.numpy as jnp
from jax import lax
from jax.experimental import pallas as pl
from jax.experimental.pallas import tpu as pltpu

F32 = jnp.float32
BF16 = jnp.bfloat16

D_MODEL = 2048
DEPTH = 4
PLE_DIM = 256
HEAD_DIM = 64
A_WIDTH = 512
A_CONV_LEN = 31
B_WIDTH = 512
B_HEADS = 8
CHUNK = 128
C_WIDTH = 448
C_CONV_LEN = 3
ATTN_GROUPS = ((128, 1), (512, 4), (2048, 16))
HEADS_PER_GROUP = 3
WIN_KEYS = 128
D_HEADS = 9
D_WIDTH = D_HEADS * HEAD_DIM
GROUP_WIDTH = HEADS_PER_GROUP * HEAD_DIM
IN_COLS = 2 * A_WIDTH + 2 * B_WIDTH + 3 * C_WIDTH + 3 * D_WIDTH
COL_B = 2 * A_WIDTH
COL_C = COL_B + 2 * B_WIDTH
COL_D = COL_C + 3 * C_WIDTH
ATTN_SCALE = HEAD_DIM ** -0.5
MOE_GROUPS = 4
EXPERTS_PER_GROUP = 4
N_EXPERTS = 16
EXPERT_FF = 512
RMS_EPS = 1e-6
LN_EPS = 1e-5

LANE = 128
SUBLANE = 8
ABC_PAD = 1536
OD_PAD = 3 * 256
ROUTER_COLS = 128
VMEM_LIMIT = 56 * 1024 * 1024
NEG_INF = float("-inf")


def _cparams(sem):
    return pltpu.CompilerParams(dimension_semantics=sem, vmem_limit_bytes=VMEM_LIMIT)


def _rms_rows(x, g):
    return x * lax.rsqrt(jnp.mean(x * x, axis=-1, keepdims=True) + RMS_EPS) * g


def _layernorm_rows(x, g, b):
    mu = jnp.mean(x, axis=-1, keepdims=True)
    xc = x - mu
    var = jnp.mean(xc * xc, axis=-1, keepdims=True)
    return xc * lax.rsqrt(var + LN_EPS) * g + b


def _sigmoid(x):
    return 1.0 / (1.0 + jnp.exp(-x))


def _silu(x):
    return x * _sigmoid(x)


def _gelu(x):
    return 0.5 * x * (1.0 + lax.erf(x * (2.0 ** -0.5)))


def _split_bf16(x):
    hi = x.astype(BF16)
    return hi, (x - hi.astype(F32)).astype(BF16)


def _mm(x, w, hp):
    if not hp:
        return jnp.dot(x.astype(BF16), w.astype(BF16), preferred_element_type=F32)
    rows = x.shape[0]
    xh, xl = _split_bf16(x)
    wh, wl = _split_bf16(w)
    r = jnp.dot(jnp.concatenate([xh, xl], axis=0), wh, preferred_element_type=F32)
    return r[:rows] + r[rows:] + jnp.dot(xh, wl, preferred_element_type=F32)


def _act_dtype(hp):
    return F32 if hp else BF16


def _head_norm(x, g):
    outs = []
    for h in range(D_HEADS):
        xh = x[:, HEAD_DIM * h:HEAD_DIM * (h + 1)]
        outs.append(_rms_rows(xh, g))
    return outs


def _split_qkv(z_ref, rows):
    q_lo = (COL_D // LANE) * LANE
    zq = z_ref[0, rows, q_lo:q_lo + 640]
    q = zq[:, COL_D - q_lo:COL_D - q_lo + D_WIDTH]
    k_lo = COL_D + D_WIDTH
    zk = z_ref[0, rows, k_lo:k_lo + 640]
    k = zk[:, :D_WIDTH]
    v_lo = ((COL_D + 2 * D_WIDTH) // LANE) * LANE
    zv = z_ref[0, rows, v_lo:v_lo + 640]
    v = zv[:, COL_D + 2 * D_WIDTH - v_lo:]
    return q, k, v


def _split_c(z_ref, rows):
    zc = z_ref[0, rows, COL_C:COL_C + 1408]
    return zc[:, 0:C_WIDTH], zc[:, C_WIDTH:2 * C_WIDTH], zc[:, 2 * C_WIDTH:3 * C_WIDTH]


def _inproj_kernel(x_ref, g_ref, w_ref, o_ref, xn_ref, *, hp):
    @pl.when(pl.program_id(1) == 0)
    def _():
        xn_ref[...] = _rms_rows(x_ref[...], g_ref[...]).astype(xn_ref.dtype)

    o_ref[...] = _mm(xn_ref[...], w_ref[...], hp)


def _inproj(h, g_all, w_all, layer, tm, tn, hp=False):
    rows = h.shape[0]
    return pl.pallas_call(
        functools.partial(_inproj_kernel, hp=hp),
        grid=(rows // tm, IN_COLS // tn),
        in_specs=[
            pl.BlockSpec((tm, D_MODEL), lambda i, j: (i, 0)),
            pl.BlockSpec((None, 1, D_MODEL), lambda i, j: (layer, 0, 0)),
            pl.BlockSpec((None, D_MODEL, tn), lambda i, j: (layer, 0, j)),
        ],
        out_specs=pl.BlockSpec((tm, tn), lambda i, j: (i, j)),
        out_shape=jax.ShapeDtypeStruct((rows, IN_COLS), F32),
        scratch_shapes=[pltpu.VMEM((tm, D_MODEL), _act_dtype(hp))],
        compiler_params=_cparams(("arbitrary", "arbitrary")),
        name="inproj",
    )(h, g_all, w_all)


MIX_TT = 256
CONV_ROWS = 64
A_HALO = 32
C_HALO = 8


def _pair_weights(sw_ref, wp_ref):
    row = lax.broadcasted_iota(jnp.int32, (CHUNK, CHUNK), 0)
    col = lax.broadcasted_iota(jnp.int32, (CHUNK, CHUNK), 1)
    keep = col <= row
    for p in range(B_HEADS // 2):
        w0 = jnp.where(keep, sw_ref[2 * p], 0.0)
        w1 = jnp.where(keep, sw_ref[2 * p + 1], 0.0)
        wp_ref[p] = jnp.concatenate([w0, w1], axis=1).astype(BF16)


def _mixer_kernel(z_ref, caw_ref, cab_ref, lag_ref, lab_ref, lbg_ref, lbb_ref, sw_ref, sb_ref, ccw_ref,
                  gq_ref, gk_ref,
                  mix_ref, q1_ref, kv1_ref, q4_ref, kv4_ref, q16_ref, kv16_ref,
                  sta_ref, stc_ref, st1_ref, st4_ref, st16_ref,
                  abuf, cbuf, wp_ref, qs_ref, kvs_ref):
    t = pl.program_id(1)
    tt = MIX_TT

    @pl.when(t == 0)
    def _():
        abuf[0:A_HALO, :] = jnp.zeros((A_HALO, A_WIDTH), F32)
        cbuf[0:C_HALO, :] = jnp.zeros((C_HALO, C_WIDTH), F32)
        _pair_weights(sw_ref, wp_ref)

    @pl.when(t > 0)
    def _():
        abuf[0:A_HALO, :] = abuf[tt:tt + A_HALO, :]
        cbuf[0:C_HALO, :] = cbuf[tt:tt + C_HALO, :]

    za = z_ref[0, :, 0:2 * A_WIDTH]
    abuf[A_HALO:A_HALO + tt, :] = za[:, :A_WIDTH] * _sigmoid(za[:, A_WIDTH:])
    sta_ref[0] = abuf[tt:tt + A_HALO, :]
    base = A_HALO - (A_CONV_LEN - 1)
    for r0 in range(0, tt, CONV_ROWS):
        acc = caw_ref[0:1, :] * abuf[r0 + base:r0 + base + CONV_ROWS, :]
        for j in range(1, A_CONV_LEN):
            acc = acc + caw_ref[j:j + 1, :] * abuf[r0 + base + j:r0 + base + j + CONV_ROWS, :]
        y = _layernorm_rows(acc + cab_ref[...], lag_ref[...], lab_ref[...])
        mix_ref[0, r0:r0 + CONV_ROWS, 0:A_WIDTH] = _silu(y).astype(mix_ref.dtype)

    lane = lax.broadcasted_iota(jnp.int32, (CHUNK, LANE), 1)
    for c0 in range(0, tt, CHUNK):
        gb = _gelu(z_ref[0, c0:c0 + CHUNK, COL_B:COL_B + 2 * B_WIDTH])
        u = gb[:, :B_WIDTH]
        v = _layernorm_rows(gb[:, B_WIDTH:], lbg_ref[...], lbb_ref[...])
        pieces = []
        for p in range(B_HEADS // 2):
            v128 = v[:, LANE * p:LANE * (p + 1)]
            rhs = jnp.concatenate([jnp.where(lane < HEAD_DIM, v128, 0.0),
                                   jnp.where(lane >= HEAD_DIM, v128, 0.0)], axis=0).astype(BF16)
            pieces.append(jnp.dot(wp_ref[p], rhs, preferred_element_type=F32))
        mixed = jnp.concatenate(pieces, axis=1) + sb_ref[...]
        mix_ref[0, c0:c0 + CHUNK, A_WIDTH:A_WIDTH + B_WIDTH] = (u * mixed).astype(mix_ref.dtype)

    g_b, g_c, x_c = _split_c(z_ref, slice(None))
    cbuf[C_HALO:C_HALO + tt, :] = g_c * x_c
    stc_ref[0] = cbuf[tt:tt + C_HALO, :]
    cbase = C_HALO - (C_CONV_LEN - 1)
    conv = ccw_ref[0:1, :] * cbuf[cbase:cbase + tt, :]
    for j in range(1, C_CONV_LEN):
        conv = conv + ccw_ref[j:j + 1, :] * cbuf[cbase + j:cbase + j + tt, :]
    o_c = jnp.concatenate([g_b * conv, jnp.zeros((tt, ABC_PAD - 2 * A_WIDTH - C_WIDTH), F32)], axis=1)
    mix_ref[0, :, 2 * A_WIDTH:ABC_PAD] = o_c.astype(mix_ref.dtype)

    q, k, v = _split_qkv(z_ref, slice(None))
    qn = _head_norm(q, gq_ref[...] * ATTN_SCALE)
    kn = _head_norm(k, gk_ref[...])
    outs = ((q1_ref, kv1_ref, st1_ref), (q4_ref, kv4_ref, st4_ref), (q16_ref, kv16_ref, st16_ref))
    for g, (window, dil) in enumerate(ATTN_GROUPS):
        q_ref, kv_ref, st_ref = outs[g]
        hs = slice(HEADS_PER_GROUP * g, HEADS_PER_GROUP * (g + 1))
        q_g = jnp.concatenate(qn[hs], axis=1)
        kv_g = jnp.concatenate(kn[hs] + [v[:, GROUP_WIDTH * g:GROUP_WIDTH * (g + 1)]], axis=1)
        keep = min(window, tt)
        st_ref[0] = kv_g[tt - keep:, :]
        if dil == 1:
            q_ref[0, 0] = q_g
            kv_ref[0, 0] = kv_g
        else:
            qs_ref[0] = q_g[:, :LANE]
            qs_ref[1] = jnp.concatenate([q_g[:, LANE:], jnp.zeros((tt, 2 * LANE - GROUP_WIDTH), F32)], axis=1)
            for i in range(3):
                kvs_ref[i] = kv_g[:, LANE * i:LANE * (i + 1)]
            for r in range(dil):
                rows = pl.ds(r, tt // dil, stride=dil)
                q_ref[0, r, :, 0:LANE] = qs_ref[0, rows, :]
                q_ref[0, r, :, LANE:GROUP_WIDTH] = qs_ref[1, rows, :][:, :GROUP_WIDTH - LANE]
                for i in range(3):
                    kv_ref[0, r, :, LANE * i:LANE * (i + 1)] = kvs_ref[i, rows, :]


def _mixers_prompt(z, layer, wts):
    (caw, cab, lag, lab, lbg, lbb, sw, sb_rep, ccw, gq, gk) = wts
    n, s, _ = z.shape
    tt = MIX_TT
    nt = s // tt

    def lw(shape):
        nd = len(shape)
        return pl.BlockSpec((None,) + shape, lambda b, t: (layer,) + (0,) * nd)

    in_specs = [
        pl.BlockSpec((1, tt, IN_COLS), lambda b, t: (b, t, 0)),
        lw((A_CONV_LEN, A_WIDTH)), lw((1, A_WIDTH)), lw((1, A_WIDTH)), lw((1, A_WIDTH)),
        lw((1, B_WIDTH)), lw((1, B_WIDTH)), lw((B_HEADS, CHUNK, CHUNK)), lw((CHUNK, B_WIDTH)),
        lw((C_CONV_LEN, C_WIDTH)), lw((1, HEAD_DIM)), lw((1, HEAD_DIM)),
    ]
    out_shape = [jax.ShapeDtypeStruct((n, s, ABC_PAD), BF16)]
    out_specs = [pl.BlockSpec((1, tt, ABC_PAD), lambda b, t: (b, t, 0))]
    for _, dil in ATTN_GROUPS:
        for width in (GROUP_WIDTH, 2 * GROUP_WIDTH):
            out_shape.append(jax.ShapeDtypeStruct((n, dil, s // dil, width), F32))
            out_specs.append(pl.BlockSpec((1, dil, tt // dil, width), lambda b, t: (b, 0, t, 0)))
    out_shape.append(jax.ShapeDtypeStruct((n, A_HALO, A_WIDTH), F32))
    out_specs.append(pl.BlockSpec((1, A_HALO, A_WIDTH), lambda b, t: (b, 0, 0)))
    out_shape.append(jax.ShapeDtypeStruct((n, C_HALO, C_WIDTH), F32))
    out_specs.append(pl.BlockSpec((1, C_HALO, C_WIDTH), lambda b, t: (b, 0, 0)))
    for window, _ in ATTN_GROUPS:
        keep = min(window, s)
        blk = min(keep, tt)
        first = (s - keep) // blk
        out_shape.append(jax.ShapeDtypeStruct((n, keep, 2 * GROUP_WIDTH), F32))
        if keep <= tt:
            out_specs.append(pl.BlockSpec((1, blk, 2 * GROUP_WIDTH), lambda b, t: (b, 0, 0)))
        else:
            out_specs.append(pl.BlockSpec((1, blk, 2 * GROUP_WIDTH),
                                          lambda b, t, first=first: (b, jnp.maximum(t - first, 0), 0)))
    return pl.pallas_call(
        _mixer_kernel,
        grid=(n, nt),
        in_specs=in_specs,
        out_specs=out_specs,
        out_shape=out_shape,
        scratch_shapes=[
            pltpu.VMEM((tt + A_HALO, A_WIDTH), F32),
            pltpu.VMEM((tt + C_HALO, C_WIDTH), F32),
            pltpu.VMEM((B_HEADS // 2, CHUNK, 2 * CHUNK), BF16),
            pltpu.VMEM((2, tt, LANE), F32),
            pltpu.VMEM((3, tt, LANE), F32),
        ],
        compiler_params=_cparams(("arbitrary", "arbitrary")),
        name="mixers_prompt",
    )(z, caw, cab, lag, lab, lbg, lbb, sw, sb_rep, ccw, gq, gk)


ATTN_QBLOCKS = 4


def _attn_kernel(q_ref, kvo_ref, kvp_ref, o_ref, l_ref, *, qblocks):
    c = pl.program_id(2)
    qi = lax.broadcasted_iota(jnp.int32, (WIN_KEYS, 2 * WIN_KEYS), 0)
    kj = lax.broadcasted_iota(jnp.int32, (WIN_KEYS, 2 * WIN_KEYS), 1)
    dist = qi + WIN_KEYS - kj
    band = (dist >= 0) & (dist <= WIN_KEYS)
    first_key = jnp.where(c > 0, 0, WIN_KEYS)
    for s in range(qblocks):
        rows = slice(WIN_KEYS * s, WIN_KEYS * (s + 1))
        q = q_ref[0, 0, rows, :]
        kvo = kvo_ref[0, 0, rows, :]
        if s == 0:
            kvp = kvp_ref[0, 0]
            mask = band & (kj >= first_key)
        else:
            kvp = kvo_ref[0, 0, WIN_KEYS * (s - 1):WIN_KEYS * s, :]
            mask = band
        o_parts, l_parts = [], []
        for h in range(HEADS_PER_GROUP):
            ks = slice(HEAD_DIM * h, HEAD_DIM * (h + 1))
            vs = slice(GROUP_WIDTH + HEAD_DIM * h, GROUP_WIDTH + HEAD_DIM * (h + 1))
            qh = q[:, ks].astype(BF16)
            kk = jnp.concatenate([kvp[:, ks], kvo[:, ks]], axis=0).astype(BF16)
            vv = jnp.concatenate([kvp[:, vs], kvo[:, vs]], axis=0).astype(BF16)
            sc = lax.dot_general(qh, kk, (((1,), (1,)), ((), ())), preferred_element_type=F32)
            sc = jnp.where(mask, sc, NEG_INF)
            m = jnp.max(sc, axis=-1, keepdims=True)
            ex = jnp.exp(sc - m)
            den = jnp.sum(ex, axis=-1, keepdims=True)
            probs = (ex / den).astype(BF16)
            o_parts.append(jnp.dot(probs, vv, preferred_element_type=F32))
            l_parts.append(jnp.broadcast_to(m + jnp.log(den), (WIN_KEYS, HEAD_DIM)))
        o_ref[0, 0, rows, :] = jnp.concatenate(o_parts, axis=1)
        l_ref[0, 0, rows, :] = jnp.concatenate(l_parts, axis=1)


def _attn_prompt(q, kv):
    n, dil, sub, _ = q.shape
    qblocks = min(ATTN_QBLOCKS, sub // WIN_KEYS)
    rows = qblocks * WIN_KEYS
    qspec = pl.BlockSpec((1, 1, rows, GROUP_WIDTH), lambda b, r, c: (b, r, c, 0))
    return pl.pallas_call(
        functools.partial(_attn_kernel, qblocks=qblocks),
        grid=(n, dil, sub // rows),
        in_specs=[
            qspec,
            pl.BlockSpec((1, 1, rows, 2 * GROUP_WIDTH), lambda b, r, c: (b, r, c, 0)),
            pl.BlockSpec((1, 1, WIN_KEYS, 2 * GROUP_WIDTH),
                         lambda b, r, c: (b, r, jnp.maximum(c * qblocks - 1, 0), 0)),
        ],
        out_specs=[qspec, qspec],
        out_shape=[jax.ShapeDtypeStruct(q.shape, F32), jax.ShapeDtypeStruct(q.shape, F32)],
        compiler_params=_cparams(("arbitrary", "arbitrary", "arbitrary")),
        name="attn_prompt",
    )(q, kv, kv)


def _combine_kernel(o1_ref, l1_ref, o4_ref, l4_ref, o16_ref, l16_ref, od_ref, s_o4, s_l4, s_o16, s_l16):
    tt = MIX_TT
    for dil, src, dst in ((4, o4_ref, s_o4), (4, l4_ref, s_l4), (16, o16_ref, s_o16), (16, l16_ref, s_l16)):
        for r in range(dil):
            x = src[0, r]
            rows = pl.ds(r, tt // dil, stride=dil)
            dst[0, rows, :] = x[:, :LANE]
            dst[1, rows, :] = jnp.concatenate(
                [x[:, LANE:], jnp.zeros((tt // dil, 2 * LANE - GROUP_WIDTH), F32)], axis=1)

    def whole(scr):
        return jnp.concatenate([scr[0], scr[1][:, :GROUP_WIDTH - LANE]], axis=1)

    outs = (o1_ref[0, 0], whole(s_o4), whole(s_o16))
    lses = (l1_ref[0, 0], whole(s_l4), whole(s_l16))
    mx = jnp.maximum(jnp.maximum(lses[0], lses[1]), lses[2])
    es = [jnp.exp(l - mx) for l in lses]
    den = es[0] + es[1] + es[2]
    pad = jnp.zeros((tt, OD_PAD // 3 - GROUP_WIDTH), F32)
    parts = []
    for g in range(3):
        parts += [outs[g] * (es[g] / den), pad]
    od_ref[0] = jnp.concatenate(parts, axis=1).astype(od_ref.dtype)


def _combine_prompt(o1, l1, o4, l4, o16, l16):
    n, _, s, _ = o1.shape
    tt = MIX_TT

    def spec(dil):
        return pl.BlockSpec((1, dil, tt // dil, GROUP_WIDTH), lambda b, t: (b, 0, t, 0))

    return pl.pallas_call(
        _combine_kernel,
        grid=(n, s // tt),
        in_specs=[spec(1), spec(1), spec(4), spec(4), spec(16), spec(16)],
        out_specs=pl.BlockSpec((1, tt, OD_PAD), lambda b, t: (b, t, 0)),
        out_shape=jax.ShapeDtypeStruct((n, s, OD_PAD), BF16),
        scratch_shapes=[pltpu.VMEM((2, tt, LANE), F32)] * 4,
        compiler_params=_cparams(("arbitrary", "arbitrary")),
        name="combine_prompt",
    )(o1, l1, o4, l4, o16, l16)


def _dec_kernel(z_ref, ha_ref, hc_ref, c1_ref, c4_ref, c16_ref,
                caw_ref, cab_ref, lag_ref, lab_ref, lbg_ref, lbb_ref, sw8_ref, sb_ref, ccw_ref, gq_ref, gk_ref,
                mix_ref, od_ref, na_ref, nc_ref, cv_ref, n1_ref, n4_ref, n16_ref,
                abuf, cbuf, kvbuf, exbuf):
    t_new = z_ref.shape[1]
    hist_a = A_CONV_LEN - 1
    hist_c = C_CONV_LEN - 1

    za = z_ref[0, :, 0:2 * A_WIDTH]
    abuf[0:hist_a, :] = ha_ref[0]
    abuf[hist_a:hist_a + t_new, :] = za[:, :A_WIDTH] * _sigmoid(za[:, A_WIDTH:])
    acc = caw_ref[0:1, :] * abuf[0:t_new, :]
    for j in range(1, A_CONV_LEN):
        acc = acc + caw_ref[j:j + 1, :] * abuf[j:j + t_new, :]
    y = _layernorm_rows(acc + cab_ref[...], lag_ref[...], lab_ref[...])
    mix_ref[0, :, 0:A_WIDTH] = _silu(y).astype(mix_ref.dtype)
    na_ref[0] = abuf[t_new:t_new + hist_a, :]

    gb = _gelu(z_ref[0, :, COL_B:COL_B + 2 * B_WIDTH])
    u = gb[:, :B_WIDTH]
    v = _layernorm_rows(gb[:, B_WIDTH:], lbg_ref[...], lbb_ref[...])
    cv_ref[0] = v
    row = lax.broadcasted_iota(jnp.int32, (t_new, B_WIDTH), 0)
    mixed = sb_ref[0:t_new, :]
    for s in range(t_new):
        mixed = mixed + jnp.where(row >= s, sw8_ref[s], 0.0) * v[s:s + 1, :]
    mix_ref[0, :, A_WIDTH:A_WIDTH + B_WIDTH] = (u * mixed).astype(mix_ref.dtype)

    g_b, g_c, x_c = _split_c(z_ref, slice(None))
    cbuf[0:hist_c, :] = hc_ref[0]
    cbuf[hist_c:hist_c + t_new, :] = g_c * x_c
    conv = ccw_ref[0:1, :] * cbuf[0:t_new, :]
    for j in range(1, C_CONV_LEN):
        conv = conv + ccw_ref[j:j + 1, :] * cbuf[j:j + t_new, :]
    o_c = jnp.concatenate([g_b * conv, jnp.zeros((t_new, ABC_PAD - 2 * A_WIDTH - C_WIDTH), F32)], axis=1)
    mix_ref[0, :, 2 * A_WIDTH:ABC_PAD] = o_c.astype(mix_ref.dtype)
    nc_ref[0] = cbuf[t_new:t_new + hist_c, :]

    q, k, v_d = _split_qkv(z_ref, slice(None))
    qn = _head_norm(q, gq_ref[...] * ATTN_SCALE)
    kn = _head_norm(k, gk_ref[...])
    caches = (c1_ref, c4_ref, c16_ref)
    news = (n1_ref, n4_ref, n16_ref)
    qrow = lax.broadcasted_iota(jnp.int32, (LANE, 2 * GROUP_WIDTH), 0)
    qlane = lax.broadcasted_iota(jnp.int32, (LANE, 2 * GROUP_WIDTH), 1)
    qmask = (qrow >> 3) == (qlane >> 6)
    zeros_q = jnp.zeros((t_new, GROUP_WIDTH), F32)
    lses, dens, offs = [], [], []
    off = 0
    for g, (window, dil) in enumerate(ATTN_GROUPS):
        hs = slice(HEADS_PER_GROUP * g, HEADS_PER_GROUP * (g + 1))
        buf_len = caches[g].shape[1]
        rows = buf_len + t_new
        kv_new = jnp.concatenate(kn[hs] + [v_d[:, GROUP_WIDTH * g:GROUP_WIDTH * (g + 1)]], axis=1)
        kvbuf[off:off + buf_len, :] = caches[g][0]
        kvbuf[off + buf_len:off + rows, :] = kv_new
        news[g][0] = kvbuf[off + t_new:off + rows, :]
        q_g = jnp.concatenate(qn[hs] + [zeros_q], axis=1)
        q_rep = jnp.concatenate([q_g] * HEADS_PER_GROUP
                                + [jnp.zeros((LANE - HEADS_PER_GROUP * t_new, 2 * GROUP_WIDTH), F32)], axis=0)
        q_hi, q_lo = _split_bf16(jnp.where(qmask, q_rep, 0.0))
        kv_hi, kv_lo = _split_bf16(kvbuf[off:off + rows, :])
        nt_dims = (((1,), (1,)), ((), ()))
        sc = (lax.dot_general(kv_hi, q_hi, nt_dims, preferred_element_type=F32)
              + lax.dot_general(kv_lo, q_hi, nt_dims, preferred_element_type=F32)
              + lax.dot_general(kv_hi, q_lo, nt_dims, preferred_element_type=F32))
        krow = lax.broadcasted_iota(jnp.int32, (rows, LANE), 0)
        tok = lax.broadcasted_iota(jnp.int32, (rows, LANE), 1) & (t_new - 1)
        dist = buf_len + tok - krow
        valid = (dist >= 0) & (dist <= dil * WIN_KEYS) & ((dist & (dil - 1)) == 0)
        sc = jnp.where(valid, sc, NEG_INF)
        m = jnp.max(sc, axis=0, keepdims=True)
        ex = jnp.exp(sc - m)
        den = jnp.sum(ex, axis=0, keepdims=True)
        exbuf[off:off + rows, :] = ex
        lses.append(m + jnp.log(den))
        dens.append(den)
        offs.append((off, rows))
        off += rows
    mx = jnp.maximum(jnp.maximum(lses[0], lses[1]), lses[2])
    es = [jnp.exp(l - mx) for l in lses]
    tot = es[0] + es[1] + es[2]
    lane = lax.broadcasted_iota(jnp.int32, (t_new, 2 * GROUP_WIDTH), 1)
    pad = jnp.zeros((t_new, OD_PAD // 3 - GROUP_WIDTH), F32)
    parts = []
    for g in range(3):
        off, rows = offs[g]
        coef = es[g] / (tot * dens[g])
        probs = (exbuf[off:off + rows, :] * coef).astype(BF16)
        kv_all = kvbuf[off:off + rows, :].astype(BF16)
        o_t = lax.dot_general(probs, kv_all, (((0,), (0,)), ((), ())), preferred_element_type=F32)
        o_g = jnp.zeros((t_new, 2 * GROUP_WIDTH), F32)
        for h in range(HEADS_PER_GROUP):
            sel = (lane >= GROUP_WIDTH + HEAD_DIM * h) & (lane < GROUP_WIDTH + HEAD_DIM * (h + 1))
            o_g = o_g + jnp.where(sel, o_t[t_new * h:t_new * (h + 1), :], 0.0)
        parts += [o_g[:, GROUP_WIDTH:], pad]
    od_ref[0] = jnp.concatenate(parts, axis=1).astype(od_ref.dtype)


def _mixers_sample(z, layer, hist_a, hist_c, caches, wts):
    (caw, cab, lag, lab, lbg, lbb, sw8, sb_rep, ccw, gq, gk) = wts
    n, t_new, _ = z.shape

    def lw(shape):
        nd = len(shape)
        return pl.BlockSpec((None,) + shape, lambda b: (layer,) + (0,) * nd)

    def st(shape):
        nd = len(shape)
        return pl.BlockSpec((None, 1) + shape, lambda b: (layer, b) + (0,) * nd)

    lens = [c.shape[2] for c in caches]
    total_rows = sum(lens) + 3 * t_new
    in_specs = [
        pl.BlockSpec((1, t_new, IN_COLS), lambda b: (b, 0, 0)),
        st((A_CONV_LEN - 1, A_WIDTH)), st((C_CONV_LEN - 1, C_WIDTH)),
        st((lens[0], 2 * GROUP_WIDTH)), st((lens[1], 2 * GROUP_WIDTH)), st((lens[2], 2 * GROUP_WIDTH)),
        lw((A_CONV_LEN, A_WIDTH)), lw((1, A_WIDTH)), lw((1, A_WIDTH)), lw((1, A_WIDTH)),
        lw((1, B_WIDTH)), lw((1, B_WIDTH)), lw((t_new, t_new, B_WIDTH)), lw((CHUNK, B_WIDTH)),
        lw((C_CONV_LEN, C_WIDTH)), lw((1, HEAD_DIM)), lw((1, HEAD_DIM)),
    ]

    def ob(shape):
        nd = len(shape)
        return pl.BlockSpec((1,) + shape, lambda b: (b,) + (0,) * nd)

    out_shape = [
        jax.ShapeDtypeStruct((n, t_new, ABC_PAD), F32),
        jax.ShapeDtypeStruct((n, t_new, OD_PAD), F32),
        jax.ShapeDtypeStruct((n, A_CONV_LEN - 1, A_WIDTH), F32),
        jax.ShapeDtypeStruct((n, C_CONV_LEN - 1, C_WIDTH), F32),
        jax.ShapeDtypeStruct((n, t_new, B_WIDTH), F32),
    ] + [jax.ShapeDtypeStruct((n, ln, 2 * GROUP_WIDTH), F32) for ln in lens]
    out_specs = [ob(s.shape[1:]) for s in out_shape]
    return pl.pallas_call(
        _dec_kernel,
        grid=(n,),
        in_specs=in_specs,
        out_specs=out_specs,
        out_shape=out_shape,
        scratch_shapes=[
            pltpu.VMEM((A_CONV_LEN - 1 + t_new + 2, A_WIDTH), F32),
            pltpu.VMEM((16, C_WIDTH), F32),
            pltpu.VMEM((total_rows, 2 * GROUP_WIDTH), F32),
            pltpu.VMEM((total_rows, LANE), F32),
        ],
        compiler_params=_cparams(("arbitrary",)),
        name="mixers_sample",
    )(z, hist_a, hist_c, *caches, caw, cab, lag, lab, lbg, lbb, sw8, sb_rep, ccw, gq, gk)


def _outproj_kernel(h_ref, mix_ref, od_ref, w_ref, wd_ref, o_ref, *, hp):
    acc = _mm(mix_ref[...], w_ref[0:ABC_PAD, :], hp) + _mm(od_ref[...], wd_ref[...], hp)
    o_ref[...] = h_ref[...] + acc


def _outproj(h, mix, od, w_all, wd_all, layer, tm, tn, hp):
    rows = h.shape[0]
    return pl.pallas_call(
        functools.partial(_outproj_kernel, hp=hp),
        grid=(rows // tm, D_MODEL // tn),
        in_specs=[
            pl.BlockSpec((tm, tn), lambda i, j: (i, j)),
            pl.BlockSpec((tm, ABC_PAD), lambda i, j: (i, 0)),
            pl.BlockSpec((tm, OD_PAD), lambda i, j: (i, 0)),
            pl.BlockSpec((None, D_MODEL, tn), lambda i, j: (layer, 0, j)),
            pl.BlockSpec((None, OD_PAD, tn), lambda i, j: (layer, 0, j)),
        ],
        out_specs=pl.BlockSpec((tm, tn), lambda i, j: (i, j)),
        out_shape=jax.ShapeDtypeStruct((rows, D_MODEL), F32),
        compiler_params=_cparams(("arbitrary", "arbitrary")),
        name="outproj",
    )(h, mix, od, w_all, wd_all)


def _top2_in_top_group(logits):
    rows = logits.shape[0]
    lane = lax.broadcasted_iota(jnp.int32, (rows, ROUTER_COLS), 1)
    big = jnp.int32(ROUTER_COLS)
    is_grp = (lane >= N_EXPERTS) & (lane < N_EXPERTS + MOE_GROUPS)
    gl = jnp.where(is_grp, logits, NEG_INF)
    gmax = jnp.max(gl, axis=-1, keepdims=True)
    gidx = jnp.min(jnp.where(gl == gmax, lane - N_EXPERTS, big), axis=-1, keepdims=True)
    gate = 1.0 / jnp.sum(jnp.where(is_grp, jnp.exp(gl - gmax), 0.0), axis=-1, keepdims=True)
    in_grp = (lane < N_EXPERTS) & ((lane >> 2) == gidx)
    el = jnp.where(in_grp, logits, NEG_INF)
    t1 = jnp.max(el, axis=-1, keepdims=True)
    i1 = jnp.min(jnp.where(el == t1, lane, big), axis=-1, keepdims=True)
    el2 = jnp.where(lane == i1, NEG_INF, el)
    t2 = jnp.max(el2, axis=-1, keepdims=True)
    i2 = jnp.min(jnp.where(el2 == t2, lane, big), axis=-1, keepdims=True)
    e2 = jnp.exp(t2 - t1)
    return lane, i1, i2, gate / (1.0 + e2), gate * e2 / (1.0 + e2)


META_E1, META_E2, META_R1, META_R2, META_W1, META_W2 = range(6)


def _router_sparse_kernel(h_ref, g_ref, wr_ref, br_ref, m_ref, meta_ref, meta_t_ref, cnt_ref, carry_ref):
    @pl.when(pl.program_id(0) == 0)
    def _():
        carry_ref[...] = jnp.zeros_like(carry_ref)

    m = _rms_rows(h_ref[...], g_ref[...])
    m_ref[...] = m
    logits = _mm(m, wr_ref[...], False) + br_ref[...]
    rows = logits.shape[0]
    lane, i1, i2, w1, w2 = _top2_in_top_group(logits)
    sel = jnp.where((lane == i1) | (lane == i2), 1.0, 0.0)
    r = lax.broadcasted_iota(jnp.int32, (rows, rows), 0)
    c = lax.broadcasted_iota(jnp.int32, (rows, rows), 1)
    earlier = jnp.where(c < r, 1.0, 0.0).astype(BF16)
    rank = jnp.dot(earlier, sel.astype(BF16), preferred_element_type=F32) + carry_ref[...]
    r1 = jnp.sum(jnp.where(lane == i1, rank, 0.0), axis=-1, keepdims=True)
    r2 = jnp.sum(jnp.where(lane == i2, rank, 0.0), axis=-1, keepdims=True)
    carry_ref[...] += jnp.sum(sel, axis=0, keepdims=True)
    cnt_ref[...] = carry_ref[...]
    meta = jnp.zeros((rows, ROUTER_COLS), F32)
    for pos, val in ((META_E1, i1.astype(F32)), (META_E2, i2.astype(F32)), (META_R1, r1), (META_R2, r2),
                     (META_W1, w1), (META_W2, w2)):
        meta = jnp.where(lane == pos, val, meta)
    meta_ref[...] = meta
    meta_t_ref[...] = meta.T[:SUBLANE, :]


def _router_sparse(h, g_all, wr_all, br_all, layer, tm):
    rows = h.shape[0]
    return pl.pallas_call(
        _router_sparse_kernel,
        grid=(rows // tm,),
        in_specs=[
            pl.BlockSpec((tm, D_MODEL), lambda i: (i, 0)),
            pl.BlockSpec((None, 1, D_MODEL), lambda i: (layer, 0, 0)),
            pl.BlockSpec((None, D_MODEL, ROUTER_COLS), lambda i: (layer, 0, 0)),
            pl.BlockSpec((None, 1, ROUTER_COLS), lambda i: (layer, 0, 0)),
        ],
        out_specs=[pl.BlockSpec((tm, D_MODEL), lambda i: (i, 0)),
                   pl.BlockSpec((tm, ROUTER_COLS), lambda i: (i, 0)),
                   pl.BlockSpec((SUBLANE, tm), lambda i: (0, i)),
                   pl.BlockSpec((1, ROUTER_COLS), lambda i: (0, 0))],
        out_shape=[jax.ShapeDtypeStruct((rows, D_MODEL), F32),
                   jax.ShapeDtypeStruct((rows, ROUTER_COLS), F32),
                   jax.ShapeDtypeStruct((SUBLANE, rows), F32),
                   jax.ShapeDtypeStruct((1, ROUTER_COLS), F32)],
        scratch_shapes=[pltpu.VMEM((1, ROUTER_COLS), F32)],
        compiler_params=_cparams(("arbitrary",)),
        name="router_sparse",
    )(h, g_all, wr_all, br_all)


EXPERT_TILE = 256


def _experts_kernel(te_ref, tv_ref, s0_ref, s1_ref, nu_ref,
                    m_hbm, wg_ref, wu_ref, wd_ref, yk_hbm,
                    src_ref, xbuf, ybuf, wgb, wub, wdb, gsem, ssem, *, n_tok):
    i = pl.program_id(0)
    tile = EXPERT_TILE
    nused = nu_ref[0]
    slot = lax.rem(i, 2)

    def rows_moved(t):
        return pl.multiple_of(((tv_ref[t] + SUBLANE - 1) // SUBLANE) * SUBLANE, SUBLANE)

    def gather_copy(code, j, b):
        row = jnp.minimum(code >> 1, n_tok - 1)
        return pltpu.make_async_copy(m_hbm.at[pl.ds(row, 1), :], xbuf.at[b, pl.ds(j, 1), :], gsem.at[b])

    def scatter_copy(code, j, b):
        return pltpu.make_async_copy(ybuf.at[b, pl.ds(j, 1), :], yk_hbm.at[code & 1, pl.ds(code >> 1, 1), :],
                                     ssem.at[b])

    def start_rows(copy_fn, t, b):
        def body(j8, carry):
            for u in range(SUBLANE):
                j = j8 * SUBLANE + u
                copy_fn(src_ref[t * tile + j], j, b).start()
            return carry
        lax.fori_loop(0, rows_moved(t) // SUBLANE, body, 0)

    def wait_gather(t, b):
        n = rows_moved(t)
        pltpu.make_async_copy(m_hbm.at[pl.ds(0, n), :], xbuf.at[b, pl.ds(0, n), :], gsem.at[b]).wait()

    def wait_scatter(t, b):
        n = rows_moved(t)
        pltpu.make_async_copy(ybuf.at[b, pl.ds(0, n), :], yk_hbm.at[0, pl.ds(0, n), :], ssem.at[b]).wait()

    @pl.when(i == 0)
    def _():
        def fill(t, carry):
            src_ref[s0_ref[t]] = 2 * t
            src_ref[s1_ref[t]] = 2 * t + 1
            return carry
        lax.fori_loop(0, n_tok, fill, 0)

        def fill_pad(t, carry):
            def one(j, c):
                src_ref[t * tile + j] = 2 * (n_tok + SUBLANE * lax.rem(t, 2) + lax.rem(j, SUBLANE))
                return c
            lax.fori_loop(tv_ref[t], rows_moved(t), one, 0)
            return carry
        lax.fori_loop(0, nused, fill_pad, 0)
        xbuf[...] = jnp.zeros_like(xbuf)
        for k in range(2):
            spare = pltpu.make_async_copy(xbuf.at[0, pl.ds(0, 2 * SUBLANE), :],
                                          yk_hbm.at[k, pl.ds(n_tok, 2 * SUBLANE), :], ssem.at[0])
            spare.start()
            spare.wait()
        start_rows(gather_copy, 0, 0)

    @pl.when(i < nused)
    def _():
        @pl.when(i + 1 < nused)
        def _():
            start_rows(gather_copy, i + 1, 1 - slot)

        changed = jnp.logical_or(i == 0, te_ref[i] != te_ref[jnp.maximum(i - 1, 0)])

        @pl.when(changed)
        def _():
            wgb[...] = wg_ref[...].astype(BF16)
            wub[...] = wu_ref[...].astype(BF16)
            wdb[...] = wd_ref[...].astype(BF16)

        wait_gather(i, slot)

        @pl.when(i >= 2)
        def _():
            wait_scatter(i - 2, slot)

        x = xbuf[slot].astype(BF16)
        gate = jnp.dot(x, wgb[...], preferred_element_type=F32)
        up = jnp.dot(x, wub[...], preferred_element_type=F32)
        ybuf[slot] = jnp.dot((_silu(gate) * up).astype(BF16), wdb[...], preferred_element_type=F32)
        start_rows(scatter_copy, i, slot)

        @pl.when(i == nused - 1)
        def _():
            @pl.when(i >= 1)
            def _():
                wait_scatter(i - 1, 1 - slot)
            wait_scatter(i, slot)


def _experts_sparse(m, tile_expert, tile_rows, slot0, slot1, nused, wg_all, wu_all, wd_all, layer):
    n_tok = m.shape[0]
    n_tiles = tile_expert.shape[0]

    def wspec(shape):
        return pl.BlockSpec((None, None) + shape, lambda i, te, tv, s0, s1, nu: (layer, te[i], 0, 0))

    grid_spec = pltpu.PrefetchScalarGridSpec(
        num_scalar_prefetch=5,
        grid=(n_tiles,),
        in_specs=[pl.BlockSpec(memory_space=pl.ANY),
                  wspec((D_MODEL, EXPERT_FF)), wspec((D_MODEL, EXPERT_FF)), wspec((EXPERT_FF, D_MODEL))],
        out_specs=pl.BlockSpec(memory_space=pl.ANY),
        scratch_shapes=[
            pltpu.SMEM((n_tiles * EXPERT_TILE,), jnp.int32),
            pltpu.VMEM((2, EXPERT_TILE, D_MODEL), F32),
            pltpu.VMEM((2, EXPERT_TILE, D_MODEL), F32),
            pltpu.VMEM((D_MODEL, EXPERT_FF), BF16),
            pltpu.VMEM((D_MODEL, EXPERT_FF), BF16),
            pltpu.VMEM((EXPERT_FF, D_MODEL), BF16),
            pltpu.SemaphoreType.DMA((2,)),
            pltpu.SemaphoreType.DMA((2,)),
        ],
    )
    return pl.pallas_call(
        functools.partial(_experts_kernel, n_tok=n_tok),
        grid_spec=grid_spec,
        out_shape=jax.ShapeDtypeStruct((2, n_tok + 2 * SUBLANE, D_MODEL), F32),
        compiler_params=_cparams(("arbitrary",)),
        name="experts_sparse",
    )(tile_expert, tile_rows, slot0, slot1, nused, m, wg_all, wu_all, wd_all)


def _expert_plan(meta_t, cnt, n_tiles):
    tile = EXPERT_TILE
    counts = cnt[0, :N_EXPERTS].astype(jnp.int32)
    padded = ((counts + tile - 1) // tile) * tile
    ends = jnp.cumsum(padded)
    base = ends - padded
    e1 = meta_t[META_E1].astype(jnp.int32)
    e2 = meta_t[META_E2].astype(jnp.int32)
    slot0 = base[e1] + meta_t[META_R1].astype(jnp.int32)
    slot1 = base[e2] + meta_t[META_R2].astype(jnp.int32)
    start = jnp.arange(n_tiles, dtype=jnp.int32) * tile
    expert_of = jnp.sum((start[:, None] >= ends[None, :]).astype(jnp.int32), axis=1)
    last_used = jnp.max(jnp.where(counts > 0, jnp.arange(N_EXPERTS, dtype=jnp.int32), 0))
    tile_expert = jnp.minimum(expert_of, last_used)
    e_clamped = jnp.minimum(expert_of, N_EXPERTS - 1)
    tile_rows = jnp.clip(counts[e_clamped] - (start - base[e_clamped]), 0, tile)
    tile_rows = jnp.where(expert_of < N_EXPERTS, tile_rows, 0).astype(jnp.int32)
    nused = (ends[-1] // tile).astype(jnp.int32).reshape(1)
    return tile_expert, tile_rows, slot0, slot1, nused


def _router_kernel(h_ref, g_ref, wr_ref, br_ref, m_ref, comb_ref, *, hp):
    m = _rms_rows(h_ref[...], g_ref[...]).astype(m_ref.dtype)
    m_ref[...] = m
    logits = _mm(m, wr_ref[...], hp) + br_ref[...]
    rows = logits.shape[0]
    lane = lax.broadcasted_iota(jnp.int32, (rows, ROUTER_COLS), 1)
    big = jnp.int32(ROUTER_COLS)
    is_grp = (lane >= N_EXPERTS) & (lane < N_EXPERTS + MOE_GROUPS)
    gl = jnp.where(is_grp, logits, NEG_INF)
    gmax = jnp.max(gl, axis=-1, keepdims=True)
    gidx = jnp.min(jnp.where(gl == gmax, lane - N_EXPERTS, big), axis=-1, keepdims=True)
    gate = 1.0 / jnp.sum(jnp.where(is_grp, jnp.exp(gl - gmax), 0.0), axis=-1, keepdims=True)
    in_grp = (lane < N_EXPERTS) & ((lane >> 2) == gidx)
    el = jnp.where(in_grp, logits, NEG_INF)
    t1 = jnp.max(el, axis=-1, keepdims=True)
    i1 = jnp.min(jnp.where(el == t1, lane, big), axis=-1, keepdims=True)
    el2 = jnp.where(lane == i1, NEG_INF, el)
    t2 = jnp.max(el2, axis=-1, keepdims=True)
    i2 = jnp.min(jnp.where(el2 == t2, lane, big), axis=-1, keepdims=True)
    e2 = jnp.exp(t2 - t1)
    w1 = gate / (1.0 + e2)
    w2 = gate * e2 / (1.0 + e2)
    comb_ref[...] = jnp.where(lane == i1, w1, 0.0) + jnp.where(lane == i2, w2, 0.0)


def _router(h, g_all, wr_all, br_all, layer, tm, hp):
    rows = h.shape[0]
    return pl.pallas_call(
        functools.partial(_router_kernel, hp=hp),
        grid=(rows // tm,),
        in_specs=[
            pl.BlockSpec((tm, D_MODEL), lambda i: (i, 0)),
            pl.BlockSpec((None, 1, D_MODEL), lambda i: (layer, 0, 0)),
            pl.BlockSpec((None, D_MODEL, ROUTER_COLS), lambda i: (layer, 0, 0)),
            pl.BlockSpec((None, 1, ROUTER_COLS), lambda i: (layer, 0, 0)),
        ],
        out_specs=[pl.BlockSpec((tm, D_MODEL), lambda i: (i, 0)),
                   pl.BlockSpec((tm, ROUTER_COLS), lambda i: (i, 0))],
        out_shape=[jax.ShapeDtypeStruct((rows, D_MODEL), _act_dtype(hp)),
                   jax.ShapeDtypeStruct((rows, ROUTER_COLS), F32)],
        compiler_params=_cparams(("arbitrary",)),
        name="router",
    )(h, g_all, wr_all, br_all)


def _moe_kernel(h_ref, m_ref, comb_ref, wg_ref, wu_ref, wd_ref, o_ref, *, hp):
    e = pl.program_id(1)

    @pl.when(e == 0)
    def _():
        o_ref[...] = h_ref[...]

    x = m_ref[...]
    gate = _mm(x, wg_ref[...], hp)
    up = _mm(x, wu_ref[...], hp)
    lane = lax.broadcasted_iota(jnp.int32, comb_ref.shape, 1)
    w = jnp.sum(jnp.where(lane == e, comb_ref[...], 0.0), axis=-1, keepdims=True)
    o_ref[...] += _mm(_silu(gate) * up * w, wd_ref[...], hp)


def _moe_dense(h, m, comb, wg_all, wu_all, wd_all, layer, tm, hp):
    rows = h.shape[0]
    return pl.pallas_call(
        functools.partial(_moe_kernel, hp=hp),
        grid=(rows // tm, N_EXPERTS),
        in_specs=[
            pl.BlockSpec((tm, D_MODEL), lambda i, e: (i, 0)),
            pl.BlockSpec((tm, D_MODEL), lambda i, e: (i, 0)),
            pl.BlockSpec((tm, ROUTER_COLS), lambda i, e: (i, 0)),
            pl.BlockSpec((None, None, D_MODEL, EXPERT_FF), lambda i, e: (layer, e, 0, 0)),
            pl.BlockSpec((None, None, D_MODEL, EXPERT_FF), lambda i, e: (layer, e, 0, 0)),
            pl.BlockSpec((None, None, EXPERT_FF, D_MODEL), lambda i, e: (layer, e, 0, 0)),
        ],
        out_specs=pl.BlockSpec((tm, D_MODEL), lambda i, e: (i, 0)),
        out_shape=jax.ShapeDtypeStruct((rows, D_MODEL), F32),
        compiler_params=_cparams(("arbitrary", "arbitrary")),
        name="moe_dense",
    )(h, m, comb, wg_all, wu_all, wd_all)


def _ple_kernel(h_ref, hc_ref, g_ref, p_ref, wg_ref, wp_ref, o_ref, xn_ref, *, hp):
    @pl.when(pl.program_id(1) == 0)
    def _():
        xn_ref[...] = _rms_rows(h_ref[...], g_ref[...]).astype(xn_ref.dtype)

    gate = _sigmoid(_mm(xn_ref[...], wg_ref[...], hp))
    o_ref[...] = hc_ref[...] + gate * _mm(p_ref[...], wp_ref[...], hp)


def _ple(h, p_all, g_all, wg_all, wp_all, layer, tm, tn, hp):
    rows = h.shape[0]
    return pl.pallas_call(
        functools.partial(_ple_kernel, hp=hp),
        grid=(rows // tm, D_MODEL // tn),
        in_specs=[
            pl.BlockSpec((tm, D_MODEL), lambda i, j: (i, 0)),
            pl.BlockSpec((tm, tn), lambda i, j: (i, j)),
            pl.BlockSpec((None, 1, D_MODEL), lambda i, j: (layer, 0, 0)),
            pl.BlockSpec((None, tm, PLE_DIM), lambda i, j: (layer, i, 0)),
            pl.BlockSpec((None, D_MODEL, tn), lambda i, j: (layer, 0, j)),
            pl.BlockSpec((None, PLE_DIM, tn), lambda i, j: (layer, 0, j)),
        ],
        out_specs=pl.BlockSpec((tm, tn), lambda i, j: (i, j)),
        out_shape=jax.ShapeDtypeStruct((rows, D_MODEL), F32),
        scratch_shapes=[pltpu.VMEM((tm, D_MODEL), _act_dtype(hp))],
        compiler_params=_cparams(("arbitrary", "arbitrary")),
        name="ple",
    )(h, h, g_all, p_all, wg_all, wp_all)


def _token_tail_hp(h, mix, od, p_all, layer, tw):
    (w_out, w_out_d, g_ffn, w_router, b_router, w_gate, w_up, w_down, g_ple, w_ple_gate, w_ple_proj) = tw
    tm = h.shape[0]
    h = _outproj(h, mix, od, w_out, w_out_d, layer, tm, 512, True)
    m, comb = _router(h, g_ffn, w_router, b_router, layer, tm, True)
    h = _moe_dense(h, m, comb, w_gate, w_up, w_down, layer, tm, True)
    return _ple(h, p_all, g_ple, w_ple_gate, w_ple_proj, layer, tm, 512, True)


POST_TM = 128
W_CHUNK = 256
Z_CHUNK = 1280
POST_VMEM_LIMIT = 60 * 1024 * 1024


def _stream_cast(w_hbm, layer, dst, stage, sem):
    n_chunks = dst.shape[1] // W_CHUNK

    def chunk_copy(c):
        return pltpu.make_async_copy(w_hbm.at[layer, :, pl.ds(c * W_CHUNK, W_CHUNK)], stage.at[c % 2],
                                     sem.at[c % 2])

    chunk_copy(0).start()
    for c in range(n_chunks):
        if c + 1 < n_chunks:
            chunk_copy(c + 1).start()
        chunk_copy(c).wait()
        dst[:, c * W_CHUNK:(c + 1) * W_CHUNK] = stage[c % 2].astype(BF16)


def _post_kernel(*refs, layer, with_inproj):
    if with_inproj:
        (h_ref, y0_ref, y1_ref, meta_ref, p_ref, gp_ref, wpp_ref, wpg_hbm, gm_ref, win_hbm,
         h_out, z_out, wpg_b, stage, sem, win_b) = refs
    else:
        (h_ref, y0_ref, y1_ref, meta_ref, p_ref, gp_ref, wpp_ref, wpg_hbm,
         h_out, wpg_b, stage, sem) = refs

    @pl.when(pl.program_id(0) == 0)
    def _():
        _stream_cast(wpg_hbm, layer, wpg_b, stage, sem)
        if with_inproj:
            _stream_cast(win_hbm, layer + 1, win_b, stage, sem)

    meta = meta_ref[...]
    lane = lax.broadcasted_iota(jnp.int32, meta.shape, 1)
    w1 = jnp.sum(jnp.where(lane == META_W1, meta, 0.0), axis=-1, keepdims=True)
    w2 = jnp.sum(jnp.where(lane == META_W2, meta, 0.0), axis=-1, keepdims=True)
    hn = h_ref[...] + w1 * y0_ref[...] + w2 * y1_ref[...]
    xn = _rms_rows(hn, gp_ref[...]).astype(BF16)
    gate = _sigmoid(jnp.dot(xn, wpg_b[...], preferred_element_type=F32))
    h3 = hn + gate * _mm(p_ref[...], wpp_ref[...], False)
    h_out[...] = h3
    if with_inproj:
        a = _rms_rows(h3, gm_ref[...]).astype(BF16)
        for c0 in range(0, IN_COLS, Z_CHUNK):
            z_out[:, c0:c0 + Z_CHUNK] = jnp.dot(a, win_b[:, c0:c0 + Z_CHUNK], preferred_element_type=F32)


def _post(h, yk, meta, p_all, g_ple, w_ple_gate, w_ple_proj, g_mix, w_in, layer):
    rows = h.shape[0]
    tm = POST_TM
    with_inproj = layer + 1 < DEPTH
    row_spec = pl.BlockSpec((tm, D_MODEL), lambda i: (i, 0))
    in_specs = [
        row_spec,
        pl.BlockSpec((None, tm, D_MODEL), lambda i: (0, i, 0)),
        pl.BlockSpec((None, tm, D_MODEL), lambda i: (1, i, 0)),
        pl.BlockSpec((tm, ROUTER_COLS), lambda i: (i, 0)),
        pl.BlockSpec((None, tm, PLE_DIM), lambda i: (layer, i, 0)),
        pl.BlockSpec((None, 1, D_MODEL), lambda i: (layer, 0, 0)),
        pl.BlockSpec((None, PLE_DIM, D_MODEL), lambda i: (layer, 0, 0)),
        pl.BlockSpec(memory_space=pl.ANY),
    ]
    args = [h, yk, yk, meta, p_all, g_ple, w_ple_proj, w_ple_gate]
    out_specs = [row_spec]
    out_shape = [jax.ShapeDtypeStruct((rows, D_MODEL), F32)]
    scratch = [pltpu.VMEM((D_MODEL, D_MODEL), BF16),
               pltpu.VMEM((2, D_MODEL, W_CHUNK), F32),
               pltpu.SemaphoreType.DMA((2,))]
    if with_inproj:
        in_specs += [pl.BlockSpec((None, 1, D_MODEL), lambda i: (layer + 1, 0, 0)),
                     pl.BlockSpec(memory_space=pl.ANY)]
        args += [g_mix, w_in]
        out_specs.append(pl.BlockSpec((tm, IN_COLS), lambda i: (i, 0)))
        out_shape.append(jax.ShapeDtypeStruct((rows, IN_COLS), F32))
        scratch.append(pltpu.VMEM((D_MODEL, IN_COLS), BF16))
    return pl.pallas_call(
        functools.partial(_post_kernel, layer=layer, with_inproj=with_inproj),
        grid=(rows // tm,),
        in_specs=in_specs,
        out_specs=out_specs,
        out_shape=out_shape,
        scratch_shapes=scratch,
        compiler_params=pltpu.CompilerParams(dimension_semantics=("arbitrary",),
                                             vmem_limit_bytes=POST_VMEM_LIMIT),
        name="post",
    )(*args)


def _token_tail_prompt(h, mix, od, p_all, layer, tw, g_mix, w_in):
    (w_out, w_out_d, g_ffn, w_router, b_router, w_gate, w_up, w_down, g_ple, w_ple_gate, w_ple_proj) = tw
    rows = h.shape[0]
    h = _outproj(h, mix, od, w_out, w_out_d, layer, 1024, 512, False)
    n_tiles = (2 * rows + N_EXPERTS * (EXPERT_TILE - 1) + EXPERT_TILE - 1) // EXPERT_TILE
    m, meta, meta_t, cnt = _router_sparse(h, g_ffn, w_router, b_router, layer, 512)
    plan = _expert_plan(meta_t, cnt, n_tiles)
    yk = _experts_sparse(m, *plan, w_gate, w_up, w_down, layer)
    out = _post(h, yk, meta, p_all, g_ple, w_ple_gate, w_ple_proj, g_mix, w_in, layer)
    return (out[0], out[1]) if layer + 1 < DEPTH else (out[0], None)


TAIL = 8


def _patch_kernel(x_ref, tail_ref, o_ref):
    del x_ref
    o_ref[...] = tail_ref[...]


def _patch_tail(x, tail):
    n, s, width = x.shape
    return pl.pallas_call(
        _patch_kernel,
        grid=(n,),
        in_specs=[pl.BlockSpec(memory_space=pl.ANY),
                  pl.BlockSpec((1, TAIL, width), lambda b: (b, 0, 0))],
        out_specs=pl.BlockSpec((1, TAIL, width), lambda b: (b, s // TAIL - 1, 0)),
        out_shape=jax.ShapeDtypeStruct(x.shape, x.dtype),
        input_output_aliases={0: 0},
        compiler_params=_cparams(("arbitrary",)),
        name="patch_tail",
    )(x, tail)


def kernel(x_prompt, x_sample, p_prompt, p_sample, state_conv_a, state_conv_c, cache_kv_w128, cache_kv_w512, cache_kv_w2048, g_mix, w_in, conv_a_w, conv_a_b, ln_a_g, ln_a_b, ln_b_g, ln_b_b, sgu_w, sgu_b, conv_c_w, g_q, g_k, w_out, g_ffn, w_router_grp, b_router_grp, w_router_exp, b_router_exp, w_gate, w_up, w_down, g_ple, w_ple_gate, w_ple_proj):
    n_p, s_p, _ = x_prompt.shape
    n_s, t_s, _ = x_sample.shape
    rows_p = n_p * s_p
    rows_s = n_s * t_s

    def row3(a):
        return a.reshape(DEPTH, 1, a.shape[-1])

    g_mix3, g_ffn3, g_ple3 = row3(g_mix), row3(g_ffn), row3(g_ple)
    cab3, lag3, lab3, lbg3, lbb3 = row3(conv_a_b), row3(ln_a_g), row3(ln_a_b), row3(ln_b_g), row3(ln_b_b)
    gq3, gk3 = row3(g_q), row3(g_k)
    sb_rep = jnp.repeat(jnp.swapaxes(sgu_b, 1, 2), HEAD_DIM, axis=2)
    sw8 = jnp.repeat(jnp.transpose(sgu_w[:, :, :t_s, :t_s], (0, 3, 2, 1)), HEAD_DIM, axis=3)
    w_out_d = jnp.pad(w_out[:, ABC_PAD - 64:].reshape(DEPTH, 3, GROUP_WIDTH, D_MODEL),
                      ((0, 0), (0, 0), (0, OD_PAD // 3 - GROUP_WIDTH), (0, 0))).reshape(DEPTH, OD_PAD, D_MODEL)
    w_router = jnp.concatenate(
        [jnp.transpose(w_router_exp, (0, 2, 1, 3)).reshape(DEPTH, D_MODEL, N_EXPERTS), w_router_grp,
         jnp.zeros((DEPTH, D_MODEL, ROUTER_COLS - N_EXPERTS - MOE_GROUPS), F32)], axis=2)
    b_router = jnp.concatenate(
        [b_router_exp.reshape(DEPTH, N_EXPERTS), b_router_grp,
         jnp.zeros((DEPTH, ROUTER_COLS - N_EXPERTS - MOE_GROUPS), F32)], axis=1).reshape(DEPTH, 1, ROUTER_COLS)
    tail_w = (w_out, w_out_d, g_ffn3, w_router, b_router, w_gate, w_up, w_down, g_ple3, w_ple_gate, w_ple_proj)
    mix_w_p = (conv_a_w, cab3, lag3, lab3, lbg3, lbb3, sgu_w, sb_rep, conv_c_w, gq3, gk3)
    mix_w_s = (conv_a_w, cab3, lag3, lab3, lbg3, lbb3, sw8, sb_rep, conv_c_w, gq3, gk3)

    caches = [c.reshape(c.shape[0], c.shape[1], c.shape[2], 2 * GROUP_WIDTH)
              for c in (cache_kv_w128, cache_kv_w512, cache_kv_w2048)]
    p_p = p_prompt.reshape(DEPTH, rows_p, PLE_DIM)
    rows_t = n_p * TAIL
    rows_h = rows_s + rows_t
    p_h = jnp.concatenate([p_sample.reshape(DEPTH, rows_s, PLE_DIM),
                           p_prompt[:, :, s_p - TAIL:].reshape(DEPTH, rows_t, PLE_DIM)], axis=1)
    h_h = jnp.concatenate([x_sample.reshape(rows_s, D_MODEL),
                           x_prompt[:, s_p - TAIL:].reshape(rows_t, D_MODEL)], axis=0)

    h = x_prompt.reshape(rows_p, D_MODEL)
    st_a, st_c, st_kv = [], [], [[], [], []]
    sa, sc, sv, skv = [], [], [], [[], [], []]
    z_next = _inproj(h, g_mix3, w_in, 0, 1024, 512)
    for i in range(DEPTH):
        z_h = _inproj(h_h, g_mix3, w_in, i, rows_h, 512, hp=True)
        z = _patch_tail(z_next.reshape(n_p, s_p, IN_COLS), z_h[rows_s:].reshape(n_p, TAIL, IN_COLS))
        (mix, q1, kv1, q4, kv4, q16, kv16, sta, stc, s1, s4, s16) = _mixers_prompt(z, i, mix_w_p)
        o1, l1 = _attn_prompt(q1, kv1)
        o4, l4 = _attn_prompt(q4, kv4)
        o16, l16 = _attn_prompt(q16, kv16)
        od = _combine_prompt(o1, l1, o4, l4, o16, l16)
        (mix_s, od_s, na, nc, cv, n1, n4, n16) = _mixers_sample(
            z_h[:rows_s].reshape(n_s, t_s, IN_COLS), i, state_conv_a, state_conv_c, caches, mix_w_s)
        mix_h = jnp.concatenate([mix_s.reshape(rows_s, ABC_PAD),
                                 mix[:, s_p - TAIL:].reshape(rows_t, ABC_PAD).astype(F32)], axis=0)
        od_h = jnp.concatenate([od_s.reshape(rows_s, OD_PAD),
                                od[:, s_p - TAIL:].reshape(rows_t, OD_PAD).astype(F32)], axis=0)
        h, z_next = _token_tail_prompt(h, mix.reshape(rows_p, ABC_PAD), od.reshape(rows_p, OD_PAD), p_p, i,
                                       tail_w, g_mix3, w_in)
        h_h = _token_tail_hp(h_h, mix_h, od_h, p_h, i, tail_w)
        st_a.append(sta[:, A_HALO - (A_CONV_LEN - 1):])
        st_c.append(stc[:, C_HALO - (C_CONV_LEN - 1):])
        for g, s_kv in enumerate((s1, s4, s16)):
            st_kv[g].append(s_kv.reshape(n_p, s_kv.shape[1], 2, HEADS_PER_GROUP, HEAD_DIM))
        sa.append(na)
        sc.append(nc)
        sv.append(cv)
        for g, nk in enumerate((n1, n4, n16)):
            skv[g].append(nk.reshape(n_s, nk.shape[1], 2, HEADS_PER_GROUP, HEAD_DIM))
    y_prompt = _patch_tail(h.reshape(n_p, s_p, D_MODEL), h_h[rows_s:].reshape(n_p, TAIL, D_MODEL))
    conv_a_prompt = jnp.stack(st_a)
    conv_c_prompt = jnp.stack(st_c)
    kv_prompt = [jnp.stack(s) for s in st_kv]
    y_sample = h_h[:rows_s].reshape(n_s, t_s, D_MODEL)
    conv_a_sample = jnp.stack(sa)
    conv_c_sample = jnp.stack(sc)
    chunk_v_sample = jnp.stack(sv)
    kv_sample = [jnp.stack(s) for s in skv]

    return (y_prompt, y_sample, conv_a_prompt, conv_a_sample, conv_c_prompt, conv_c_sample, chunk_v_sample,
            kv_prompt[0], kv_sample[0], kv_prompt[1], kv_sample[1], kv_prompt[2], kv_sample[2])
```

```python
import functools

import jax
import jax.numpy as jnp
from jax import lax
from jax.experimental import pallas as pl
from jax.experimental.pallas import tpu as pltpu

F32 = jnp.float32
BF16 = jnp.bfloat16

D_MODEL = 2048
DEPTH = 4
PLE_DIM = 256
HEAD_DIM = 64
A_WIDTH = 512
A_CONV_LEN = 31
B_WIDTH = 512
B_HEADS = 8
CHUNK = 128
C_WIDTH = 448
C_CONV_LEN = 3
ATTN_GROUPS = ((128, 1), (512, 4), (2048, 16))
HEADS_PER_GROUP = 3
WIN_KEYS = 128
D_HEADS = 9
D_WIDTH = D_HEADS * HEAD_DIM
GROUP_WIDTH = HEADS_PER_GROUP * HEAD_DIM
IN_COLS = 2 * A_WIDTH + 2 * B_WIDTH + 3 * C_WIDTH + 3 * D_WIDTH
COL_B = 2 * A_WIDTH
COL_C = COL_B + 2 * B_WIDTH
COL_D = COL_C + 3 * C_WIDTH
ATTN_SCALE = HEAD_DIM ** -0.5
MOE_GROUPS = 4
EXPERTS_PER_GROUP = 4
N_EXPERTS = 16
EXPERT_FF = 512
RMS_EPS = 1e-6
LN_EPS = 1e-5

LANE = 128
SUBLANE = 8
ABC_PAD = 1536
OD_PAD = 3 * 256
ROUTER_COLS = 128
VMEM_LIMIT = 56 * 1024 * 1024
NEG_INF = float("-inf")


def _cparams(sem):
    return pltpu.CompilerParams(dimension_semantics=sem, vmem_limit_bytes=VMEM_LIMIT)


def _rms_rows(x, g):
    return x * lax.rsqrt(jnp.mean(x * x, axis=-1, keepdims=True) + RMS_EPS) * g


def _layernorm_rows(x, g, b):
    mu = jnp.mean(x, axis=-1, keepdims=True)
    xc = x - mu
    var = jnp.mean(xc * xc, axis=-1, keepdims=True)
    return xc * lax.rsqrt(var + LN_EPS) * g + b


def _sigmoid(x):
    return 1.0 / (1.0 + jnp.exp(-x))


def _silu(x):
    return x * _sigmoid(x)


def _gelu(x):
    return 0.5 * x * (1.0 + lax.erf(x * (2.0 ** -0.5)))


def _split_bf16(x):
    hi = x.astype(BF16)
    return hi, (x - hi.astype(F32)).astype(BF16)


def _mm(x, w, hp):
    if not hp:
        return jnp.dot(x.astype(BF16), w.astype(BF16), preferred_element_type=F32)
    rows = x.shape[0]
    xh, xl = _split_bf16(x)
    wh, wl = _split_bf16(w)
    r = jnp.dot(jnp.concatenate([xh, xl], axis=0), wh, preferred_element_type=F32)
    return r[:rows] + r[rows:] + jnp.dot(xh, wl, preferred_element_type=F32)


def _act_dtype(hp):
    return F32 if hp else BF16


def _head_norm(x, g):
    outs = []
    for h in range(D_HEADS):
        xh = x[:, HEAD_DIM * h:HEAD_DIM * (h + 1)]
        outs.append(_rms_rows(xh, g))
    return outs


def _split_qkv(z_ref, rows):
    q_lo = (COL_D // LANE) * LANE
    zq = z_ref[0, rows, q_lo:q_lo + 640]
    q = zq[:, COL_D - q_lo:COL_D - q_lo + D_WIDTH]
    k_lo = COL_D + D_WIDTH
    zk = z_ref[0, rows, k_lo:k_lo + 640]
    k = zk[:, :D_WIDTH]
    v_lo = ((COL_D + 2 * D_WIDTH) // LANE) * LANE
    zv = z_ref[0, rows, v_lo:v_lo + 640]
    v = zv[:, COL_D + 2 * D_WIDTH - v_lo:]
    return q, k, v


def _split_c(z_ref, rows):
    zc = z_ref[0, rows, COL_C:COL_C + 1408]
    return zc[:, 0:C_WIDTH], zc[:, C_WIDTH:2 * C_WIDTH], zc[:, 2 * C_WIDTH:3 * C_WIDTH]


def _inproj_kernel(x_ref, g_ref, w_ref, o_ref, xn_ref, *, hp):
    @pl.when(pl.program_id(1) == 0)
    def _():
        xn_ref[...] = _rms_rows(x_ref[...], g_ref[...]).astype(xn_ref.dtype)

    o_ref[...] = _mm(xn_ref[...], w_ref[...], hp)


def _inproj(h, g_all, w_all, layer, tm, tn, hp=False):
    rows = h.shape[0]
    return pl.pallas_call(
        functools.partial(_inproj_kernel, hp=hp),
        grid=(rows // tm, IN_COLS // tn),
        in_specs=[
            pl.BlockSpec((tm, D_MODEL), lambda i, j: (i, 0)),
            pl.BlockSpec((None, 1, D_MODEL), lambda i, j: (layer, 0, 0)),
            pl.BlockSpec((None, D_MODEL, tn), lambda i, j: (layer, 0, j)),
        ],
        out_specs=pl.BlockSpec((tm, tn), lambda i, j: (i, j)),
        out_shape=jax.ShapeDtypeStruct((rows, IN_COLS), F32),
        scratch_shapes=[pltpu.VMEM((tm, D_MODEL), _act_dtype(hp))],
        compiler_params=_cparams(("arbitrary", "arbitrary")),
        name="inproj",
    )(h, g_all, w_all)


MIX_TT = 256
CONV_ROWS = 64
A_HALO = 32
C_HALO = 8


def _pair_weights(sw_ref, wp_ref):
    row = lax.broadcasted_iota(jnp.int32, (CHUNK, CHUNK), 0)
    col = lax.broadcasted_iota(jnp.int32, (CHUNK, CHUNK), 1)
    keep = col <= row
    for p in range(B_HEADS // 2):
        w0 = jnp.where(keep, sw_ref[2 * p], 0.0)
        w1 = jnp.where(keep, sw_ref[2 * p + 1], 0.0)
        wp_ref[p] = jnp.concatenate([w0, w1], axis=1).astype(BF16)


def _mixer_kernel(z_ref, caw_ref, cab_ref, lag_ref, lab_ref, lbg_ref, lbb_ref, sw_ref, sb_ref, ccw_ref,
                  gq_ref, gk_ref,
                  mix_ref, q1_ref, kv1_ref, q4_ref, kv4_ref, q16_ref, kv16_ref,
                  sta_ref, stc_ref, st1_ref, st4_ref, st16_ref,
                  abuf, cbuf, wp_ref, qs_ref, kvs_ref, ashift):
    t = pl.program_id(1)
    tt = MIX_TT

    @pl.when(t == 0)
    def _():
        abuf[0:A_HALO, :] = jnp.zeros((A_HALO, A_WIDTH), F32)
        cbuf[0:C_HALO, :] = jnp.zeros((C_HALO, C_WIDTH), F32)
        _pair_weights(sw_ref, wp_ref)

    @pl.when(t > 0)
    def _():
        abuf[0:A_HALO, :] = abuf[tt:tt + A_HALO, :]
        cbuf[0:C_HALO, :] = cbuf[tt:tt + C_HALO, :]

    za = z_ref[0, :, 0:2 * A_WIDTH]
    abuf[A_HALO:A_HALO + tt, :] = za[:, :A_WIDTH] * _sigmoid(za[:, A_WIDTH:])
    sta_ref[0] = abuf[tt:tt + A_HALO, :]
    base = A_HALO - (A_CONV_LEN - 1)
    span = tt + A_HALO - SUBLANE
    for s in range(1, SUBLANE):
        ashift[s - 1, 0:span, :] = abuf[s:s + span, :]

    def tap_rows(first):
        phase = first % SUBLANE
        src = abuf if phase == 0 else ashift.at[phase - 1]
        return src[first - phase:first - phase + CONV_ROWS, :]

    for r0 in range(0, tt, CONV_ROWS):
        acc = caw_ref[0:1, :] * tap_rows(r0 + base)
        for j in range(1, A_CONV_LEN):
            acc = acc + caw_ref[j:j + 1, :] * tap_rows(r0 + base + j)
        y = _layernorm_rows(acc + cab_ref[...], lag_ref[...], lab_ref[...])
        mix_ref[0, r0:r0 + CONV_ROWS, 0:A_WIDTH] = _silu(y).astype(mix_ref.dtype)

    lane = lax.broadcasted_iota(jnp.int32, (CHUNK, LANE), 1)
    for c0 in range(0, tt, CHUNK):
        gb = _gelu(z_ref[0, c0:c0 + CHUNK, COL_B:COL_B + 2 * B_WIDTH])
        u = gb[:, :B_WIDTH]
        v = _layernorm_rows(gb[:, B_WIDTH:], lbg_ref[...], lbb_ref[...])
        pieces = []
        for p in range(B_HEADS // 2):
            v128 = v[:, LANE * p:LANE * (p + 1)]
            rhs = jnp.concatenate([jnp.where(lane < HEAD_DIM, v128, 0.0),
                                   jnp.where(lane >= HEAD_DIM, v128, 0.0)], axis=0).astype(BF16)
            pieces.append(jnp.dot(wp_ref[p], rhs, preferred_element_type=F32))
        mixed = jnp.concatenate(pieces, axis=1) + sb_ref[...]
        mix_ref[0, c0:c0 + CHUNK, A_WIDTH:A_WIDTH + B_WIDTH] = (u * mixed).astype(mix_ref.dtype)

    g_b, g_c, x_c = _split_c(z_ref, slice(None))
    cbuf[C_HALO:C_HALO + tt, :] = g_c * x_c
    stc_ref[0] = cbuf[tt:tt + C_HALO, :]
    cbase = C_HALO - (C_CONV_LEN - 1)
    conv = ccw_ref[0:1, :] * cbuf[cbase:cbase + tt, :]
    for j in range(1, C_CONV_LEN):
        conv = conv + ccw_ref[j:j + 1, :] * cbuf[cbase + j:cbase + j + tt, :]
    o_c = jnp.concatenate([g_b * conv, jnp.zeros((tt, ABC_PAD - 2 * A_WIDTH - C_WIDTH), F32)], axis=1)
    mix_ref[0, :, 2 * A_WIDTH:ABC_PAD] = o_c.astype(mix_ref.dtype)

    q, k, v = _split_qkv(z_ref, slice(None))
    qn = _head_norm(q, gq_ref[...] * ATTN_SCALE)
    kn = _head_norm(k, gk_ref[...])
    outs = ((q1_ref, kv1_ref, st1_ref), (q4_ref, kv4_ref, st4_ref), (q16_ref, kv16_ref, st16_ref))
    for g, (window, dil) in enumerate(ATTN_GROUPS):
        q_ref, kv_ref, st_ref = outs[g]
        hs = slice(HEADS_PER_GROUP * g, HEADS_PER_GROUP * (g + 1))
        q_g = jnp.concatenate(qn[hs], axis=1)
        kv_g = jnp.concatenate(kn[hs] + [v[:, GROUP_WIDTH * g:GROUP_WIDTH * (g + 1)]], axis=1)
        keep = min(window, tt)
        st_ref[0] = kv_g[tt - keep:, :]
        if dil == 1:
            q_ref[0, 0] = q_g
            kv_ref[0, 0] = kv_g
        else:
            qs_ref[0] = q_g[:, :LANE]
            qs_ref[1] = jnp.concatenate([q_g[:, LANE:], jnp.zeros((tt, 2 * LANE - GROUP_WIDTH), F32)], axis=1)
            for i in range(3):
                kvs_ref[i] = kv_g[:, LANE * i:LANE * (i + 1)]
            for r in range(dil):
                rows = pl.ds(r, tt // dil, stride=dil)
                q_ref[0, r, :, 0:LANE] = qs_ref[0, rows, :]
                q_ref[0, r, :, LANE:GROUP_WIDTH] = qs_ref[1, rows, :][:, :GROUP_WIDTH - LANE]
                for i in range(3):
                    kv_ref[0, r, :, LANE * i:LANE * (i + 1)] = kvs_ref[i, rows, :]


def _mixers_prompt(z, layer, wts):
    (caw, cab, lag, lab, lbg, lbb, sw, sb_rep, ccw, gq, gk) = wts
    n, s, _ = z.shape
    tt = MIX_TT
    nt = s // tt

    def lw(shape):
        nd = len(shape)
        return pl.BlockSpec((None,) + shape, lambda b, t: (layer,) + (0,) * nd)

    in_specs = [
        pl.BlockSpec((1, tt, IN_COLS), lambda b, t: (b, t, 0)),
        lw((A_CONV_LEN, A_WIDTH)), lw((1, A_WIDTH)), lw((1, A_WIDTH)), lw((1, A_WIDTH)),
        lw((1, B_WIDTH)), lw((1, B_WIDTH)), lw((B_HEADS, CHUNK, CHUNK)), lw((CHUNK, B_WIDTH)),
        lw((C_CONV_LEN, C_WIDTH)), lw((1, HEAD_DIM)), lw((1, HEAD_DIM)),
    ]
    out_shape = [jax.ShapeDtypeStruct((n, s, ABC_PAD), BF16)]
    out_specs = [pl.BlockSpec((1, tt, ABC_PAD), lambda b, t: (b, t, 0))]
    for _, dil in ATTN_GROUPS:
        for width in (GROUP_WIDTH, 2 * GROUP_WIDTH):
            out_shape.append(jax.ShapeDtypeStruct((n, dil, s // dil, width), F32))
            out_specs.append(pl.BlockSpec((1, dil, tt // dil, width), lambda b, t: (b, 0, t, 0)))
    out_shape.append(jax.ShapeDtypeStruct((n, A_HALO, A_WIDTH), F32))
    out_specs.append(pl.BlockSpec((1, A_HALO, A_WIDTH), lambda b, t: (b, 0, 0)))
    out_shape.append(jax.ShapeDtypeStruct((n, C_HALO, C_WIDTH), F32))
    out_specs.append(pl.BlockSpec((1, C_HALO, C_WIDTH), lambda b, t: (b, 0, 0)))
    for window, _ in ATTN_GROUPS:
        keep = min(window, s)
        blk = min(keep, tt)
        first = (s - keep) // blk
        out_shape.append(jax.ShapeDtypeStruct((n, keep, 2 * GROUP_WIDTH), F32))
        if keep <= tt:
            out_specs.append(pl.BlockSpec((1, blk, 2 * GROUP_WIDTH), lambda b, t: (b, 0, 0)))
        else:
            out_specs.append(pl.BlockSpec((1, blk, 2 * GROUP_WIDTH),
                                          lambda b, t, first=first: (b, jnp.maximum(t - first, 0), 0)))
    return pl.pallas_call(
        _mixer_kernel,
        grid=(n, nt),
        in_specs=in_specs,
        out_specs=out_specs,
        out_shape=out_shape,
        scratch_shapes=[
            pltpu.VMEM((tt + A_HALO, A_WIDTH), F32),
            pltpu.VMEM((tt + C_HALO, C_WIDTH), F32),
            pltpu.VMEM((B_HEADS // 2, CHUNK, 2 * CHUNK), BF16),
            pltpu.VMEM((2, tt, LANE), F32),
            pltpu.VMEM((3, tt, LANE), F32),
            pltpu.VMEM((SUBLANE - 1, tt + A_HALO, A_WIDTH), F32),
        ],
        compiler_params=_cparams(("arbitrary", "arbitrary")),
        name="mixers_prompt",
    )(z, caw, cab, lag, lab, lbg, lbb, sw, sb_rep, ccw, gq, gk)


ATTN_QBLOCKS = 4


def _attn_kernel(q_ref, kvo_ref, kvp_ref, o_ref, l_ref, *, qblocks):
    c = pl.program_id(2)
    qi = lax.broadcasted_iota(jnp.int32, (WIN_KEYS, 2 * WIN_KEYS), 0)
    kj = lax.broadcasted_iota(jnp.int32, (WIN_KEYS, 2 * WIN_KEYS), 1)
    dist = qi + WIN_KEYS - kj
    band = (dist >= 0) & (dist <= WIN_KEYS)
    first_key = jnp.where(c > 0, 0, WIN_KEYS)
    for s in range(qblocks):
        rows = slice(WIN_KEYS * s, WIN_KEYS * (s + 1))
        q = q_ref[0, 0, rows, :]
        kvo = kvo_ref[0, 0, rows, :]
        if s == 0:
            kvp = kvp_ref[0, 0]
            mask = band & (kj >= first_key)
        else:
            kvp = kvo_ref[0, 0, WIN_KEYS * (s - 1):WIN_KEYS * s, :]
            mask = band
        o_parts, l_parts = [], []
        for h in range(HEADS_PER_GROUP):
            ks = slice(HEAD_DIM * h, HEAD_DIM * (h + 1))
            vs = slice(GROUP_WIDTH + HEAD_DIM * h, GROUP_WIDTH + HEAD_DIM * (h + 1))
            qh = q[:, ks].astype(BF16)
            kk = jnp.concatenate([kvp[:, ks], kvo[:, ks]], axis=0).astype(BF16)
            vv = jnp.concatenate([kvp[:, vs], kvo[:, vs]], axis=0).astype(BF16)
            sc = lax.dot_general(qh, kk, (((1,), (1,)), ((), ())), preferred_element_type=F32)
            sc = jnp.where(mask, sc, NEG_INF)
            m = jnp.max(sc, axis=-1, keepdims=True)
            ex = jnp.exp(sc - m)
            den = jnp.sum(ex, axis=-1, keepdims=True)
            probs = (ex / den).astype(BF16)
            o_parts.append(jnp.dot(probs, vv, preferred_element_type=F32))
            l_parts.append(jnp.broadcast_to(m + jnp.log(den), (WIN_KEYS, HEAD_DIM)))
        o_ref[0, 0, rows, :] = jnp.concatenate(o_parts, axis=1)
        l_ref[0, 0, rows, :] = jnp.concatenate(l_parts, axis=1)


def _attn_prompt(q, kv):
    n, dil, sub, _ = q.shape
    qblocks = min(ATTN_QBLOCKS, sub // WIN_KEYS)
    rows = qblocks * WIN_KEYS
    qspec = pl.BlockSpec((1, 1, rows, GROUP_WIDTH), lambda b, r, c: (b, r, c, 0))
    return pl.pallas_call(
        functools.partial(_attn_kernel, qblocks=qblocks),
        grid=(n, dil, sub // rows),
        in_specs=[
            qspec,
            pl.BlockSpec((1, 1, rows, 2 * GROUP_WIDTH), lambda b, r, c: (b, r, c, 0)),
            pl.BlockSpec((1, 1, WIN_KEYS, 2 * GROUP_WIDTH),
                         lambda b, r, c: (b, r, jnp.maximum(c * qblocks - 1, 0), 0)),
        ],
        out_specs=[qspec, qspec],
        out_shape=[jax.ShapeDtypeStruct(q.shape, F32), jax.ShapeDtypeStruct(q.shape, F32)],
        compiler_params=_cparams(("arbitrary", "arbitrary", "arbitrary")),
        name="attn_prompt",
    )(q, kv, kv)


def _combine_kernel(o1_ref, l1_ref, o4_ref, l4_ref, o16_ref, l16_ref, od_ref, s_o4, s_l4, s_o16, s_l16):
    tt = MIX_TT
    for dil, src, dst in ((4, o4_ref, s_o4), (4, l4_ref, s_l4), (16, o16_ref, s_o16), (16, l16_ref, s_l16)):
        for r in range(dil):
            x = src[0, r]
            rows = pl.ds(r, tt // dil, stride=dil)
            dst[0, rows, :] = x[:, :LANE]
            dst[1, rows, :] = jnp.concatenate(
                [x[:, LANE:], jnp.zeros((tt // dil, 2 * LANE - GROUP_WIDTH), F32)], axis=1)

    def whole(scr):
        return jnp.concatenate([scr[0], scr[1][:, :GROUP_WIDTH - LANE]], axis=1)

    outs = (o1_ref[0, 0], whole(s_o4), whole(s_o16))
    lses = (l1_ref[0, 0], whole(s_l4), whole(s_l16))
    mx = jnp.maximum(jnp.maximum(lses[0], lses[1]), lses[2])
    es = [jnp.exp(l - mx) for l in lses]
    den = es[0] + es[1] + es[2]
    pad = jnp.zeros((tt, OD_PAD // 3 - GROUP_WIDTH), F32)
    parts = []
    for g in range(3):
        parts += [outs[g] * (es[g] / den), pad]
    od_ref[0] = jnp.concatenate(parts, axis=1).astype(od_ref.dtype)


def _combine_prompt(o1, l1, o4, l4, o16, l16):
    n, _, s, _ = o1.shape
    tt = MIX_TT

    def spec(dil):
        return pl.BlockSpec((1, dil, tt // dil, GROUP_WIDTH), lambda b, t: (b, 0, t, 0))

    return pl.pallas_call(
        _combine_kernel,
        grid=(n, s // tt),
        in_specs=[spec(1), spec(1), spec(4), spec(4), spec(16), spec(16)],
        out_specs=pl.BlockSpec((1, tt, OD_PAD), lambda b, t: (b, t, 0)),
        out_shape=jax.ShapeDtypeStruct((n, s, OD_PAD), BF16),
        scratch_shapes=[pltpu.VMEM((2, tt, LANE), F32)] * 4,
        compiler_params=_cparams(("arbitrary", "arbitrary")),
        name="combine_prompt",
    )(o1, l1, o4, l4, o16, l16)


def _dec_kernel(z_ref, ha_ref, hc_ref, c1_ref, c4_ref, c16_ref,
                caw_ref, cab_ref, lag_ref, lab_ref, lbg_ref, lbb_ref, sw8_ref, sb_ref, ccw_ref, gq_ref, gk_ref,
                mix_ref, od_ref, na_ref, nc_ref, cv_ref, n1_ref, n4_ref, n16_ref,
                abuf, cbuf, kvbuf, exbuf):
    t_new = z_ref.shape[1]
    hist_a = A_CONV_LEN - 1
    hist_c = C_CONV_LEN - 1

    za = z_ref[0, :, 0:2 * A_WIDTH]
    abuf[0:hist_a, :] = ha_ref[0]
    abuf[hist_a:hist_a + t_new, :] = za[:, :A_WIDTH] * _sigmoid(za[:, A_WIDTH:])
    acc = caw_ref[0:1, :] * abuf[0:t_new, :]
    for j in range(1, A_CONV_LEN):
        acc = acc + caw_ref[j:j + 1, :] * abuf[j:j + t_new, :]
    y = _layernorm_rows(acc + cab_ref[...], lag_ref[...], lab_ref[...])
    mix_ref[0, :, 0:A_WIDTH] = _silu(y).astype(mix_ref.dtype)
    na_ref[0] = abuf[t_new:t_new + hist_a, :]

    gb = _gelu(z_ref[0, :, COL_B:COL_B + 2 * B_WIDTH])
    u = gb[:, :B_WIDTH]
    v = _layernorm_rows(gb[:, B_WIDTH:], lbg_ref[...], lbb_ref[...])
    cv_ref[0] = v
    row = lax.broadcasted_iota(jnp.int32, (t_new, B_WIDTH), 0)
    mixed = sb_ref[0:t_new, :]
    for s in range(t_new):
        mixed = mixed + jnp.where(row >= s, sw8_ref[s], 0.0) * v[s:s + 1, :]
    mix_ref[0, :, A_WIDTH:A_WIDTH + B_WIDTH] = (u * mixed).astype(mix_ref.dtype)

    g_b, g_c, x_c = _split_c(z_ref, slice(None))
    cbuf[0:hist_c, :] = hc_ref[0]
    cbuf[hist_c:hist_c + t_new, :] = g_c * x_c
    conv = ccw_ref[0:1, :] * cbuf[0:t_new, :]
    for j in range(1, C_CONV_LEN):
        conv = conv + ccw_ref[j:j + 1, :] * cbuf[j:j + t_new, :]
    o_c = jnp.concatenate([g_b * conv, jnp.zeros((t_new, ABC_PAD - 2 * A_WIDTH - C_WIDTH), F32)], axis=1)
    mix_ref[0, :, 2 * A_WIDTH:ABC_PAD] = o_c.astype(mix_ref.dtype)
    nc_ref[0] = cbuf[t_new:t_new + hist_c, :]

    q, k, v_d = _split_qkv(z_ref, slice(None))
    qn = _head_norm(q, gq_ref[...] * ATTN_SCALE)
    kn = _head_norm(k, gk_ref[...])
    caches = (c1_ref, c4_ref, c16_ref)
    news = (n1_ref, n4_ref, n16_ref)
    qrow = lax.broadcasted_iota(jnp.int32, (LANE, 2 * GROUP_WIDTH), 0)
    qlane = lax.broadcasted_iota(jnp.int32, (LANE, 2 * GROUP_WIDTH), 1)
    qmask = (qrow >> 3) == (qlane >> 6)
    zeros_q = jnp.zeros((t_new, GROUP_WIDTH), F32)
    lses, dens, offs = [], [], []
    off = 0
    for g, (window, dil) in enumerate(ATTN_GROUPS):
        hs = slice(HEADS_PER_GROUP * g, HEADS_PER_GROUP * (g + 1))
        buf_len = caches[g].shape[1]
        rows = buf_len + t_new
        kv_new = jnp.concatenate(kn[hs] + [v_d[:, GROUP_WIDTH * g:GROUP_WIDTH * (g + 1)]], axis=1)
        kvbuf[off:off + buf_len, :] = caches[g][0]
        kvbuf[off + buf_len:off + rows, :] = kv_new
        news[g][0] = kvbuf[off + t_new:off + rows, :]
        q_g = jnp.concatenate(qn[hs] + [zeros_q], axis=1)
        q_rep = jnp.concatenate([q_g] * HEADS_PER_GROUP
                                + [jnp.zeros((LANE - HEADS_PER_GROUP * t_new, 2 * GROUP_WIDTH), F32)], axis=0)
        q_hi, q_lo = _split_bf16(jnp.where(qmask, q_rep, 0.0))
        kv_hi, kv_lo = _split_bf16(kvbuf[off:off + rows, :])
        nt_dims = (((1,), (1,)), ((), ()))
        sc = (lax.dot_general(kv_hi, q_hi, nt_dims, preferred_element_type=F32)
              + lax.dot_general(kv_lo, q_hi, nt_dims, preferred_element_type=F32)
              + lax.dot_general(kv_hi, q_lo, nt_dims, preferred_element_type=F32))
        krow = lax.broadcasted_iota(jnp.int32, (rows, LANE), 0)
        tok = lax.broadcasted_iota(jnp.int32, (rows, LANE), 1) & (t_new - 1)
        dist = buf_len + tok - krow
        valid = (dist >= 0) & (dist <= dil * WIN_KEYS) & ((dist & (dil - 1)) == 0)
        sc = jnp.where(valid, sc, NEG_INF)
        m = jnp.max(sc, axis=0, keepdims=True)
        ex = jnp.exp(sc - m)
        den = jnp.sum(ex, axis=0, keepdims=True)
        exbuf[off:off + rows, :] = ex
        lses.append(m + jnp.log(den))
        dens.append(den)
        offs.append((off, rows))
        off += rows
    mx = jnp.maximum(jnp.maximum(lses[0], lses[1]), lses[2])
    es = [jnp.exp(l - mx) for l in lses]
    tot = es[0] + es[1] + es[2]
    lane = lax.broadcasted_iota(jnp.int32, (t_new, 2 * GROUP_WIDTH), 1)
    pad = jnp.zeros((t_new, OD_PAD // 3 - GROUP_WIDTH), F32)
    parts = []
    for g in range(3):
        off, rows = offs[g]
        coef = es[g] / (tot * dens[g])
        probs = (exbuf[off:off + rows, :] * coef).astype(BF16)
        kv_all = kvbuf[off:off + rows, :].astype(BF16)
        o_t = lax.dot_general(probs, kv_all, (((0,), (0,)), ((), ())), preferred_element_type=F32)
        o_g = jnp.zeros((t_new, 2 * GROUP_WIDTH), F32)
        for h in range(HEADS_PER_GROUP):
            sel = (lane >= GROUP_WIDTH + HEAD_DIM * h) & (lane < GROUP_WIDTH + HEAD_DIM * (h + 1))
            o_g = o_g + jnp.where(sel, o_t[t_new * h:t_new * (h + 1), :], 0.0)
        parts += [o_g[:, GROUP_WIDTH:], pad]
    od_ref[0] = jnp.concatenate(parts, axis=1).astype(od_ref.dtype)


def _mixers_sample(z, layer, hist_a, hist_c, caches, wts):
    (caw, cab, lag, lab, lbg, lbb, sw8, sb_rep, ccw, gq, gk) = wts
    n, t_new, _ = z.shape

    def lw(shape):
        nd = len(shape)
        return pl.BlockSpec((None,) + shape, lambda b: (layer,) + (0,) * nd)

    def st(shape):
        nd = len(shape)
        return pl.BlockSpec((None, 1) + shape, lambda b: (layer, b) + (0,) * nd)

    lens = [c.shape[2] for c in caches]
    total_rows = sum(lens) + 3 * t_new
    in_specs = [
        pl.BlockSpec((1, t_new, IN_COLS), lambda b: (b, 0, 0)),
        st((A_CONV_LEN - 1, A_WIDTH)), st((C_CONV_LEN - 1, C_WIDTH)),
        st((lens[0], 2 * GROUP_WIDTH)), st((lens[1], 2 * GROUP_WIDTH)), st((lens[2], 2 * GROUP_WIDTH)),
        lw((A_CONV_LEN, A_WIDTH)), lw((1, A_WIDTH)), lw((1, A_WIDTH)), lw((1, A_WIDTH)),
        lw((1, B_WIDTH)), lw((1, B_WIDTH)), lw((t_new, t_new, B_WIDTH)), lw((CHUNK, B_WIDTH)),
        lw((C_CONV_LEN, C_WIDTH)), lw((1, HEAD_DIM)), lw((1, HEAD_DIM)),
    ]

    def ob(shape):
        nd = len(shape)
        return pl.BlockSpec((1,) + shape, lambda b: (b,) + (0,) * nd)

    out_shape = [
        jax.ShapeDtypeStruct((n, t_new, ABC_PAD), F32),
        jax.ShapeDtypeStruct((n, t_new, OD_PAD), F32),
        jax.ShapeDtypeStruct((n, A_CONV_LEN - 1, A_WIDTH), F32),
        jax.ShapeDtypeStruct((n, C_CONV_LEN - 1, C_WIDTH), F32),
        jax.ShapeDtypeStruct((n, t_new, B_WIDTH), F32),
    ] + [jax.ShapeDtypeStruct((n, ln, 2 * GROUP_WIDTH), F32) for ln in lens]
    out_specs = [ob(s.shape[1:]) for s in out_shape]
    return pl.pallas_call(
        _dec_kernel,
        grid=(n,),
        in_specs=in_specs,
        out_specs=out_specs,
        out_shape=out_shape,
        scratch_shapes=[
            pltpu.VMEM((A_CONV_LEN - 1 + t_new + 2, A_WIDTH), F32),
            pltpu.VMEM((16, C_WIDTH), F32),
            pltpu.VMEM((total_rows, 2 * GROUP_WIDTH), F32),
            pltpu.VMEM((total_rows, LANE), F32),
        ],
        compiler_params=_cparams(("arbitrary",)),
        name="mixers_sample",
    )(z, hist_a, hist_c, *caches, caw, cab, lag, lab, lbg, lbb, sw8, sb_rep, ccw, gq, gk)


def _outproj_kernel(h_ref, mix_ref, od_ref, w_ref, wd_ref, o_ref, *, hp):
    acc = _mm(mix_ref[...], w_ref[0:ABC_PAD, :], hp) + _mm(od_ref[...], wd_ref[...], hp)
    o_ref[...] = h_ref[...] + acc


def _outproj(h, mix, od, w_all, wd_all, layer, tm, tn, hp):
    rows = h.shape[0]
    return pl.pallas_call(
        functools.partial(_outproj_kernel, hp=hp),
        grid=(rows // tm, D_MODEL // tn),
        in_specs=[
            pl.BlockSpec((tm, tn), lambda i, j: (i, j)),
            pl.BlockSpec((tm, ABC_PAD), lambda i, j: (i, 0)),
            pl.BlockSpec((tm, OD_PAD), lambda i, j: (i, 0)),
            pl.BlockSpec((None, D_MODEL, tn), lambda i, j: (layer, 0, j)),
            pl.BlockSpec((None, OD_PAD, tn), lambda i, j: (layer, 0, j)),
        ],
        out_specs=pl.BlockSpec((tm, tn), lambda i, j: (i, j)),
        out_shape=jax.ShapeDtypeStruct((rows, D_MODEL), F32),
        compiler_params=_cparams(("arbitrary", "arbitrary")),
        name="outproj",
    )(h, mix, od, w_all, wd_all)


def _top2_in_top_group(logits):
    rows = logits.shape[0]
    lane = lax.broadcasted_iota(jnp.int32, (rows, ROUTER_COLS), 1)
    big = jnp.int32(ROUTER_COLS)
    is_grp = (lane >= N_EXPERTS) & (lane < N_EXPERTS + MOE_GROUPS)
    gl = jnp.where(is_grp, logits, NEG_INF)
    gmax = jnp.max(gl, axis=-1, keepdims=True)
    gidx = jnp.min(jnp.where(gl == gmax, lane - N_EXPERTS, big), axis=-1, keepdims=True)
    gate = 1.0 / jnp.sum(jnp.where(is_grp, jnp.exp(gl - gmax), 0.0), axis=-1, keepdims=True)
    in_grp = (lane < N_EXPERTS) & ((lane >> 2) == gidx)
    el = jnp.where(in_grp, logits, NEG_INF)
    t1 = jnp.max(el, axis=-1, keepdims=True)
    i1 = jnp.min(jnp.where(el == t1, lane, big), axis=-1, keepdims=True)
    el2 = jnp.where(lane == i1, NEG_INF, el)
    t2 = jnp.max(el2, axis=-1, keepdims=True)
    i2 = jnp.min(jnp.where(el2 == t2, lane, big), axis=-1, keepdims=True)
    e2 = jnp.exp(t2 - t1)
    return lane, i1, i2, gate / (1.0 + e2), gate * e2 / (1.0 + e2)


META_E1, META_E2, META_R1, META_R2, META_W1, META_W2 = range(6)


def _outrouter_kernel(h_ref, mix_ref, od_ref, g_ref, wr_ref, br_ref, wout_hbm, woutd_hbm,
                      h_out, m_ref, meta_ref, meta_t_ref, cnt_ref,
                      wo_b, wod_b, stage, sem, carry_ref, *, layer):
    @pl.when(pl.program_id(0) == 0)
    def _():
        carry_ref[...] = jnp.zeros_like(carry_ref)
        _stream_cast(wout_hbm, layer, wo_b, stage, sem)
        _stream_cast(woutd_hbm, layer, wod_b, stage, sem)

    acc = (jnp.dot(mix_ref[...], wo_b[...], preferred_element_type=F32)
           + jnp.dot(od_ref[...], wod_b[...], preferred_element_type=F32))
    h1 = h_ref[...] + acc
    h_out[...] = h1
    m = _rms_rows(h1, g_ref[...])
    m_ref[...] = m
    logits = _mm(m, wr_ref[...], False) + br_ref[...]
    rows = logits.shape[0]
    lane, i1, i2, w1, w2 = _top2_in_top_group(logits)
    sel = jnp.where((lane == i1) | (lane == i2), 1.0, 0.0)
    r = lax.broadcasted_iota(jnp.int32, (rows, rows), 0)
    c = lax.broadcasted_iota(jnp.int32, (rows, rows), 1)
    earlier = jnp.where(c < r, 1.0, 0.0).astype(BF16)
    rank = jnp.dot(earlier, sel.astype(BF16), preferred_element_type=F32) + carry_ref[...]
    r1 = jnp.sum(jnp.where(lane == i1, rank, 0.0), axis=-1, keepdims=True)
    r2 = jnp.sum(jnp.where(lane == i2, rank, 0.0), axis=-1, keepdims=True)
    carry_ref[...] += jnp.sum(sel, axis=0, keepdims=True)
    cnt_ref[...] = carry_ref[...]
    meta = jnp.zeros((rows, ROUTER_COLS), F32)
    for pos, val in ((META_E1, i1.astype(F32)), (META_E2, i2.astype(F32)), (META_R1, r1), (META_R2, r2),
                     (META_W1, w1), (META_W2, w2)):
        meta = jnp.where(lane == pos, val, meta)
    meta_ref[...] = meta
    meta_t_ref[...] = meta.T[:SUBLANE, :]


OUTROUTER_TM = 256


def _outrouter(h, mix, od, g_all, wr_all, br_all, w_out, w_out_d, layer):
    rows = h.shape[0]
    tm = OUTROUTER_TM
    row_spec = pl.BlockSpec((tm, D_MODEL), lambda i: (i, 0))
    return pl.pallas_call(
        functools.partial(_outrouter_kernel, layer=layer),
        grid=(rows // tm,),
        in_specs=[
            row_spec,
            pl.BlockSpec((tm, ABC_PAD), lambda i: (i, 0)),
            pl.BlockSpec((tm, OD_PAD), lambda i: (i, 0)),
            pl.BlockSpec((None, 1, D_MODEL), lambda i: (layer, 0, 0)),
            pl.BlockSpec((None, D_MODEL, ROUTER_COLS), lambda i: (layer, 0, 0)),
            pl.BlockSpec((None, 1, ROUTER_COLS), lambda i: (layer, 0, 0)),
            pl.BlockSpec(memory_space=pl.ANY),
            pl.BlockSpec(memory_space=pl.ANY),
        ],
        out_specs=[row_spec, row_spec,
                   pl.BlockSpec((tm, ROUTER_COLS), lambda i: (i, 0)),
                   pl.BlockSpec((SUBLANE, tm), lambda i: (0, i)),
                   pl.BlockSpec((1, ROUTER_COLS), lambda i: (0, 0))],
        out_shape=[jax.ShapeDtypeStruct((rows, D_MODEL), F32),
                   jax.ShapeDtypeStruct((rows, D_MODEL), F32),
                   jax.ShapeDtypeStruct((rows, ROUTER_COLS), F32),
                   jax.ShapeDtypeStruct((SUBLANE, rows), F32),
                   jax.ShapeDtypeStruct((1, ROUTER_COLS), F32)],
        scratch_shapes=[pltpu.VMEM((ABC_PAD, D_MODEL), BF16),
                        pltpu.VMEM((OD_PAD, D_MODEL), BF16),
                        pltpu.VMEM((2, D_MODEL, W_CHUNK), F32),
                        pltpu.SemaphoreType.DMA((2,)),
                        pltpu.VMEM((1, ROUTER_COLS), F32)],
        compiler_params=_cparams(("arbitrary",)),
        name="outrouter",
    )(h, mix, od, g_all, wr_all, br_all, w_out, w_out_d)


EXPERT_TILE = 256


def _experts_kernel(te_ref, tv_ref, s0_ref, s1_ref, nu_ref,
                    m_hbm, wg_ref, wu_ref, wd_ref, yk_hbm,
                    src_ref, xbuf, ybuf, wgb, wub, wdb, gsem, ssem, *, n_tok):
    i = pl.program_id(0)
    tile = EXPERT_TILE
    nused = nu_ref[0]
    slot = lax.rem(i, 2)

    def rows_moved(t):
        return pl.multiple_of(((tv_ref[t] + SUBLANE - 1) // SUBLANE) * SUBLANE, SUBLANE)

    def gather_copy(code, j, b):
        row = jnp.minimum(code >> 1, n_tok - 1)
        return pltpu.make_async_copy(m_hbm.at[pl.ds(row, 1), :], xbuf.at[b, pl.ds(j, 1), :], gsem.at[b])

    def scatter_copy(code, j, b):
        return pltpu.make_async_copy(ybuf.at[b, pl.ds(j, 1), :], yk_hbm.at[code & 1, pl.ds(code >> 1, 1), :],
                                     ssem.at[b])

    def start_rows(copy_fn, t, b):
        def body(j8, carry):
            for u in range(SUBLANE):
                j = j8 * SUBLANE + u
                copy_fn(src_ref[t * tile + j], j, b).start()
            return carry
        lax.fori_loop(0, rows_moved(t) // SUBLANE, body, 0)

    def wait_gather(t, b):
        n = rows_moved(t)
        pltpu.make_async_copy(m_hbm.at[pl.ds(0, n), :], xbuf.at[b, pl.ds(0, n), :], gsem.at[b]).wait()

    def wait_scatter(t, b):
        n = rows_moved(t)
        pltpu.make_async_copy(ybuf.at[b, pl.ds(0, n), :], yk_hbm.at[0, pl.ds(0, n), :], ssem.at[b]).wait()

    @pl.when(i == 0)
    def _():
        def fill(t, carry):
            src_ref[s0_ref[t]] = 2 * t
            src_ref[s1_ref[t]] = 2 * t + 1
            return carry
        lax.fori_loop(0, n_tok, fill, 0)

        def fill_pad(t, carry):
            def one(j, c):
                src_ref[t * tile + j] = 2 * (n_tok + SUBLANE * lax.rem(t, 2) + lax.rem(j, SUBLANE))
                return c
            lax.fori_loop(tv_ref[t], rows_moved(t), one, 0)
            return carry
        lax.fori_loop(0, nused, fill_pad, 0)
        xbuf[...] = jnp.zeros_like(xbuf)
        for k in range(2):
            spare = pltpu.make_async_copy(xbuf.at[0, pl.ds(0, 2 * SUBLANE), :],
                                          yk_hbm.at[k, pl.ds(n_tok, 2 * SUBLANE), :], ssem.at[0])
            spare.start()
            spare.wait()
        start_rows(gather_copy, 0, 0)

    @pl.when(i < nused)
    def _():
        @pl.when(i + 1 < nused)
        def _():
            start_rows(gather_copy, i + 1, 1 - slot)

        changed = jnp.logical_or(i == 0, te_ref[i] != te_ref[jnp.maximum(i - 1, 0)])

        @pl.when(changed)
        def _():
            wgb[...] = wg_ref[...].astype(BF16)
            wub[...] = wu_ref[...].astype(BF16)
            wdb[...] = wd_ref[...].astype(BF16)

        wait_gather(i, slot)

        @pl.when(i >= 2)
        def _():
            wait_scatter(i - 2, slot)

        x = xbuf[slot].astype(BF16)
        gate = jnp.dot(x, wgb[...], preferred_element_type=F32)
        up = jnp.dot(x, wub[...], preferred_element_type=F32)
        ybuf[slot] = jnp.dot((_silu(gate) * up).astype(BF16), wdb[...], preferred_element_type=F32)
        start_rows(scatter_copy, i, slot)

        @pl.when(i == nused - 1)
        def _():
            @pl.when(i >= 1)
            def _():
                wait_scatter(i - 1, 1 - slot)
            wait_scatter(i, slot)


def _experts_sparse(m, tile_expert, tile_rows, slot0, slot1, nused, wg_all, wu_all, wd_all, layer):
    n_tok = m.shape[0]
    n_tiles = tile_expert.shape[0]

    def wspec(shape):
        return pl.BlockSpec((None, None) + shape, lambda i, te, tv, s0, s1, nu: (layer, te[i], 0, 0))

    grid_spec = pltpu.PrefetchScalarGridSpec(
        num_scalar_prefetch=5,
        grid=(n_tiles,),
        in_specs=[pl.BlockSpec(memory_space=pl.ANY),
                  wspec((D_MODEL, EXPERT_FF)), wspec((D_MODEL, EXPERT_FF)), wspec((EXPERT_FF, D_MODEL))],
        out_specs=pl.BlockSpec(memory_space=pl.ANY),
        scratch_shapes=[
            pltpu.SMEM((n_tiles * EXPERT_TILE,), jnp.int32),
            pltpu.VMEM((2, EXPERT_TILE, D_MODEL), F32),
            pltpu.VMEM((2, EXPERT_TILE, D_MODEL), F32),
            pltpu.VMEM((D_MODEL, EXPERT_FF), BF16),
            pltpu.VMEM((D_MODEL, EXPERT_FF), BF16),
            pltpu.VMEM((EXPERT_FF, D_MODEL), BF16),
            pltpu.SemaphoreType.DMA((2,)),
            pltpu.SemaphoreType.DMA((2,)),
        ],
    )
    return pl.pallas_call(
        functools.partial(_experts_kernel, n_tok=n_tok),
        grid_spec=grid_spec,
        out_shape=jax.ShapeDtypeStruct((2, n_tok + 2 * SUBLANE, D_MODEL), F32),
        compiler_params=_cparams(("arbitrary",)),
        name="experts_sparse",
    )(tile_expert, tile_rows, slot0, slot1, nused, m, wg_all, wu_all, wd_all)


def _expert_plan(meta_t, cnt, n_tiles):
    tile = EXPERT_TILE
    counts = cnt[0, :N_EXPERTS].astype(jnp.int32)
    padded = ((counts + tile - 1) // tile) * tile
    ends = jnp.cumsum(padded)
    base = ends - padded
    e1 = meta_t[META_E1].astype(jnp.int32)
    e2 = meta_t[META_E2].astype(jnp.int32)
    slot0 = base[e1] + meta_t[META_R1].astype(jnp.int32)
    slot1 = base[e2] + meta_t[META_R2].astype(jnp.int32)
    start = jnp.arange(n_tiles, dtype=jnp.int32) * tile
    expert_of = jnp.sum((start[:, None] >= ends[None, :]).astype(jnp.int32), axis=1)
    last_used = jnp.max(jnp.where(counts > 0, jnp.arange(N_EXPERTS, dtype=jnp.int32), 0))
    tile_expert = jnp.minimum(expert_of, last_used)
    e_clamped = jnp.minimum(expert_of, N_EXPERTS - 1)
    tile_rows = jnp.clip(counts[e_clamped] - (start - base[e_clamped]), 0, tile)
    tile_rows = jnp.where(expert_of < N_EXPERTS, tile_rows, 0).astype(jnp.int32)
    nused = (ends[-1] // tile).astype(jnp.int32).reshape(1)
    return tile_expert, tile_rows, slot0, slot1, nused


def _router_kernel(h_ref, g_ref, wr_ref, br_ref, m_ref, comb_ref, *, hp):
    m = _rms_rows(h_ref[...], g_ref[...]).astype(m_ref.dtype)
    m_ref[...] = m
    logits = _mm(m, wr_ref[...], hp) + br_ref[...]
    rows = logits.shape[0]
    lane = lax.broadcasted_iota(jnp.int32, (rows, ROUTER_COLS), 1)
    big = jnp.int32(ROUTER_COLS)
    is_grp = (lane >= N_EXPERTS) & (lane < N_EXPERTS + MOE_GROUPS)
    gl = jnp.where(is_grp, logits, NEG_INF)
    gmax = jnp.max(gl, axis=-1, keepdims=True)
    gidx = jnp.min(jnp.where(gl == gmax, lane - N_EXPERTS, big), axis=-1, keepdims=True)
    gate = 1.0 / jnp.sum(jnp.where(is_grp, jnp.exp(gl - gmax), 0.0), axis=-1, keepdims=True)
    in_grp = (lane < N_EXPERTS) & ((lane >> 2) == gidx)
    el = jnp.where(in_grp, logits, NEG_INF)
    t1 = jnp.max(el, axis=-1, keepdims=True)
    i1 = jnp.min(jnp.where(el == t1, lane, big), axis=-1, keepdims=True)
    el2 = jnp.where(lane == i1, NEG_INF, el)
    t2 = jnp.max(el2, axis=-1, keepdims=True)
    i2 = jnp.min(jnp.where(el2 == t2, lane, big), axis=-1, keepdims=True)
    e2 = jnp.exp(t2 - t1)
    w1 = gate / (1.0 + e2)
    w2 = gate * e2 / (1.0 + e2)
    comb_ref[...] = jnp.where(lane == i1, w1, 0.0) + jnp.where(lane == i2, w2, 0.0)


def _router(h, g_all, wr_all, br_all, layer, tm, hp):
    rows = h.shape[0]
    return pl.pallas_call(
        functools.partial(_router_kernel, hp=hp),
        grid=(rows // tm,),
        in_specs=[
            pl.BlockSpec((tm, D_MODEL), lambda i: (i, 0)),
            pl.BlockSpec((None, 1, D_MODEL), lambda i: (layer, 0, 0)),
            pl.BlockSpec((None, D_MODEL, ROUTER_COLS), lambda i: (layer, 0, 0)),
            pl.BlockSpec((None, 1, ROUTER_COLS), lambda i: (layer, 0, 0)),
        ],
        out_specs=[pl.BlockSpec((tm, D_MODEL), lambda i: (i, 0)),
                   pl.BlockSpec((tm, ROUTER_COLS), lambda i: (i, 0))],
        out_shape=[jax.ShapeDtypeStruct((rows, D_MODEL), _act_dtype(hp)),
                   jax.ShapeDtypeStruct((rows, ROUTER_COLS), F32)],
        compiler_params=_cparams(("arbitrary",)),
        name="router",
    )(h, g_all, wr_all, br_all)


def _moe_kernel(h_ref, m_ref, comb_ref, wg_ref, wu_ref, wd_ref, o_ref, *, hp):
    e = pl.program_id(1)

    @pl.when(e == 0)
    def _():
        o_ref[...] = h_ref[...]

    x = m_ref[...]
    gate = _mm(x, wg_ref[...], hp)
    up = _mm(x, wu_ref[...], hp)
    lane = lax.broadcasted_iota(jnp.int32, comb_ref.shape, 1)
    w = jnp.sum(jnp.where(lane == e, comb_ref[...], 0.0), axis=-1, keepdims=True)
    o_ref[...] += _mm(_silu(gate) * up * w, wd_ref[...], hp)


def _moe_dense(h, m, comb, wg_all, wu_all, wd_all, layer, tm, hp):
    rows = h.shape[0]
    return pl.pallas_call(
        functools.partial(_moe_kernel, hp=hp),
        grid=(rows // tm, N_EXPERTS),
        in_specs=[
            pl.BlockSpec((tm, D_MODEL), lambda i, e: (i, 0)),
            pl.BlockSpec((tm, D_MODEL), lambda i, e: (i, 0)),
            pl.BlockSpec((tm, ROUTER_COLS), lambda i, e: (i, 0)),
            pl.BlockSpec((None, None, D_MODEL, EXPERT_FF), lambda i, e: (layer, e, 0, 0)),
            pl.BlockSpec((None, None, D_MODEL, EXPERT_FF), lambda i, e: (layer, e, 0, 0)),
            pl.BlockSpec((None, None, EXPERT_FF, D_MODEL), lambda i, e: (layer, e, 0, 0)),
        ],
        out_specs=pl.BlockSpec((tm, D_MODEL), lambda i, e: (i, 0)),
        out_shape=jax.ShapeDtypeStruct((rows, D_MODEL), F32),
        compiler_params=_cparams(("arbitrary", "arbitrary")),
        name="moe_dense",
    )(h, m, comb, wg_all, wu_all, wd_all)


def _ple_kernel(h_ref, hc_ref, g_ref, p_ref, wg_ref, wp_ref, o_ref, xn_ref, *, hp):
    @pl.when(pl.program_id(1) == 0)
    def _():
        xn_ref[...] = _rms_rows(h_ref[...], g_ref[...]).astype(xn_ref.dtype)

    gate = _sigmoid(_mm(xn_ref[...], wg_ref[...], hp))
    o_ref[...] = hc_ref[...] + gate * _mm(p_ref[...], wp_ref[...], hp)


def _ple(h, p_all, g_all, wg_all, wp_all, layer, tm, tn, hp):
    rows = h.shape[0]
    return pl.pallas_call(
        functools.partial(_ple_kernel, hp=hp),
        grid=(rows // tm, D_MODEL // tn),
        in_specs=[
            pl.BlockSpec((tm, D_MODEL), lambda i, j: (i, 0)),
            pl.BlockSpec((tm, tn), lambda i, j: (i, j)),
            pl.BlockSpec((None, 1, D_MODEL), lambda i, j: (layer, 0, 0)),
            pl.BlockSpec((None, tm, PLE_DIM), lambda i, j: (layer, i, 0)),
            pl.BlockSpec((None, D_MODEL, tn), lambda i, j: (layer, 0, j)),
            pl.BlockSpec((None, PLE_DIM, tn), lambda i, j: (layer, 0, j)),
        ],
        out_specs=pl.BlockSpec((tm, tn), lambda i, j: (i, j)),
        out_shape=jax.ShapeDtypeStruct((rows, D_MODEL), F32),
        scratch_shapes=[pltpu.VMEM((tm, D_MODEL), _act_dtype(hp))],
        compiler_params=_cparams(("arbitrary", "arbitrary")),
        name="ple",
    )(h, h, g_all, p_all, wg_all, wp_all)


def _token_tail_hp(h, mix, od, p_all, layer, tw):
    (w_out, w_out_d, g_ffn, w_router, b_router, w_gate, w_up, w_down, g_ple, w_ple_gate, w_ple_proj) = tw
    tm = h.shape[0]
    h = _outproj(h, mix, od, w_out, w_out_d, layer, tm, 512, True)
    m, comb = _router(h, g_ffn, w_router, b_router, layer, tm, True)
    h = _moe_dense(h, m, comb, w_gate, w_up, w_down, layer, tm, True)
    return _ple(h, p_all, g_ple, w_ple_gate, w_ple_proj, layer, tm, 512, True)


POST_TM = 256
W_CHUNK = 128
Z_PARTS = 2
Z_CHUNK = 1280
POST_VMEM_LIMIT = 60 * 1024 * 1024


def _stream_cast(w_hbm, layer, dst, stage, sem, col0=0):
    k_rows = dst.shape[0]
    n_chunks = dst.shape[1] // W_CHUNK

    def chunk_copy(c):
        return pltpu.make_async_copy(w_hbm.at[layer, pl.ds(0, k_rows), pl.ds(col0 + c * W_CHUNK, W_CHUNK)],
                                     stage.at[c % 2, pl.ds(0, k_rows), :], sem.at[c % 2])

    chunk_copy(0).start()
    for c in range(n_chunks):
        if c + 1 < n_chunks:
            chunk_copy(c + 1).start()
        chunk_copy(c).wait()
        dst[:, c * W_CHUNK:(c + 1) * W_CHUNK] = stage[c % 2, 0:k_rows, :].astype(BF16)


def _post_kernel(*refs, layer, with_inproj):
    if with_inproj:
        (h_ref, y0_ref, y1_ref, meta_ref, p_ref, gp_ref, wpp_ref, wpg_hbm, gm_ref, win_hbm,
         h_out, z_out, wpg_b, stage, sem, win_b, a_scr) = refs
    else:
        (h_ref, y0_ref, y1_ref, meta_ref, p_ref, gp_ref, wpp_ref, wpg_hbm,
         h_out, wpg_b, stage, sem) = refs
    part = pl.program_id(1)
    part_cols = IN_COLS // Z_PARTS

    @pl.when(jnp.logical_and(pl.program_id(0) == 0, part == 0))
    def _():
        _stream_cast(wpg_hbm, layer, wpg_b, stage, sem)
        if with_inproj:
            for k in range(Z_PARTS):
                _stream_cast(win_hbm, layer + 1, win_b.at[k], stage, sem, col0=k * part_cols)

    @pl.when(part == 0)
    def _():
        meta = meta_ref[...]
        lane = lax.broadcasted_iota(jnp.int32, meta.shape, 1)
        w1 = jnp.sum(jnp.where(lane == META_W1, meta, 0.0), axis=-1, keepdims=True)
        w2 = jnp.sum(jnp.where(lane == META_W2, meta, 0.0), axis=-1, keepdims=True)
        hn = h_ref[...] + w1 * y0_ref[...] + w2 * y1_ref[...]
        xn = _rms_rows(hn, gp_ref[...]).astype(BF16)
        gate = _sigmoid(jnp.dot(xn, wpg_b[...], preferred_element_type=F32))
        h3 = hn + gate * _mm(p_ref[...], wpp_ref[...], False)
        h_out[...] = h3
        if with_inproj:
            a_scr[...] = _rms_rows(h3, gm_ref[...]).astype(BF16)

    if with_inproj:
        for c0 in range(0, part_cols, Z_CHUNK):
            z_out[:, c0:c0 + Z_CHUNK] = jnp.dot(a_scr[...], win_b[part, :, c0:c0 + Z_CHUNK],
                                                preferred_element_type=F32)


def _post(h, yk, meta, p_all, g_ple, w_ple_gate, w_ple_proj, g_mix, w_in, layer):
    rows = h.shape[0]
    tm = POST_TM
    with_inproj = layer + 1 < DEPTH
    row_spec = pl.BlockSpec((tm, D_MODEL), lambda i, j: (i, 0))
    in_specs = [
        row_spec,
        pl.BlockSpec((None, tm, D_MODEL), lambda i, j: (0, i, 0)),
        pl.BlockSpec((None, tm, D_MODEL), lambda i, j: (1, i, 0)),
        pl.BlockSpec((tm, ROUTER_COLS), lambda i, j: (i, 0)),
        pl.BlockSpec((None, tm, PLE_DIM), lambda i, j: (layer, i, 0)),
        pl.BlockSpec((None, 1, D_MODEL), lambda i, j: (layer, 0, 0)),
        pl.BlockSpec((None, PLE_DIM, D_MODEL), lambda i, j: (layer, 0, 0)),
        pl.BlockSpec(memory_space=pl.ANY),
    ]
    args = [h, yk, yk, meta, p_all, g_ple, w_ple_proj, w_ple_gate]
    out_specs = [row_spec]
    out_shape = [jax.ShapeDtypeStruct((rows, D_MODEL), F32)]
    scratch = [pltpu.VMEM((D_MODEL, D_MODEL), BF16),
               pltpu.VMEM((2, D_MODEL, W_CHUNK), F32),
               pltpu.SemaphoreType.DMA((2,))]
    if with_inproj:
        part_cols = IN_COLS // Z_PARTS
        in_specs += [pl.BlockSpec((None, 1, D_MODEL), lambda i, j: (layer + 1, 0, 0)),
                     pl.BlockSpec(memory_space=pl.ANY)]
        args += [g_mix, w_in]
        out_specs.append(pl.BlockSpec((tm, part_cols), lambda i, j: (i, j)))
        out_shape.append(jax.ShapeDtypeStruct((rows, IN_COLS), F32))
        scratch += [pltpu.VMEM((Z_PARTS, D_MODEL, part_cols), BF16), pltpu.VMEM((tm, D_MODEL), BF16)]
    return pl.pallas_call(
        functools.partial(_post_kernel, layer=layer, with_inproj=with_inproj),
        grid=(rows // tm, Z_PARTS if with_inproj else 1),
        in_specs=in_specs,
        out_specs=out_specs,
        out_shape=out_shape,
        scratch_shapes=scratch,
        compiler_params=pltpu.CompilerParams(dimension_semantics=("arbitrary", "arbitrary"),
                                             vmem_limit_bytes=POST_VMEM_LIMIT),
        name="post",
    )(*args)


def _token_tail_prompt(h, mix, od, p_all, layer, tw, g_mix, w_in):
    (w_out, w_out_d, g_ffn, w_router, b_router, w_gate, w_up, w_down, g_ple, w_ple_gate, w_ple_proj) = tw
    rows = h.shape[0]
    n_tiles = (2 * rows + N_EXPERTS * (EXPERT_TILE - 1) + EXPERT_TILE - 1) // EXPERT_TILE
    h, m, meta, meta_t, cnt = _outrouter(h, mix, od, g_ffn, w_router, b_router, w_out, w_out_d, layer)
    plan = _expert_plan(meta_t, cnt, n_tiles)
    yk = _experts_sparse(m, *plan, w_gate, w_up, w_down, layer)
    out = _post(h, yk, meta, p_all, g_ple, w_ple_gate, w_ple_proj, g_mix, w_in, layer)
    return (out[0], out[1]) if layer + 1 < DEPTH else (out[0], None)


TAIL = 8


def _patch_kernel(x_ref, tail_ref, o_ref):
    del x_ref
    o_ref[...] = tail_ref[...]


def _patch_tail(x, tail):
    n, s, width = x.shape
    return pl.pallas_call(
        _patch_kernel,
        grid=(n,),
        in_specs=[pl.BlockSpec(memory_space=pl.ANY),
                  pl.BlockSpec((1, TAIL, width), lambda b: (b, 0, 0))],
        out_specs=pl.BlockSpec((1, TAIL, width), lambda b: (b, s // TAIL - 1, 0)),
        out_shape=jax.ShapeDtypeStruct(x.shape, x.dtype),
        input_output_aliases={0: 0},
        compiler_params=_cparams(("arbitrary",)),
        name="patch_tail",
    )(x, tail)


def kernel(x_prompt, x_sample, p_prompt, p_sample, state_conv_a, state_conv_c, cache_kv_w128, cache_kv_w512, cache_kv_w2048, g_mix, w_in, conv_a_w, conv_a_b, ln_a_g, ln_a_b, ln_b_g, ln_b_b, sgu_w, sgu_b, conv_c_w, g_q, g_k, w_out, g_ffn, w_router_grp, b_router_grp, w_router_exp, b_router_exp, w_gate, w_up, w_down, g_ple, w_ple_gate, w_ple_proj):
    n_p, s_p, _ = x_prompt.shape
    n_s, t_s, _ = x_sample.shape
    rows_p = n_p * s_p
    rows_s = n_s * t_s

    def row3(a):
        return a.reshape(DEPTH, 1, a.shape[-1])

    g_mix3, g_ffn3, g_ple3 = row3(g_mix), row3(g_ffn), row3(g_ple)
    cab3, lag3, lab3, lbg3, lbb3 = row3(conv_a_b), row3(ln_a_g), row3(ln_a_b), row3(ln_b_g), row3(ln_b_b)
    gq3, gk3 = row3(g_q), row3(g_k)
    sb_rep = jnp.repeat(jnp.swapaxes(sgu_b, 1, 2), HEAD_DIM, axis=2)
    sw8 = jnp.repeat(jnp.transpose(sgu_w[:, :, :t_s, :t_s], (0, 3, 2, 1)), HEAD_DIM, axis=3)
    w_out_d = jnp.pad(w_out[:, ABC_PAD - 64:].reshape(DEPTH, 3, GROUP_WIDTH, D_MODEL),
                      ((0, 0), (0, 0), (0, OD_PAD // 3 - GROUP_WIDTH), (0, 0))).reshape(DEPTH, OD_PAD, D_MODEL)
    w_router = jnp.concatenate(
        [jnp.transpose(w_router_exp, (0, 2, 1, 3)).reshape(DEPTH, D_MODEL, N_EXPERTS), w_router_grp,
         jnp.zeros((DEPTH, D_MODEL, ROUTER_COLS - N_EXPERTS - MOE_GROUPS), F32)], axis=2)
    b_router = jnp.concatenate(
        [b_router_exp.reshape(DEPTH, N_EXPERTS), b_router_grp,
         jnp.zeros((DEPTH, ROUTER_COLS - N_EXPERTS - MOE_GROUPS), F32)], axis=1).reshape(DEPTH, 1, ROUTER_COLS)
    tail_w = (w_out, w_out_d, g_ffn3, w_router, b_router, w_gate, w_up, w_down, g_ple3, w_ple_gate, w_ple_proj)
    mix_w_p = (conv_a_w, cab3, lag3, lab3, lbg3, lbb3, sgu_w, sb_rep, conv_c_w, gq3, gk3)
    mix_w_s = (conv_a_w, cab3, lag3, lab3, lbg3, lbb3, sw8, sb_rep, conv_c_w, gq3, gk3)

    caches = [c.reshape(c.shape[0], c.shape[1], c.shape[2], 2 * GROUP_WIDTH)
              for c in (cache_kv_w128, cache_kv_w512, cache_kv_w2048)]
    p_p = p_prompt.reshape(DEPTH, rows_p, PLE_DIM)
    rows_t = n_p * TAIL
    rows_h = rows_s + rows_t
    p_h = jnp.concatenate([p_sample.reshape(DEPTH, rows_s, PLE_DIM),
                           p_prompt[:, :, s_p - TAIL:].reshape(DEPTH, rows_t, PLE_DIM)], axis=1)
    h_h = jnp.concatenate([x_sample.reshape(rows_s, D_MODEL),
                           x_prompt[:, s_p - TAIL:].reshape(rows_t, D_MODEL)], axis=0)

    h = x_prompt.reshape(rows_p, D_MODEL)
    st_a, st_c, st_kv = [], [], [[], [], []]
    sa, sc, sv, skv = [], [], [], [[], [], []]
    z_next = _inproj(h, g_mix3, w_in, 0, 1024, 512)
    for i in range(DEPTH):
        z_h = _inproj(h_h, g_mix3, w_in, i, rows_h, 512, hp=True)
        z = _patch_tail(z_next.reshape(n_p, s_p, IN_COLS), z_h[rows_s:].reshape(n_p, TAIL, IN_COLS))
        (mix, q1, kv1, q4, kv4, q16, kv16, sta, stc, s1, s4, s16) = _mixers_prompt(z, i, mix_w_p)
        o1, l1 = _attn_prompt(q1, kv1)
        o4, l4 = _attn_prompt(q4, kv4)
        o16, l16 = _attn_prompt(q16, kv16)
        od = _combine_prompt(o1, l1, o4, l4, o16, l16)
        (mix_s, od_s, na, nc, cv, n1, n4, n16) = _mixers_sample(
            z_h[:rows_s].reshape(n_s, t_s, IN_COLS), i, state_conv_a, state_conv_c, caches, mix_w_s)
        mix_h = jnp.concatenate([mix_s.reshape(rows_s, ABC_PAD),
                                 mix[:, s_p - TAIL:].reshape(rows_t, ABC_PAD).astype(F32)], axis=0)
        od_h = jnp.concatenate([od_s.reshape(rows_s, OD_PAD),
                                od[:, s_p - TAIL:].reshape(rows_t, OD_PAD).astype(F32)], axis=0)
        h, z_next = _token_tail_prompt(h, mix.reshape(rows_p, ABC_PAD), od.reshape(rows_p, OD_PAD), p_p, i,
                                       tail_w, g_mix3, w_in)
        h_h = _token_tail_hp(h_h, mix_h, od_h, p_h, i, tail_w)
        st_a.append(sta[:, A_HALO - (A_CONV_LEN - 1):])
        st_c.append(stc[:, C_HALO - (C_CONV_LEN - 1):])
        for g, s_kv in enumerate((s1, s4, s16)):
            st_kv[g].append(s_kv.reshape(n_p, s_kv.shape[1], 2, HEADS_PER_GROUP, HEAD_DIM))
        sa.append(na)
        sc.append(nc)
        sv.append(cv)
        for g, nk in enumerate((n1, n4, n16)):
            skv[g].append(nk.reshape(n_s, nk.shape[1], 2, HEADS_PER_GROUP, HEAD_DIM))
    y_prompt = _patch_tail(h.reshape(n_p, s_p, D_MODEL), h_h[rows_s:].reshape(n_p, TAIL, D_MODEL))
    conv_a_prompt = jnp.stack(st_a)
    conv_c_prompt = jnp.stack(st_c)
    kv_prompt = [jnp.stack(s) for s in st_kv]
    y_sample = h_h[:rows_s].reshape(n_s, t_s, D_MODEL)
    conv_a_sample = jnp.stack(sa)
    conv_c_sample = jnp.stack(sc)
    chunk_v_sample = jnp.stack(sv)
    kv_sample = [jnp.stack(s) for s in skv]

    return (y_prompt, y_sample, conv_a_prompt, conv_a_sample, conv_c_prompt, conv_c_sample, chunk_v_sample,
            kv_prompt[0], kv_sample[0], kv_prompt[1], kv_sample[1], kv_prompt[2], kv_sample[2])
```

```python
import functools

import jax
import jax.numpy as jnp
from jax import lax
from jax.experimental import pallas as pl
from jax.experimental.pallas import tpu as pltpu

F32 = jnp.float32
BF16 = jnp.bfloat16

D_MODEL = 2048
DEPTH = 4
PLE_DIM = 256
HEAD_DIM = 64
A_WIDTH = 512
A_CONV_LEN = 31
B_WIDTH = 512
B_HEADS = 8
CHUNK = 128
C_WIDTH = 448
C_CONV_LEN = 3
ATTN_GROUPS = ((128, 1), (512, 4), (2048, 16))
HEADS_PER_GROUP = 3
WIN_KEYS = 128
D_HEADS = 9
D_WIDTH = D_HEADS * HEAD_DIM
GROUP_WIDTH = HEADS_PER_GROUP * HEAD_DIM
IN_COLS = 2 * A_WIDTH + 2 * B_WIDTH + 3 * C_WIDTH + 3 * D_WIDTH
COL_B = 2 * A_WIDTH
COL_C = COL_B + 2 * B_WIDTH
COL_D = COL_C + 3 * C_WIDTH
ATTN_SCALE = HEAD_DIM ** -0.5
MOE_GROUPS = 4
EXPERTS_PER_GROUP = 4
N_EXPERTS = 16
EXPERT_FF = 512
RMS_EPS = 1e-6
LN_EPS = 1e-5

LANE = 128
SUBLANE = 8
ABC_PAD = 1536
OD_PAD = 3 * 256
ROUTER_COLS = 128
VMEM_LIMIT = 56 * 1024 * 1024
NEG_INF = float("-inf")


def _cparams(sem):
    return pltpu.CompilerParams(dimension_semantics=sem, vmem_limit_bytes=VMEM_LIMIT)


def _rms_rows(x, g):
    return x * lax.rsqrt(jnp.mean(x * x, axis=-1, keepdims=True) + RMS_EPS) * g


def _layernorm_rows(x, g, b):
    mu = jnp.mean(x, axis=-1, keepdims=True)
    xc = x - mu
    var = jnp.mean(xc * xc, axis=-1, keepdims=True)
    return xc * lax.rsqrt(var + LN_EPS) * g + b


def _sigmoid(x):
    return 1.0 / (1.0 + jnp.exp(-x))


def _silu(x):
    return x * _sigmoid(x)


def _gelu(x):
    return 0.5 * x * (1.0 + lax.erf(x * (2.0 ** -0.5)))


def _split_bf16(x):
    hi = x.astype(BF16)
    return hi, (x - hi.astype(F32)).astype(BF16)


def _mm(x, w, hp):
    if not hp:
        return jnp.dot(x.astype(BF16), w.astype(BF16), preferred_element_type=F32)
    rows = x.shape[0]
    xh, xl = _split_bf16(x)
    wh, wl = _split_bf16(w)
    r = jnp.dot(jnp.concatenate([xh, xl], axis=0), wh, preferred_element_type=F32)
    return r[:rows] + r[rows:] + jnp.dot(xh, wl, preferred_element_type=F32)


def _act_dtype(hp):
    return F32 if hp else BF16


def _head_norm(x, g):
    outs = []
    for h in range(D_HEADS):
        xh = x[:, HEAD_DIM * h:HEAD_DIM * (h + 1)]
        outs.append(_rms_rows(xh, g))
    return outs


def _split_qkv(z_ref, rows):
    q_lo = (COL_D // LANE) * LANE
    zq = z_ref[0, rows, q_lo:q_lo + 640]
    q = zq[:, COL_D - q_lo:COL_D - q_lo + D_WIDTH]
    k_lo = COL_D + D_WIDTH
    zk = z_ref[0, rows, k_lo:k_lo + 640]
    k = zk[:, :D_WIDTH]
    v_lo = ((COL_D + 2 * D_WIDTH) // LANE) * LANE
    zv = z_ref[0, rows, v_lo:v_lo + 640]
    v = zv[:, COL_D + 2 * D_WIDTH - v_lo:]
    return q, k, v


def _split_c(z_ref, rows):
    zc = z_ref[0, rows, COL_C:COL_C + 1408]
    return zc[:, 0:C_WIDTH], zc[:, C_WIDTH:2 * C_WIDTH], zc[:, 2 * C_WIDTH:3 * C_WIDTH]


def _inproj_kernel(x_ref, g_ref, w_ref, o_ref, xn_ref, *, hp):
    @pl.when(pl.program_id(1) == 0)
    def _():
        xn_ref[...] = _rms_rows(x_ref[...], g_ref[...]).astype(xn_ref.dtype)

    o_ref[...] = _mm(xn_ref[...], w_ref[...], hp)


def _inproj(h, g_all, w_all, layer, tm, tn, hp=False):
    rows = h.shape[0]
    return pl.pallas_call(
        functools.partial(_inproj_kernel, hp=hp),
        grid=(rows // tm, IN_COLS // tn),
        in_specs=[
            pl.BlockSpec((tm, D_MODEL), lambda i, j: (i, 0)),
            pl.BlockSpec((None, 1, D_MODEL), lambda i, j: (layer, 0, 0)),
            pl.BlockSpec((None, D_MODEL, tn), lambda i, j: (layer, 0, j)),
        ],
        out_specs=pl.BlockSpec((tm, tn), lambda i, j: (i, j)),
        out_shape=jax.ShapeDtypeStruct((rows, IN_COLS), F32),
        scratch_shapes=[pltpu.VMEM((tm, D_MODEL), _act_dtype(hp))],
        compiler_params=_cparams(("arbitrary", "arbitrary")),
        name="inproj",
    )(h, g_all, w_all)


MIX_TT = 256
CONV_ROWS = 64
A_HALO = 32
C_HALO = 8


def _pair_weights(sw_ref, wp_ref):
    row = lax.broadcasted_iota(jnp.int32, (CHUNK, CHUNK), 0)
    col = lax.broadcasted_iota(jnp.int32, (CHUNK, CHUNK), 1)
    keep = col <= row
    for p in range(B_HEADS // 2):
        w0 = jnp.where(keep, sw_ref[2 * p], 0.0)
        w1 = jnp.where(keep, sw_ref[2 * p + 1], 0.0)
        wp_ref[p] = jnp.concatenate([w0, w1], axis=1).astype(BF16)


def _mixer_kernel(z_ref, caw_ref, cab_ref, lag_ref, lab_ref, lbg_ref, lbb_ref, sw_ref, sb_ref, ccw_ref,
                  gq_ref, gk_ref,
                  mix_ref, q1_ref, kv1_ref, q4_ref, kv4_ref, q16_ref, kv16_ref,
                  sta_ref, stc_ref, st1_ref, st4_ref, st16_ref,
                  abuf, cbuf, wp_ref, qs_ref, kvs_ref, ashift):
    t = pl.program_id(1)
    tt = MIX_TT

    @pl.when(t == 0)
    def _():
        abuf[0:A_HALO, :] = jnp.zeros((A_HALO, A_WIDTH), F32)
        cbuf[0:C_HALO, :] = jnp.zeros((C_HALO, C_WIDTH), F32)
        _pair_weights(sw_ref, wp_ref)

    @pl.when(t > 0)
    def _():
        abuf[0:A_HALO, :] = abuf[tt:tt + A_HALO, :]
        cbuf[0:C_HALO, :] = cbuf[tt:tt + C_HALO, :]

    za = z_ref[0, :, 0:2 * A_WIDTH]
    abuf[A_HALO:A_HALO + tt, :] = za[:, :A_WIDTH] * _sigmoid(za[:, A_WIDTH:])
    sta_ref[0] = abuf[tt:tt + A_HALO, :]
    base = A_HALO - (A_CONV_LEN - 1)
    span = tt + A_HALO - SUBLANE
    for s in range(1, SUBLANE):
        ashift[s - 1, 0:span, :] = abuf[s:s + span, :]

    def tap_rows(first):
        phase = first % SUBLANE
        src = abuf if phase == 0 else ashift.at[phase - 1]
        return src[first - phase:first - phase + CONV_ROWS, :]

    for r0 in range(0, tt, CONV_ROWS):
        acc = caw_ref[0:1, :] * tap_rows(r0 + base)
        for j in range(1, A_CONV_LEN):
            acc = acc + caw_ref[j:j + 1, :] * tap_rows(r0 + base + j)
        y = _layernorm_rows(acc + cab_ref[...], lag_ref[...], lab_ref[...])
        mix_ref[0, r0:r0 + CONV_ROWS, 0:A_WIDTH] = _silu(y).astype(mix_ref.dtype)

    lane = lax.broadcasted_iota(jnp.int32, (CHUNK, LANE), 1)
    for c0 in range(0, tt, CHUNK):
        gb = _gelu(z_ref[0, c0:c0 + CHUNK, COL_B:COL_B + 2 * B_WIDTH])
        u = gb[:, :B_WIDTH]
        v = _layernorm_rows(gb[:, B_WIDTH:], lbg_ref[...], lbb_ref[...])
        pieces = []
        for p in range(B_HEADS // 2):
            v128 = v[:, LANE * p:LANE * (p + 1)]
            rhs = jnp.concatenate([jnp.where(lane < HEAD_DIM, v128, 0.0),
                                   jnp.where(lane >= HEAD_DIM, v128, 0.0)], axis=0).astype(BF16)
            pieces.append(jnp.dot(wp_ref[p], rhs, preferred_element_type=F32))
        mixed = jnp.concatenate(pieces, axis=1) + sb_ref[...]
        mix_ref[0, c0:c0 + CHUNK, A_WIDTH:A_WIDTH + B_WIDTH] = (u * mixed).astype(mix_ref.dtype)

    g_b, g_c, x_c = _split_c(z_ref, slice(None))
    cbuf[C_HALO:C_HALO + tt, :] = g_c * x_c
    stc_ref[0] = cbuf[tt:tt + C_HALO, :]
    cbase = C_HALO - (C_CONV_LEN - 1)
    conv = ccw_ref[0:1, :] * cbuf[cbase:cbase + tt, :]
    for j in range(1, C_CONV_LEN):
        conv = conv + ccw_ref[j:j + 1, :] * cbuf[cbase + j:cbase + j + tt, :]
    o_c = jnp.concatenate([g_b * conv, jnp.zeros((tt, ABC_PAD - 2 * A_WIDTH - C_WIDTH), F32)], axis=1)
    mix_ref[0, :, 2 * A_WIDTH:ABC_PAD] = o_c.astype(mix_ref.dtype)

    q, k, v = _split_qkv(z_ref, slice(None))
    qn = _head_norm(q, gq_ref[...] * ATTN_SCALE)
    kn = _head_norm(k, gk_ref[...])
    outs = ((q1_ref, kv1_ref, st1_ref), (q4_ref, kv4_ref, st4_ref), (q16_ref, kv16_ref, st16_ref))
    for g, (window, dil) in enumerate(ATTN_GROUPS):
        q_ref, kv_ref, st_ref = outs[g]
        hs = slice(HEADS_PER_GROUP * g, HEADS_PER_GROUP * (g + 1))
        q_g = jnp.concatenate(qn[hs], axis=1)
        kv_g = jnp.concatenate(kn[hs] + [v[:, GROUP_WIDTH * g:GROUP_WIDTH * (g + 1)]], axis=1)
        keep = min(window, tt)
        st_ref[0] = kv_g[tt - keep:, :]
        if dil == 1:
            q_ref[0, 0] = q_g
            kv_ref[0, 0] = kv_g
        else:
            qs_ref[0] = q_g[:, :LANE]
            qs_ref[1] = jnp.concatenate([q_g[:, LANE:], jnp.zeros((tt, 2 * LANE - GROUP_WIDTH), F32)], axis=1)
            for i in range(3):
                kvs_ref[i] = kv_g[:, LANE * i:LANE * (i + 1)]
            for r in range(dil):
                rows = pl.ds(r, tt // dil, stride=dil)
                q_ref[0, r, :, 0:LANE] = qs_ref[0, rows, :]
                q_ref[0, r, :, LANE:GROUP_WIDTH] = qs_ref[1, rows, :][:, :GROUP_WIDTH - LANE]
                for i in range(3):
                    kv_ref[0, r, :, LANE * i:LANE * (i + 1)] = kvs_ref[i, rows, :]


def _mixers_prompt(z, layer, wts):
    (caw, cab, lag, lab, lbg, lbb, sw, sb_rep, ccw, gq, gk) = wts
    n, s, _ = z.shape
    tt = MIX_TT
    nt = s // tt

    def lw(shape):
        nd = len(shape)
        return pl.BlockSpec((None,) + shape, lambda b, t: (layer,) + (0,) * nd)

    in_specs = [
        pl.BlockSpec((1, tt, IN_COLS), lambda b, t: (b, t, 0)),
        lw((A_CONV_LEN, A_WIDTH)), lw((1, A_WIDTH)), lw((1, A_WIDTH)), lw((1, A_WIDTH)),
        lw((1, B_WIDTH)), lw((1, B_WIDTH)), lw((B_HEADS, CHUNK, CHUNK)), lw((CHUNK, B_WIDTH)),
        lw((C_CONV_LEN, C_WIDTH)), lw((1, HEAD_DIM)), lw((1, HEAD_DIM)),
    ]
    out_shape = [jax.ShapeDtypeStruct((n, s, ABC_PAD), BF16)]
    out_specs = [pl.BlockSpec((1, tt, ABC_PAD), lambda b, t: (b, t, 0))]
    for _, dil in ATTN_GROUPS:
        for width in (GROUP_WIDTH, 2 * GROUP_WIDTH):
            out_shape.append(jax.ShapeDtypeStruct((n, dil, s // dil, width), F32))
            out_specs.append(pl.BlockSpec((1, dil, tt // dil, width), lambda b, t: (b, 0, t, 0)))
    out_shape.append(jax.ShapeDtypeStruct((n, A_HALO, A_WIDTH), F32))
    out_specs.append(pl.BlockSpec((1, A_HALO, A_WIDTH), lambda b, t: (b, 0, 0)))
    out_shape.append(jax.ShapeDtypeStruct((n, C_HALO, C_WIDTH), F32))
    out_specs.append(pl.BlockSpec((1, C_HALO, C_WIDTH), lambda b, t: (b, 0, 0)))
    for window, _ in ATTN_GROUPS:
        keep = min(window, s)
        blk = min(keep, tt)
        first = (s - keep) // blk
        out_shape.append(jax.ShapeDtypeStruct((n, keep, 2 * GROUP_WIDTH), F32))
        if keep <= tt:
            out_specs.append(pl.BlockSpec((1, blk, 2 * GROUP_WIDTH), lambda b, t: (b, 0, 0)))
        else:
            out_specs.append(pl.BlockSpec((1, blk, 2 * GROUP_WIDTH),
                                          lambda b, t, first=first: (b, jnp.maximum(t - first, 0), 0)))
    return pl.pallas_call(
        _mixer_kernel,
        grid=(n, nt),
        in_specs=in_specs,
        out_specs=out_specs,
        out_shape=out_shape,
        scratch_shapes=[
            pltpu.VMEM((tt + A_HALO, A_WIDTH), F32),
            pltpu.VMEM((tt + C_HALO, C_WIDTH), F32),
            pltpu.VMEM((B_HEADS // 2, CHUNK, 2 * CHUNK), BF16),
            pltpu.VMEM((2, tt, LANE), F32),
            pltpu.VMEM((3, tt, LANE), F32),
            pltpu.VMEM((SUBLANE - 1, tt + A_HALO, A_WIDTH), F32),
        ],
        compiler_params=_cparams(("arbitrary", "arbitrary")),
        name="mixers_prompt",
    )(z, caw, cab, lag, lab, lbg, lbb, sw, sb_rep, ccw, gq, gk)


ATTN_QBLOCKS = 4


def _attn_kernel(q_ref, kvo_ref, kvp_ref, o_ref, l_ref, *, qblocks):
    c = pl.program_id(2)
    qi = lax.broadcasted_iota(jnp.int32, (WIN_KEYS, 2 * WIN_KEYS), 0)
    kj = lax.broadcasted_iota(jnp.int32, (WIN_KEYS, 2 * WIN_KEYS), 1)
    dist = qi + WIN_KEYS - kj
    band = (dist >= 0) & (dist <= WIN_KEYS)
    first_key = jnp.where(c > 0, 0, WIN_KEYS)
    for s in range(qblocks):
        rows = slice(WIN_KEYS * s, WIN_KEYS * (s + 1))
        q = q_ref[0, 0, rows, :]
        kvo = kvo_ref[0, 0, rows, :]
        if s == 0:
            kvp = kvp_ref[0, 0]
            mask = band & (kj >= first_key)
        else:
            kvp = kvo_ref[0, 0, WIN_KEYS * (s - 1):WIN_KEYS * s, :]
            mask = band
        o_parts, l_parts = [], []
        for h in range(HEADS_PER_GROUP):
            ks = slice(HEAD_DIM * h, HEAD_DIM * (h + 1))
            vs = slice(GROUP_WIDTH + HEAD_DIM * h, GROUP_WIDTH + HEAD_DIM * (h + 1))
            qh = q[:, ks].astype(BF16)
            kk = jnp.concatenate([kvp[:, ks], kvo[:, ks]], axis=0).astype(BF16)
            vv = jnp.concatenate([kvp[:, vs], kvo[:, vs]], axis=0).astype(BF16)
            sc = lax.dot_general(qh, kk, (((1,), (1,)), ((), ())), preferred_element_type=F32)
            sc = jnp.where(mask, sc, NEG_INF)
            m = jnp.max(sc, axis=-1, keepdims=True)
            ex = jnp.exp(sc - m)
            den = jnp.sum(ex, axis=-1, keepdims=True)
            probs = (ex / den).astype(BF16)
            o_parts.append(jnp.dot(probs, vv, preferred_element_type=F32))
            l_parts.append(jnp.broadcast_to(m + jnp.log(den), (WIN_KEYS, HEAD_DIM)))
        o_ref[0, 0, rows, :] = jnp.concatenate(o_parts, axis=1)
        l_ref[0, 0, rows, :] = jnp.concatenate(l_parts, axis=1)


def _attn_prompt(q, kv):
    n, dil, sub, _ = q.shape
    qblocks = min(ATTN_QBLOCKS, sub // WIN_KEYS)
    rows = qblocks * WIN_KEYS
    qspec = pl.BlockSpec((1, 1, rows, GROUP_WIDTH), lambda b, r, c: (b, r, c, 0))
    return pl.pallas_call(
        functools.partial(_attn_kernel, qblocks=qblocks),
        grid=(n, dil, sub // rows),
        in_specs=[
            qspec,
            pl.BlockSpec((1, 1, rows, 2 * GROUP_WIDTH), lambda b, r, c: (b, r, c, 0)),
            pl.BlockSpec((1, 1, WIN_KEYS, 2 * GROUP_WIDTH),
                         lambda b, r, c: (b, r, jnp.maximum(c * qblocks - 1, 0), 0)),
        ],
        out_specs=[qspec, qspec],
        out_shape=[jax.ShapeDtypeStruct(q.shape, F32), jax.ShapeDtypeStruct(q.shape, F32)],
        compiler_params=_cparams(("arbitrary", "arbitrary", "arbitrary")),
        name="attn_prompt",
    )(q, kv, kv)


def _combine_kernel(o1_ref, l1_ref, o4_ref, l4_ref, o16_ref, l16_ref, od_ref, s_o4, s_l4, s_o16, s_l16):
    tt = MIX_TT
    for dil, src, dst in ((4, o4_ref, s_o4), (4, l4_ref, s_l4), (16, o16_ref, s_o16), (16, l16_ref, s_l16)):
        for r in range(dil):
            x = src[0, r]
            rows = pl.ds(r, tt // dil, stride=dil)
            dst[0, rows, :] = x[:, :LANE]
            dst[1, rows, :] = jnp.concatenate(
                [x[:, LANE:], jnp.zeros((tt // dil, 2 * LANE - GROUP_WIDTH), F32)], axis=1)

    def whole(scr):
        return jnp.concatenate([scr[0], scr[1][:, :GROUP_WIDTH - LANE]], axis=1)

    outs = (o1_ref[0, 0], whole(s_o4), whole(s_o16))
    lses = (l1_ref[0, 0], whole(s_l4), whole(s_l16))
    mx = jnp.maximum(jnp.maximum(lses[0], lses[1]), lses[2])
    es = [jnp.exp(l - mx) for l in lses]
    den = es[0] + es[1] + es[2]
    pad = jnp.zeros((tt, OD_PAD // 3 - GROUP_WIDTH), F32)
    parts = []
    for g in range(3):
        parts += [outs[g] * (es[g] / den), pad]
    od_ref[0] = jnp.concatenate(parts, axis=1).astype(od_ref.dtype)


def _combine_prompt(o1, l1, o4, l4, o16, l16):
    n, _, s, _ = o1.shape
    tt = MIX_TT

    def spec(dil):
        return pl.BlockSpec((1, dil, tt // dil, GROUP_WIDTH), lambda b, t: (b, 0, t, 0))

    return pl.pallas_call(
        _combine_kernel,
        grid=(n, s // tt),
        in_specs=[spec(1), spec(1), spec(4), spec(4), spec(16), spec(16)],
        out_specs=pl.BlockSpec((1, tt, OD_PAD), lambda b, t: (b, t, 0)),
        out_shape=jax.ShapeDtypeStruct((n, s, OD_PAD), BF16),
        scratch_shapes=[pltpu.VMEM((2, tt, LANE), F32)] * 4,
        compiler_params=_cparams(("arbitrary", "arbitrary")),
        name="combine_prompt",
    )(o1, l1, o4, l4, o16, l16)


def _dec_kernel(z_ref, ha_ref, hc_ref, c1_ref, c4_ref, c16_ref,
                caw_ref, cab_ref, lag_ref, lab_ref, lbg_ref, lbb_ref, sw8_ref, sb_ref, ccw_ref, gq_ref, gk_ref,
                mix_ref, od_ref, na_ref, nc_ref, cv_ref, n1_ref, n4_ref, n16_ref,
                abuf, cbuf, kvbuf, exbuf):
    t_new = z_ref.shape[1]
    hist_a = A_CONV_LEN - 1
    hist_c = C_CONV_LEN - 1

    za = z_ref[0, :, 0:2 * A_WIDTH]
    abuf[0:hist_a, :] = ha_ref[0]
    abuf[hist_a:hist_a + t_new, :] = za[:, :A_WIDTH] * _sigmoid(za[:, A_WIDTH:])
    acc = caw_ref[0:1, :] * abuf[0:t_new, :]
    for j in range(1, A_CONV_LEN):
        acc = acc + caw_ref[j:j + 1, :] * abuf[j:j + t_new, :]
    y = _layernorm_rows(acc + cab_ref[...], lag_ref[...], lab_ref[...])
    mix_ref[0, :, 0:A_WIDTH] = _silu(y).astype(mix_ref.dtype)
    na_ref[0] = abuf[t_new:t_new + hist_a, :]

    gb = _gelu(z_ref[0, :, COL_B:COL_B + 2 * B_WIDTH])
    u = gb[:, :B_WIDTH]
    v = _layernorm_rows(gb[:, B_WIDTH:], lbg_ref[...], lbb_ref[...])
    cv_ref[0] = v
    row = lax.broadcasted_iota(jnp.int32, (t_new, B_WIDTH), 0)
    mixed = sb_ref[0:t_new, :]
    for s in range(t_new):
        mixed = mixed + jnp.where(row >= s, sw8_ref[s], 0.0) * v[s:s + 1, :]
    mix_ref[0, :, A_WIDTH:A_WIDTH + B_WIDTH] = (u * mixed).astype(mix_ref.dtype)

    g_b, g_c, x_c = _split_c(z_ref, slice(None))
    cbuf[0:hist_c, :] = hc_ref[0]
    cbuf[hist_c:hist_c + t_new, :] = g_c * x_c
    conv = ccw_ref[0:1, :] * cbuf[0:t_new, :]
    for j in range(1, C_CONV_LEN):
        conv = conv + ccw_ref[j:j + 1, :] * cbuf[j:j + t_new, :]
    o_c = jnp.concatenate([g_b * conv, jnp.zeros((t_new, ABC_PAD - 2 * A_WIDTH - C_WIDTH), F32)], axis=1)
    mix_ref[0, :, 2 * A_WIDTH:ABC_PAD] = o_c.astype(mix_ref.dtype)
    nc_ref[0] = cbuf[t_new:t_new + hist_c, :]

    q, k, v_d = _split_qkv(z_ref, slice(None))
    qn = _head_norm(q, gq_ref[...] * ATTN_SCALE)
    kn = _head_norm(k, gk_ref[...])
    caches = (c1_ref, c4_ref, c16_ref)
    news = (n1_ref, n4_ref, n16_ref)
    qrow = lax.broadcasted_iota(jnp.int32, (LANE, 2 * GROUP_WIDTH), 0)
    qlane = lax.broadcasted_iota(jnp.int32, (LANE, 2 * GROUP_WIDTH), 1)
    qmask = (qrow >> 3) == (qlane >> 6)
    zeros_q = jnp.zeros((t_new, GROUP_WIDTH), F32)
    lses, dens, offs = [], [], []
    off = 0
    for g, (window, dil) in enumerate(ATTN_GROUPS):
        hs = slice(HEADS_PER_GROUP * g, HEADS_PER_GROUP * (g + 1))
        buf_len = caches[g].shape[1]
        rows = buf_len + t_new
        kv_new = jnp.concatenate(kn[hs] + [v_d[:, GROUP_WIDTH * g:GROUP_WIDTH * (g + 1)]], axis=1)
        kvbuf[off:off + buf_len, :] = caches[g][0]
        kvbuf[off + buf_len:off + rows, :] = kv_new
        news[g][0] = kvbuf[off + t_new:off + rows, :]
        q_g = jnp.concatenate(qn[hs] + [zeros_q], axis=1)
        q_rep = jnp.concatenate([q_g] * HEADS_PER_GROUP
                                + [jnp.zeros((LANE - HEADS_PER_GROUP * t_new, 2 * GROUP_WIDTH), F32)], axis=0)
        q_hi, q_lo = _split_bf16(jnp.where(qmask, q_rep, 0.0))
        kv_hi, kv_lo = _split_bf16(kvbuf[off:off + rows, :])
        nt_dims = (((1,), (1,)), ((), ()))
        sc = (lax.dot_general(kv_hi, q_hi, nt_dims, preferred_element_type=F32)
              + lax.dot_general(kv_lo, q_hi, nt_dims, preferred_element_type=F32)
              + lax.dot_general(kv_hi, q_lo, nt_dims, preferred_element_type=F32))
        krow = lax.broadcasted_iota(jnp.int32, (rows, LANE), 0)
        tok = lax.broadcasted_iota(jnp.int32, (rows, LANE), 1) & (t_new - 1)
        dist = buf_len + tok - krow
        valid = (dist >= 0) & (dist <= dil * WIN_KEYS) & ((dist & (dil - 1)) == 0)
        sc = jnp.where(valid, sc, NEG_INF)
        m = jnp.max(sc, axis=0, keepdims=True)
        ex = jnp.exp(sc - m)
        den = jnp.sum(ex, axis=0, keepdims=True)
        exbuf[off:off + rows, :] = ex
        lses.append(m + jnp.log(den))
        dens.append(den)
        offs.append((off, rows))
        off += rows
    mx = jnp.maximum(jnp.maximum(lses[0], lses[1]), lses[2])
    es = [jnp.exp(l - mx) for l in lses]
    tot = es[0] + es[1] + es[2]
    lane = lax.broadcasted_iota(jnp.int32, (t_new, 2 * GROUP_WIDTH), 1)
    pad = jnp.zeros((t_new, OD_PAD // 3 - GROUP_WIDTH), F32)
    parts = []
    for g in range(3):
        off, rows = offs[g]
        coef = es[g] / (tot * dens[g])
        probs = (exbuf[off:off + rows, :] * coef).astype(BF16)
        kv_all = kvbuf[off:off + rows, :].astype(BF16)
        o_t = lax.dot_general(probs, kv_all, (((0,), (0,)), ((), ())), preferred_element_type=F32)
        o_g = jnp.zeros((t_new, 2 * GROUP_WIDTH), F32)
        for h in range(HEADS_PER_GROUP):
            sel = (lane >= GROUP_WIDTH + HEAD_DIM * h) & (lane < GROUP_WIDTH + HEAD_DIM * (h + 1))
            o_g = o_g + jnp.where(sel, o_t[t_new * h:t_new * (h + 1), :], 0.0)
        parts += [o_g[:, GROUP_WIDTH:], pad]
    od_ref[0] = jnp.concatenate(parts, axis=1).astype(od_ref.dtype)


def _mixers_sample(z, layer, hist_a, hist_c, caches, wts):
    (caw, cab, lag, lab, lbg, lbb, sw8, sb_rep, ccw, gq, gk) = wts
    n, t_new, _ = z.shape

    def lw(shape):
        nd = len(shape)
        return pl.BlockSpec((None,) + shape, lambda b: (layer,) + (0,) * nd)

    def st(shape):
        nd = len(shape)
        return pl.BlockSpec((None, 1) + shape, lambda b: (layer, b) + (0,) * nd)

    lens = [c.shape[2] for c in caches]
    total_rows = sum(lens) + 3 * t_new
    in_specs = [
        pl.BlockSpec((1, t_new, IN_COLS), lambda b: (b, 0, 0)),
        st((A_CONV_LEN - 1, A_WIDTH)), st((C_CONV_LEN - 1, C_WIDTH)),
        st((lens[0], 2 * GROUP_WIDTH)), st((lens[1], 2 * GROUP_WIDTH)), st((lens[2], 2 * GROUP_WIDTH)),
        lw((A_CONV_LEN, A_WIDTH)), lw((1, A_WIDTH)), lw((1, A_WIDTH)), lw((1, A_WIDTH)),
        lw((1, B_WIDTH)), lw((1, B_WIDTH)), lw((t_new, t_new, B_WIDTH)), lw((CHUNK, B_WIDTH)),
        lw((C_CONV_LEN, C_WIDTH)), lw((1, HEAD_DIM)), lw((1, HEAD_DIM)),
    ]

    def ob(shape):
        nd = len(shape)
        return pl.BlockSpec((1,) + shape, lambda b: (b,) + (0,) * nd)

    out_shape = [
        jax.ShapeDtypeStruct((n, t_new, ABC_PAD), F32),
        jax.ShapeDtypeStruct((n, t_new, OD_PAD), F32),
        jax.ShapeDtypeStruct((n, A_CONV_LEN - 1, A_WIDTH), F32),
        jax.ShapeDtypeStruct((n, C_CONV_LEN - 1, C_WIDTH), F32),
        jax.ShapeDtypeStruct((n, t_new, B_WIDTH), F32),
    ] + [jax.ShapeDtypeStruct((n, ln, 2 * GROUP_WIDTH), F32) for ln in lens]
    out_specs = [ob(s.shape[1:]) for s in out_shape]
    return pl.pallas_call(
        _dec_kernel,
        grid=(n,),
        in_specs=in_specs,
        out_specs=out_specs,
        out_shape=out_shape,
        scratch_shapes=[
            pltpu.VMEM((A_CONV_LEN - 1 + t_new + 2, A_WIDTH), F32),
            pltpu.VMEM((16, C_WIDTH), F32),
            pltpu.VMEM((total_rows, 2 * GROUP_WIDTH), F32),
            pltpu.VMEM((total_rows, LANE), F32),
        ],
        compiler_params=_cparams(("arbitrary",)),
        name="mixers_sample",
    )(z, hist_a, hist_c, *caches, caw, cab, lag, lab, lbg, lbb, sw8, sb_rep, ccw, gq, gk)


def _outproj_kernel(h_ref, mix_ref, od_ref, w_ref, wd_ref, o_ref, *, hp):
    acc = _mm(mix_ref[...], w_ref[0:ABC_PAD, :], hp) + _mm(od_ref[...], wd_ref[...], hp)
    o_ref[...] = h_ref[...] + acc


def _outproj(h, mix, od, w_all, wd_all, layer, tm, tn, hp):
    rows = h.shape[0]
    return pl.pallas_call(
        functools.partial(_outproj_kernel, hp=hp),
        grid=(rows // tm, D_MODEL // tn),
        in_specs=[
            pl.BlockSpec((tm, tn), lambda i, j: (i, j)),
            pl.BlockSpec((tm, ABC_PAD), lambda i, j: (i, 0)),
            pl.BlockSpec((tm, OD_PAD), lambda i, j: (i, 0)),
            pl.BlockSpec((None, D_MODEL, tn), lambda i, j: (layer, 0, j)),
            pl.BlockSpec((None, OD_PAD, tn), lambda i, j: (layer, 0, j)),
        ],
        out_specs=pl.BlockSpec((tm, tn), lambda i, j: (i, j)),
        out_shape=jax.ShapeDtypeStruct((rows, D_MODEL), F32),
        compiler_params=_cparams(("arbitrary", "arbitrary")),
        name="outproj",
    )(h, mix, od, w_all, wd_all)


def _top2_in_top_group(logits):
    rows = logits.shape[0]
    lane = lax.broadcasted_iota(jnp.int32, (rows, ROUTER_COLS), 1)
    big = jnp.int32(ROUTER_COLS)
    is_grp = (lane >= N_EXPERTS) & (lane < N_EXPERTS + MOE_GROUPS)
    gl = jnp.where(is_grp, logits, NEG_INF)
    gmax = jnp.max(gl, axis=-1, keepdims=True)
    gidx = jnp.min(jnp.where(gl == gmax, lane - N_EXPERTS, big), axis=-1, keepdims=True)
    gate = 1.0 / jnp.sum(jnp.where(is_grp, jnp.exp(gl - gmax), 0.0), axis=-1, keepdims=True)
    in_grp = (lane < N_EXPERTS) & ((lane >> 2) == gidx)
    el = jnp.where(in_grp, logits, NEG_INF)
    t1 = jnp.max(el, axis=-1, keepdims=True)
    i1 = jnp.min(jnp.where(el == t1, lane, big), axis=-1, keepdims=True)
    el2 = jnp.where(lane == i1, NEG_INF, el)
    t2 = jnp.max(el2, axis=-1, keepdims=True)
    i2 = jnp.min(jnp.where(el2 == t2, lane, big), axis=-1, keepdims=True)
    e2 = jnp.exp(t2 - t1)
    return lane, i1, i2, gate / (1.0 + e2), gate * e2 / (1.0 + e2)


META_E1, META_E2, META_R1, META_R2, META_W1, META_W2 = range(6)


def _outrouter_kernel(h_ref, mix_ref, od_ref, g_ref, wr_ref, br_ref, wout_hbm, woutd_hbm,
                      h_out, m_ref, meta_ref, meta_t_ref, cnt_ref,
                      wo_b, wod_b, stage, sem, carry_ref, *, layer):
    @pl.when(pl.program_id(0) == 0)
    def _():
        carry_ref[...] = jnp.zeros_like(carry_ref)
        _stream_cast(wout_hbm, layer, wo_b, stage, sem)
        _stream_cast(woutd_hbm, layer, wod_b, stage, sem)

    acc = (jnp.dot(mix_ref[...], wo_b[...], preferred_element_type=F32)
           + jnp.dot(od_ref[...], wod_b[...], preferred_element_type=F32))
    h1 = h_ref[...] + acc
    h_out[...] = h1
    m = _rms_rows(h1, g_ref[...])
    m_ref[...] = m
    logits = _mm(m, wr_ref[...], False) + br_ref[...]
    rows = logits.shape[0]
    lane, i1, i2, w1, w2 = _top2_in_top_group(logits)
    sel = jnp.where((lane == i1) | (lane == i2), 1.0, 0.0)
    r = lax.broadcasted_iota(jnp.int32, (rows, rows), 0)
    c = lax.broadcasted_iota(jnp.int32, (rows, rows), 1)
    earlier = jnp.where(c < r, 1.0, 0.0).astype(BF16)
    rank = jnp.dot(earlier, sel.astype(BF16), preferred_element_type=F32) + carry_ref[...]
    r1 = jnp.sum(jnp.where(lane == i1, rank, 0.0), axis=-1, keepdims=True)
    r2 = jnp.sum(jnp.where(lane == i2, rank, 0.0), axis=-1, keepdims=True)
    carry_ref[...] += jnp.sum(sel, axis=0, keepdims=True)
    cnt_ref[...] = carry_ref[...]
    meta = jnp.zeros((rows, ROUTER_COLS), F32)
    for pos, val in ((META_E1, i1.astype(F32)), (META_E2, i2.astype(F32)), (META_R1, r1), (META_R2, r2),
                     (META_W1, w1), (META_W2, w2)):
        meta = jnp.where(lane == pos, val, meta)
    meta_ref[...] = meta
    meta_t_ref[...] = meta.T[:SUBLANE, :]


OUTROUTER_TM = 256


def _outrouter(h, mix, od, g_all, wr_all, br_all, w_out, w_out_d, layer):
    rows = h.shape[0]
    tm = OUTROUTER_TM
    row_spec = pl.BlockSpec((tm, D_MODEL), lambda i: (i, 0))
    return pl.pallas_call(
        functools.partial(_outrouter_kernel, layer=layer),
        grid=(rows // tm,),
        in_specs=[
            row_spec,
            pl.BlockSpec((tm, ABC_PAD), lambda i: (i, 0)),
            pl.BlockSpec((tm, OD_PAD), lambda i: (i, 0)),
            pl.BlockSpec((None, 1, D_MODEL), lambda i: (layer, 0, 0)),
            pl.BlockSpec((None, D_MODEL, ROUTER_COLS), lambda i: (layer, 0, 0)),
            pl.BlockSpec((None, 1, ROUTER_COLS), lambda i: (layer, 0, 0)),
            pl.BlockSpec(memory_space=pl.ANY),
            pl.BlockSpec(memory_space=pl.ANY),
        ],
        out_specs=[row_spec, row_spec,
                   pl.BlockSpec((tm, ROUTER_COLS), lambda i: (i, 0)),
                   pl.BlockSpec((SUBLANE, tm), lambda i: (0, i)),
                   pl.BlockSpec((1, ROUTER_COLS), lambda i: (0, 0))],
        out_shape=[jax.ShapeDtypeStruct((rows, D_MODEL), F32),
                   jax.ShapeDtypeStruct((rows, D_MODEL), F32),
                   jax.ShapeDtypeStruct((rows, ROUTER_COLS), F32),
                   jax.ShapeDtypeStruct((SUBLANE, rows), F32),
                   jax.ShapeDtypeStruct((1, ROUTER_COLS), F32)],
        scratch_shapes=[pltpu.VMEM((ABC_PAD, D_MODEL), BF16),
                        pltpu.VMEM((OD_PAD, D_MODEL), BF16),
                        pltpu.VMEM((2, D_MODEL, W_CHUNK), F32),
                        pltpu.SemaphoreType.DMA((2,)),
                        pltpu.VMEM((1, ROUTER_COLS), F32)],
        compiler_params=_cparams(("arbitrary",)),
        name="outrouter",
    )(h, mix, od, g_all, wr_all, br_all, w_out, w_out_d)


EXPERT_TILE = 256


def _experts_kernel(te_ref, tv_ref, s0_ref, s1_ref, nu_ref,
                    m_hbm, wg_ref, wu_ref, wd_ref, yk_hbm,
                    gsrc_ref, ssrc_ref, xbuf, ybuf, wgb, wub, wdb, gsem, ssem, *, n_tok):
    i = pl.program_id(0)
    tile = EXPERT_TILE
    nused = nu_ref[0]
    slot = lax.rem(i, 2)
    plane = n_tok + 2 * SUBLANE

    def rows_moved(t):
        return pl.multiple_of(((tv_ref[t] + SUBLANE - 1) // SUBLANE) * SUBLANE, SUBLANE)

    def gather_copy(s, j, b):
        return pltpu.make_async_copy(m_hbm.at[pl.ds(gsrc_ref[s], 1), :], xbuf.at[b, pl.ds(j, 1), :], gsem.at[b])

    def scatter_copy(s, j, b):
        return pltpu.make_async_copy(ybuf.at[b, pl.ds(j, 1), :], yk_hbm.at[pl.ds(ssrc_ref[s], 1), :], ssem.at[b])

    def start_rows(copy_fn, t, b):
        def body(j8, carry):
            for u in range(SUBLANE):
                j = j8 * SUBLANE + u
                copy_fn(t * tile + j, j, b).start()
            return carry
        lax.fori_loop(0, rows_moved(t) // SUBLANE, body, 0)

    def wait_gather(t, b):
        n = rows_moved(t)
        pltpu.make_async_copy(m_hbm.at[pl.ds(0, n), :], xbuf.at[b, pl.ds(0, n), :], gsem.at[b]).wait()

    def wait_scatter(t, b):
        n = rows_moved(t)
        pltpu.make_async_copy(ybuf.at[b, pl.ds(0, n), :], yk_hbm.at[pl.ds(0, n), :], ssem.at[b]).wait()

    @pl.when(i == 0)
    def _():
        def fill(t8, carry):
            for u in range(SUBLANE):
                t = t8 * SUBLANE + u
                gsrc_ref[s0_ref[t]] = t
                gsrc_ref[s1_ref[t]] = t
                ssrc_ref[s0_ref[t]] = t
                ssrc_ref[s1_ref[t]] = plane + t
            return carry
        lax.fori_loop(0, n_tok // SUBLANE, fill, 0)

        def fill_pad(t, carry):
            def one(j, c):
                gsrc_ref[t * tile + j] = 0
                ssrc_ref[t * tile + j] = n_tok + SUBLANE * lax.rem(t, 2) + lax.rem(j, SUBLANE)
                return c
            lax.fori_loop(tv_ref[t], rows_moved(t), one, 0)
            return carry
        lax.fori_loop(0, nused, fill_pad, 0)
        xbuf[...] = jnp.zeros_like(xbuf)
        for k in range(2):
            spare = pltpu.make_async_copy(xbuf.at[0, pl.ds(0, 2 * SUBLANE), :],
                                          yk_hbm.at[pl.ds(k * plane + n_tok, 2 * SUBLANE), :], ssem.at[0])
            spare.start()
            spare.wait()
        start_rows(gather_copy, 0, 0)

    @pl.when(i < nused)
    def _():
        @pl.when(i + 1 < nused)
        def _():
            start_rows(gather_copy, i + 1, 1 - slot)

        changed = jnp.logical_or(i == 0, te_ref[i] != te_ref[jnp.maximum(i - 1, 0)])

        @pl.when(changed)
        def _():
            wgb[...] = wg_ref[...].astype(BF16)
            wub[...] = wu_ref[...].astype(BF16)
            wdb[...] = wd_ref[...].astype(BF16)

        wait_gather(i, slot)

        @pl.when(i >= 2)
        def _():
            wait_scatter(i - 2, slot)

        x = xbuf[slot].astype(BF16)
        gate = jnp.dot(x, wgb[...], preferred_element_type=F32)
        up = jnp.dot(x, wub[...], preferred_element_type=F32)
        ybuf[slot] = jnp.dot((_silu(gate) * up).astype(BF16), wdb[...], preferred_element_type=F32)
        start_rows(scatter_copy, i, slot)

        @pl.when(i == nused - 1)
        def _():
            @pl.when(i >= 1)
            def _():
                wait_scatter(i - 1, 1 - slot)
            wait_scatter(i, slot)


def _experts_sparse(m, tile_expert, tile_rows, slot0, slot1, nused, wg_all, wu_all, wd_all, layer):
    n_tok = m.shape[0]
    n_tiles = tile_expert.shape[0]

    def wspec(shape):
        return pl.BlockSpec((None, None) + shape, lambda i, te, tv, s0, s1, nu: (layer, te[i], 0, 0))

    grid_spec = pltpu.PrefetchScalarGridSpec(
        num_scalar_prefetch=5,
        grid=(n_tiles,),
        in_specs=[pl.BlockSpec(memory_space=pl.ANY),
                  wspec((D_MODEL, EXPERT_FF)), wspec((D_MODEL, EXPERT_FF)), wspec((EXPERT_FF, D_MODEL))],
        out_specs=pl.BlockSpec(memory_space=pl.ANY),
        scratch_shapes=[
            pltpu.SMEM((n_tiles * EXPERT_TILE,), jnp.int32),
            pltpu.SMEM((n_tiles * EXPERT_TILE,), jnp.int32),
            pltpu.VMEM((2, EXPERT_TILE, D_MODEL), F32),
            pltpu.VMEM((2, EXPERT_TILE, D_MODEL), F32),
            pltpu.VMEM((D_MODEL, EXPERT_FF), BF16),
            pltpu.VMEM((D_MODEL, EXPERT_FF), BF16),
            pltpu.VMEM((EXPERT_FF, D_MODEL), BF16),
            pltpu.SemaphoreType.DMA((2,)),
            pltpu.SemaphoreType.DMA((2,)),
        ],
    )
    return pl.pallas_call(
        functools.partial(_experts_kernel, n_tok=n_tok),
        grid_spec=grid_spec,
        out_shape=jax.ShapeDtypeStruct((2 * (n_tok + 2 * SUBLANE), D_MODEL), F32),
        compiler_params=_cparams(("arbitrary",)),
        name="experts_sparse",
    )(tile_expert, tile_rows, slot0, slot1, nused, m, wg_all, wu_all, wd_all)


def _expert_plan(meta_t, cnt, n_tiles):
    tile = EXPERT_TILE
    counts = cnt[0, :N_EXPERTS].astype(jnp.int32)
    padded = ((counts + tile - 1) // tile) * tile
    ends = jnp.cumsum(padded)
    base = ends - padded
    e1 = meta_t[META_E1].astype(jnp.int32)
    e2 = meta_t[META_E2].astype(jnp.int32)
    slot0 = base[e1] + meta_t[META_R1].astype(jnp.int32)
    slot1 = base[e2] + meta_t[META_R2].astype(jnp.int32)
    start = jnp.arange(n_tiles, dtype=jnp.int32) * tile
    expert_of = jnp.sum((start[:, None] >= ends[None, :]).astype(jnp.int32), axis=1)
    last_used = jnp.max(jnp.where(counts > 0, jnp.arange(N_EXPERTS, dtype=jnp.int32), 0))
    tile_expert = jnp.minimum(expert_of, last_used)
    e_clamped = jnp.minimum(expert_of, N_EXPERTS - 1)
    tile_rows = jnp.clip(counts[e_clamped] - (start - base[e_clamped]), 0, tile)
    tile_rows = jnp.where(expert_of < N_EXPERTS, tile_rows, 0).astype(jnp.int32)
    nused = (ends[-1] // tile).astype(jnp.int32).reshape(1)
    return tile_expert, tile_rows, slot0, slot1, nused


def _router_kernel(h_ref, g_ref, wr_ref, br_ref, m_ref, comb_ref, *, hp):
    m = _rms_rows(h_ref[...], g_ref[...]).astype(m_ref.dtype)
    m_ref[...] = m
    logits = _mm(m, wr_ref[...], hp) + br_ref[...]
    rows = logits.shape[0]
    lane = lax.broadcasted_iota(jnp.int32, (rows, ROUTER_COLS), 1)
    big = jnp.int32(ROUTER_COLS)
    is_grp = (lane >= N_EXPERTS) & (lane < N_EXPERTS + MOE_GROUPS)
    gl = jnp.where(is_grp, logits, NEG_INF)
    gmax = jnp.max(gl, axis=-1, keepdims=True)
    gidx = jnp.min(jnp.where(gl == gmax, lane - N_EXPERTS, big), axis=-1, keepdims=True)
    gate = 1.0 / jnp.sum(jnp.where(is_grp, jnp.exp(gl - gmax), 0.0), axis=-1, keepdims=True)
    in_grp = (lane < N_EXPERTS) & ((lane >> 2) == gidx)
    el = jnp.where(in_grp, logits, NEG_INF)
    t1 = jnp.max(el, axis=-1, keepdims=True)
    i1 = jnp.min(jnp.where(el == t1, lane, big), axis=-1, keepdims=True)
    el2 = jnp.where(lane == i1, NEG_INF, el)
    t2 = jnp.max(el2, axis=-1, keepdims=True)
    i2 = jnp.min(jnp.where(el2 == t2, lane, big), axis=-1, keepdims=True)
    e2 = jnp.exp(t2 - t1)
    w1 = gate / (1.0 + e2)
    w2 = gate * e2 / (1.0 + e2)
    comb_ref[...] = jnp.where(lane == i1, w1, 0.0) + jnp.where(lane == i2, w2, 0.0)


def _router(h, g_all, wr_all, br_all, layer, tm, hp):
    rows = h.shape[0]
    return pl.pallas_call(
        functools.partial(_router_kernel, hp=hp),
        grid=(rows // tm,),
        in_specs=[
            pl.BlockSpec((tm, D_MODEL), lambda i: (i, 0)),
            pl.BlockSpec((None, 1, D_MODEL), lambda i: (layer, 0, 0)),
            pl.BlockSpec((None, D_MODEL, ROUTER_COLS), lambda i: (layer, 0, 0)),
            pl.BlockSpec((None, 1, ROUTER_COLS), lambda i: (layer, 0, 0)),
        ],
        out_specs=[pl.BlockSpec((tm, D_MODEL), lambda i: (i, 0)),
                   pl.BlockSpec((tm, ROUTER_COLS), lambda i: (i, 0))],
        out_shape=[jax.ShapeDtypeStruct((rows, D_MODEL), _act_dtype(hp)),
                   jax.ShapeDtypeStruct((rows, ROUTER_COLS), F32)],
        compiler_params=_cparams(("arbitrary",)),
        name="router",
    )(h, g_all, wr_all, br_all)


def _moe_kernel(h_ref, m_ref, comb_ref, wg_ref, wu_ref, wd_ref, o_ref, *, hp):
    e = pl.program_id(1)

    @pl.when(e == 0)
    def _():
        o_ref[...] = h_ref[...]

    x = m_ref[...]
    gate = _mm(x, wg_ref[...], hp)
    up = _mm(x, wu_ref[...], hp)
    lane = lax.broadcasted_iota(jnp.int32, comb_ref.shape, 1)
    w = jnp.sum(jnp.where(lane == e, comb_ref[...], 0.0), axis=-1, keepdims=True)
    o_ref[...] += _mm(_silu(gate) * up * w, wd_ref[...], hp)


def _moe_dense(h, m, comb, wg_all, wu_all, wd_all, layer, tm, hp):
    rows = h.shape[0]
    return pl.pallas_call(
        functools.partial(_moe_kernel, hp=hp),
        grid=(rows // tm, N_EXPERTS),
        in_specs=[
            pl.BlockSpec((tm, D_MODEL), lambda i, e: (i, 0)),
            pl.BlockSpec((tm, D_MODEL), lambda i, e: (i, 0)),
            pl.BlockSpec((tm, ROUTER_COLS), lambda i, e: (i, 0)),
            pl.BlockSpec((None, None, D_MODEL, EXPERT_FF), lambda i, e: (layer, e, 0, 0)),
            pl.BlockSpec((None, None, D_MODEL, EXPERT_FF), lambda i, e: (layer, e, 0, 0)),
            pl.BlockSpec((None, None, EXPERT_FF, D_MODEL), lambda i, e: (layer, e, 0, 0)),
        ],
        out_specs=pl.BlockSpec((tm, D_MODEL), lambda i, e: (i, 0)),
        out_shape=jax.ShapeDtypeStruct((rows, D_MODEL), F32),
        compiler_params=_cparams(("arbitrary", "arbitrary")),
        name="moe_dense",
    )(h, m, comb, wg_all, wu_all, wd_all)


def _ple_kernel(h_ref, hc_ref, g_ref, p_ref, wg_ref, wp_ref, o_ref, xn_ref, *, hp):
    @pl.when(pl.program_id(1) == 0)
    def _():
        xn_ref[...] = _rms_rows(h_ref[...], g_ref[...]).astype(xn_ref.dtype)

    gate = _sigmoid(_mm(xn_ref[...], wg_ref[...], hp))
    o_ref[...] = hc_ref[...] + gate * _mm(p_ref[...], wp_ref[...], hp)


def _ple(h, p_all, g_all, wg_all, wp_all, layer, tm, tn, hp):
    rows = h.shape[0]
    return pl.pallas_call(
        functools.partial(_ple_kernel, hp=hp),
        grid=(rows // tm, D_MODEL // tn),
        in_specs=[
            pl.BlockSpec((tm, D_MODEL), lambda i, j: (i, 0)),
            pl.BlockSpec((tm, tn), lambda i, j: (i, j)),
            pl.BlockSpec((None, 1, D_MODEL), lambda i, j: (layer, 0, 0)),
            pl.BlockSpec((None, tm, PLE_DIM), lambda i, j: (layer, i, 0)),
            pl.BlockSpec((None, D_MODEL, tn), lambda i, j: (layer, 0, j)),
            pl.BlockSpec((None, PLE_DIM, tn), lambda i, j: (layer, 0, j)),
        ],
        out_specs=pl.BlockSpec((tm, tn), lambda i, j: (i, j)),
        out_shape=jax.ShapeDtypeStruct((rows, D_MODEL), F32),
        scratch_shapes=[pltpu.VMEM((tm, D_MODEL), _act_dtype(hp))],
        compiler_params=_cparams(("arbitrary", "arbitrary")),
        name="ple",
    )(h, h, g_all, p_all, wg_all, wp_all)


def _token_tail_hp(h, mix, od, p_all, layer, tw):
    (w_out, w_out_d, g_ffn, w_router, b_router, w_gate, w_up, w_down, g_ple, w_ple_gate, w_ple_proj) = tw
    tm = h.shape[0]
    h = _outproj(h, mix, od, w_out, w_out_d, layer, tm, 512, True)
    m, comb = _router(h, g_ffn, w_router, b_router, layer, tm, True)
    h = _moe_dense(h, m, comb, w_gate, w_up, w_down, layer, tm, True)
    return _ple(h, p_all, g_ple, w_ple_gate, w_ple_proj, layer, tm, 512, True)


POST_TM = 128
W_CHUNK = 256
Z_PARTS = 1
Z_CHUNK = 1280
POST_VMEM_LIMIT = 60 * 1024 * 1024


def _stream_cast(w_hbm, layer, dst, stage, sem, col0=0):
    k_rows = dst.shape[0]
    n_chunks = dst.shape[1] // W_CHUNK

    def chunk_copy(c):
        return pltpu.make_async_copy(w_hbm.at[layer, pl.ds(0, k_rows), pl.ds(col0 + c * W_CHUNK, W_CHUNK)],
                                     stage.at[c % 2, pl.ds(0, k_rows), :], sem.at[c % 2])

    chunk_copy(0).start()
    for c in range(n_chunks):
        if c + 1 < n_chunks:
            chunk_copy(c + 1).start()
        chunk_copy(c).wait()
        dst[:, c * W_CHUNK:(c + 1) * W_CHUNK] = stage[c % 2, 0:k_rows, :].astype(BF16)


def _post_kernel(*refs, layer, with_inproj):
    if with_inproj:
        (h_ref, y0_ref, y1_ref, meta_ref, p_ref, gp_ref, wpp_ref, wpg_hbm, gm_ref, win_hbm,
         h_out, z_out, wpg_b, stage, sem, win_b, a_scr) = refs
    else:
        (h_ref, y0_ref, y1_ref, meta_ref, p_ref, gp_ref, wpp_ref, wpg_hbm,
         h_out, wpg_b, stage, sem) = refs
    part = pl.program_id(1)
    part_cols = IN_COLS // Z_PARTS

    @pl.when(jnp.logical_and(pl.program_id(0) == 0, part == 0))
    def _():
        _stream_cast(wpg_hbm, layer, wpg_b, stage, sem)
        if with_inproj:
            for k in range(Z_PARTS):
                _stream_cast(win_hbm, layer + 1, win_b.at[k], stage, sem, col0=k * part_cols)

    @pl.when(part == 0)
    def _():
        meta = meta_ref[...]
        lane = lax.broadcasted_iota(jnp.int32, meta.shape, 1)
        w1 = jnp.sum(jnp.where(lane == META_W1, meta, 0.0), axis=-1, keepdims=True)
        w2 = jnp.sum(jnp.where(lane == META_W2, meta, 0.0), axis=-1, keepdims=True)
        hn = h_ref[...] + w1 * y0_ref[...] + w2 * y1_ref[...]
        xn = _rms_rows(hn, gp_ref[...]).astype(BF16)
        gate = _sigmoid(jnp.dot(xn, wpg_b[...], preferred_element_type=F32))
        h3 = hn + gate * _mm(p_ref[...], wpp_ref[...], False)
        h_out[...] = h3
        if with_inproj:
            a_scr[...] = _rms_rows(h3, gm_ref[...]).astype(BF16)

    if with_inproj:
        for c0 in range(0, part_cols, Z_CHUNK):
            z_out[:, c0:c0 + Z_CHUNK] = jnp.dot(a_scr[...], win_b[part, :, c0:c0 + Z_CHUNK],
                                                preferred_element_type=F32)


def _post(h, yk, meta, p_all, g_ple, w_ple_gate, w_ple_proj, g_mix, w_in, layer):
    rows = h.shape[0]
    tm = POST_TM
    with_inproj = layer + 1 < DEPTH
    row_spec = pl.BlockSpec((tm, D_MODEL), lambda i, j: (i, 0))
    in_specs = [
        row_spec,
        pl.BlockSpec((None, tm, D_MODEL), lambda i, j: (0, i, 0)),
        pl.BlockSpec((None, tm, D_MODEL), lambda i, j: (1, i, 0)),
        pl.BlockSpec((tm, ROUTER_COLS), lambda i, j: (i, 0)),
        pl.BlockSpec((None, tm, PLE_DIM), lambda i, j: (layer, i, 0)),
        pl.BlockSpec((None, 1, D_MODEL), lambda i, j: (layer, 0, 0)),
        pl.BlockSpec((None, PLE_DIM, D_MODEL), lambda i, j: (layer, 0, 0)),
        pl.BlockSpec(memory_space=pl.ANY),
    ]
    args = [h, yk, yk, meta, p_all, g_ple, w_ple_proj, w_ple_gate]
    out_specs = [row_spec]
    out_shape = [jax.ShapeDtypeStruct((rows, D_MODEL), F32)]
    scratch = [pltpu.VMEM((D_MODEL, D_MODEL), BF16),
               pltpu.VMEM((2, D_MODEL, W_CHUNK), F32),
               pltpu.SemaphoreType.DMA((2,))]
    if with_inproj:
        part_cols = IN_COLS // Z_PARTS
        in_specs += [pl.BlockSpec((None, 1, D_MODEL), lambda i, j: (layer + 1, 0, 0)),
                     pl.BlockSpec(memory_space=pl.ANY)]
        args += [g_mix, w_in]
        out_specs.append(pl.BlockSpec((tm, part_cols), lambda i, j: (i, j)))
        out_shape.append(jax.ShapeDtypeStruct((rows, IN_COLS), F32))
        scratch += [pltpu.VMEM((Z_PARTS, D_MODEL, part_cols), BF16), pltpu.VMEM((tm, D_MODEL), BF16)]
    return pl.pallas_call(
        functools.partial(_post_kernel, layer=layer, with_inproj=with_inproj),
        grid=(rows // tm, Z_PARTS if with_inproj else 1),
        in_specs=in_specs,
        out_specs=out_specs,
        out_shape=out_shape,
        scratch_shapes=scratch,
        compiler_params=pltpu.CompilerParams(dimension_semantics=("arbitrary", "arbitrary"),
                                             vmem_limit_bytes=POST_VMEM_LIMIT),
        name="post",
    )(*args)


def _token_tail_prompt(h, mix, od, p_all, layer, tw, g_mix, w_in):
    (w_out, w_out_d, g_ffn, w_router, b_router, w_gate, w_up, w_down, g_ple, w_ple_gate, w_ple_proj) = tw
    rows = h.shape[0]
    n_tiles = (2 * rows + N_EXPERTS * (EXPERT_TILE - 1) + EXPERT_TILE - 1) // EXPERT_TILE
    h, m, meta, meta_t, cnt = _outrouter(h, mix, od, g_ffn, w_router, b_router, w_out, w_out_d, layer)
    plan = _expert_plan(meta_t, cnt, n_tiles)
    yk = _experts_sparse(m, *plan, w_gate, w_up, w_down, layer).reshape(2, rows + 2 * SUBLANE, D_MODEL)
    out = _post(h, yk, meta, p_all, g_ple, w_ple_gate, w_ple_proj, g_mix, w_in, layer)
    return (out[0], out[1]) if layer + 1 < DEPTH else (out[0], None)


TAIL = 8


def _patch_kernel(x_ref, tail_ref, o_ref):
    del x_ref
    o_ref[...] = tail_ref[...]


def _patch_tail(x, tail):
    n, s, width = x.shape
    return pl.pallas_call(
        _patch_kernel,
        grid=(n,),
        in_specs=[pl.BlockSpec(memory_space=pl.ANY),
                  pl.BlockSpec((1, TAIL, width), lambda b: (b, 0, 0))],
        out_specs=pl.BlockSpec((1, TAIL, width), lambda b: (b, s // TAIL - 1, 0)),
        out_shape=jax.ShapeDtypeStruct(x.shape, x.dtype),
        input_output_aliases={0: 0},
        compiler_params=_cparams(("arbitrary",)),
        name="patch_tail",
    )(x, tail)


def kernel(x_prompt, x_sample, p_prompt, p_sample, state_conv_a, state_conv_c, cache_kv_w128, cache_kv_w512, cache_kv_w2048, g_mix, w_in, conv_a_w, conv_a_b, ln_a_g, ln_a_b, ln_b_g, ln_b_b, sgu_w, sgu_b, conv_c_w, g_q, g_k, w_out, g_ffn, w_router_grp, b_router_grp, w_router_exp, b_router_exp, w_gate, w_up, w_down, g_ple, w_ple_gate, w_ple_proj):
    n_p, s_p, _ = x_prompt.shape
    n_s, t_s, _ = x_sample.shape
    rows_p = n_p * s_p
    rows_s = n_s * t_s

    def row3(a):
        return a.reshape(DEPTH, 1, a.shape[-1])

    g_mix3, g_ffn3, g_ple3 = row3(g_mix), row3(g_ffn), row3(g_ple)
    cab3, lag3, lab3, lbg3, lbb3 = row3(conv_a_b), row3(ln_a_g), row3(ln_a_b), row3(ln_b_g), row3(ln_b_b)
    gq3, gk3 = row3(g_q), row3(g_k)
    sb_rep = jnp.repeat(jnp.swapaxes(sgu_b, 1, 2), HEAD_DIM, axis=2)
    sw8 = jnp.repeat(jnp.transpose(sgu_w[:, :, :t_s, :t_s], (0, 3, 2, 1)), HEAD_DIM, axis=3)
    w_out_d = jnp.pad(w_out[:, ABC_PAD - 64:].reshape(DEPTH, 3, GROUP_WIDTH, D_MODEL),
                      ((0, 0), (0, 0), (0, OD_PAD // 3 - GROUP_WIDTH), (0, 0))).reshape(DEPTH, OD_PAD, D_MODEL)
    w_router = jnp.concatenate(
        [jnp.transpose(w_router_exp, (0, 2, 1, 3)).reshape(DEPTH, D_MODEL, N_EXPERTS), w_router_grp,
         jnp.zeros((DEPTH, D_MODEL, ROUTER_COLS - N_EXPERTS - MOE_GROUPS), F32)], axis=2)
    b_router = jnp.concatenate(
        [b_router_exp.reshape(DEPTH, N_EXPERTS), b_router_grp,
         jnp.zeros((DEPTH, ROUTER_COLS - N_EXPERTS - MOE_GROUPS), F32)], axis=1).reshape(DEPTH, 1, ROUTER_COLS)
    tail_w = (w_out, w_out_d, g_ffn3, w_router, b_router, w_gate, w_up, w_down, g_ple3, w_ple_gate, w_ple_proj)
    mix_w_p = (conv_a_w, cab3, lag3, lab3, lbg3, lbb3, sgu_w, sb_rep, conv_c_w, gq3, gk3)
    mix_w_s = (conv_a_w, cab3, lag3, lab3, lbg3, lbb3, sw8, sb_rep, conv_c_w, gq3, gk3)

    caches = [c.reshape(c.shape[0], c.shape[1], c.shape[2], 2 * GROUP_WIDTH)
              for c in (cache_kv_w128, cache_kv_w512, cache_kv_w2048)]
    p_p = p_prompt.reshape(DEPTH, rows_p, PLE_DIM)
    rows_t = n_p * TAIL
    rows_h = rows_s + rows_t
    p_h = jnp.concatenate([p_sample.reshape(DEPTH, rows_s, PLE_DIM),
                           p_prompt[:, :, s_p - TAIL:].reshape(DEPTH, rows_t, PLE_DIM)], axis=1)
    h_h = jnp.concatenate([x_sample.reshape(rows_s, D_MODEL),
                           x_prompt[:, s_p - TAIL:].reshape(rows_t, D_MODEL)], axis=0)

    h = x_prompt.reshape(rows_p, D_MODEL)
    st_a, st_c, st_kv = [], [], [[], [], []]
    sa, sc, sv, skv = [], [], [], [[], [], []]
    z_next = _inproj(h, g_mix3, w_in, 0, 1024, 1024)
    for i in range(DEPTH):
        z_h = _inproj(h_h, g_mix3, w_in, i, rows_h, 512, hp=True)
        z = _patch_tail(z_next.reshape(n_p, s_p, IN_COLS), z_h[rows_s:].reshape(n_p, TAIL, IN_COLS))
        (mix, q1, kv1, q4, kv4, q16, kv16, sta, stc, s1, s4, s16) = _mixers_prompt(z, i, mix_w_p)
        o1, l1 = _attn_prompt(q1, kv1)
        o4, l4 = _attn_prompt(q4, kv4)
        o16, l16 = _attn_prompt(q16, kv16)
        od = _combine_prompt(o1, l1, o4, l4, o16, l16)
        (mix_s, od_s, na, nc, cv, n1, n4, n16) = _mixers_sample(
            z_h[:rows_s].reshape(n_s, t_s, IN_COLS), i, state_conv_a, state_conv_c, caches, mix_w_s)
        mix_h = jnp.concatenate([mix_s.reshape(rows_s, ABC_PAD),
                                 mix[:, s_p - TAIL:].reshape(rows_t, ABC_PAD).astype(F32)], axis=0)
        od_h = jnp.concatenate([od_s.reshape(rows_s, OD_PAD),
                                od[:, s_p - TAIL:].reshape(rows_t, OD_PAD).astype(F32)], axis=0)
        h, z_next = _token_tail_prompt(h, mix.reshape(rows_p, ABC_PAD), od.reshape(rows_p, OD_PAD), p_p, i,
                                       tail_w, g_mix3, w_in)
        h_h = _token_tail_hp(h_h, mix_h, od_h, p_h, i, tail_w)
        st_a.append(sta[:, A_HALO - (A_CONV_LEN - 1):])
        st_c.append(stc[:, C_HALO - (C_CONV_LEN - 1):])
        for g, s_kv in enumerate((s1, s4, s16)):
            st_kv[g].append(s_kv.reshape(n_p, s_kv.shape[1], 2, HEADS_PER_GROUP, HEAD_DIM))
        sa.append(na)
        sc.append(nc)
        sv.append(cv)
        for g, nk in enumerate((n1, n4, n16)):
            skv[g].append(nk.reshape(n_s, nk.shape[1], 2, HEADS_PER_GROUP, HEAD_DIM))
    y_prompt = _patch_tail(h.reshape(n_p, s_p, D_MODEL), h_h[rows_s:].reshape(n_p, TAIL, D_MODEL))
    conv_a_prompt = jnp.stack(st_a)
    conv_c_prompt = jnp.stack(st_c)
    kv_prompt = [jnp.stack(s) for s in st_kv]
    y_sample = h_h[:rows_s].reshape(n_s, t_s, D_MODEL)
    conv_a_sample = jnp.stack(sa)
    conv_c_sample = jnp.stack(sc)
    chunk_v_sample = jnp.stack(sv)
    kv_sample = [jnp.stack(s) for s in skv]

    return (y_prompt, y_sample, conv_a_prompt, conv_a_sample, conv_c_prompt, conv_c_sample, chunk_v_sample,
            kv_prompt[0], kv_sample[0], kv_prompt[1], kv_sample[1], kv_prompt[2], kv_sample[2])
```

```python
import functools

import jax
import jax.numpy as jnp
from jax import lax
from jax.experimental import pallas as pl
from jax.experimental.pallas import tpu as pltpu

F32 = jnp.float32
BF16 = jnp.bfloat16

D_MODEL = 2048
DEPTH = 4
PLE_DIM = 256
HEAD_DIM = 64
A_WIDTH = 512
A_CONV_LEN = 31
B_WIDTH = 512
B_HEADS = 8
CHUNK = 128
C_WIDTH = 448
C_CONV_LEN = 3
ATTN_GROUPS = ((128, 1), (512, 4), (2048, 16))
HEADS_PER_GROUP = 3
WIN_KEYS = 128
D_HEADS = 9
D_WIDTH = D_HEADS * HEAD_DIM
GROUP_WIDTH = HEADS_PER_GROUP * HEAD_DIM
IN_COLS = 2 * A_WIDTH + 2 * B_WIDTH + 3 * C_WIDTH + 3 * D_WIDTH
COL_B = 2 * A_WIDTH
COL_C = COL_B + 2 * B_WIDTH
COL_D = COL_C + 3 * C_WIDTH
ATTN_SCALE = HEAD_DIM ** -0.5
MOE_GROUPS = 4
EXPERTS_PER_GROUP = 4
N_EXPERTS = 16
EXPERT_FF = 512
RMS_EPS = 1e-6
LN_EPS = 1e-5

LANE = 128
SUBLANE = 8
ABC_PAD = 1536
OD_PAD = 3 * 256
ROUTER_COLS = 128
VMEM_LIMIT = 56 * 1024 * 1024
NEG_INF = float("-inf")


def _cparams(sem):
    return pltpu.CompilerParams(dimension_semantics=sem, vmem_limit_bytes=VMEM_LIMIT)


def _rms_rows(x, g):
    return x * lax.rsqrt(jnp.mean(x * x, axis=-1, keepdims=True) + RMS_EPS) * g


def _layernorm_rows(x, g, b):
    mu = jnp.mean(x, axis=-1, keepdims=True)
    xc = x - mu
    var = jnp.mean(xc * xc, axis=-1, keepdims=True)
    return xc * lax.rsqrt(var + LN_EPS) * g + b


def _sigmoid(x):
    return 1.0 / (1.0 + jnp.exp(-x))


def _silu(x):
    return x * _sigmoid(x)


def _gelu(x):
    return 0.5 * x * (1.0 + lax.erf(x * (2.0 ** -0.5)))


def _split_bf16(x):
    hi = x.astype(BF16)
    return hi, (x - hi.astype(F32)).astype(BF16)


def _mm(x, w, hp):
    if not hp:
        return jnp.dot(x.astype(BF16), w.astype(BF16), preferred_element_type=F32)
    rows = x.shape[0]
    xh, xl = _split_bf16(x)
    wh, wl = _split_bf16(w)
    r = jnp.dot(jnp.concatenate([xh, xl], axis=0), wh, preferred_element_type=F32)
    return r[:rows] + r[rows:] + jnp.dot(xh, wl, preferred_element_type=F32)


def _act_dtype(hp):
    return F32 if hp else BF16


def _head_norm(x, g):
    outs = []
    for h in range(D_HEADS):
        xh = x[:, HEAD_DIM * h:HEAD_DIM * (h + 1)]
        outs.append(_rms_rows(xh, g))
    return outs


def _split_qkv(z_ref, rows):
    q_lo = (COL_D // LANE) * LANE
    zq = z_ref[0, rows, q_lo:q_lo + 640]
    q = zq[:, COL_D - q_lo:COL_D - q_lo + D_WIDTH]
    k_lo = COL_D + D_WIDTH
    zk = z_ref[0, rows, k_lo:k_lo + 640]
    k = zk[:, :D_WIDTH]
    v_lo = ((COL_D + 2 * D_WIDTH) // LANE) * LANE
    zv = z_ref[0, rows, v_lo:v_lo + 640]
    v = zv[:, COL_D + 2 * D_WIDTH - v_lo:]
    return q, k, v


def _split_c(z_ref, rows):
    zc = z_ref[0, rows, COL_C:COL_C + 1408]
    return zc[:, 0:C_WIDTH], zc[:, C_WIDTH:2 * C_WIDTH], zc[:, 2 * C_WIDTH:3 * C_WIDTH]


def _inproj_kernel(x_ref, g_ref, w_ref, o_ref, xn_ref, *, hp):
    @pl.when(pl.program_id(1) == 0)
    def _():
        xn_ref[...] = _rms_rows(x_ref[...], g_ref[...]).astype(xn_ref.dtype)

    o_ref[...] = _mm(xn_ref[...], w_ref[...], hp)


def _inproj(h, g_all, w_all, layer, tm, tn, hp=False):
    rows = h.shape[0]
    return pl.pallas_call(
        functools.partial(_inproj_kernel, hp=hp),
        grid=(rows // tm, IN_COLS // tn),
        in_specs=[
            pl.BlockSpec((tm, D_MODEL), lambda i, j: (i, 0)),
            pl.BlockSpec((None, 1, D_MODEL), lambda i, j: (layer, 0, 0)),
            pl.BlockSpec((None, D_MODEL, tn), lambda i, j: (layer, 0, j)),
        ],
        out_specs=pl.BlockSpec((tm, tn), lambda i, j: (i, j)),
        out_shape=jax.ShapeDtypeStruct((rows, IN_COLS), F32),
        scratch_shapes=[pltpu.VMEM((tm, D_MODEL), _act_dtype(hp))],
        compiler_params=_cparams(("arbitrary", "arbitrary")),
        name="inproj",
    )(h, g_all, w_all)


MIX_TT = 256
CONV_ROWS = 64
A_HALO = 32
C_HALO = 8


def _pair_weights(sw_ref, wp_ref):
    row = lax.broadcasted_iota(jnp.int32, (CHUNK, CHUNK), 0)
    col = lax.broadcasted_iota(jnp.int32, (CHUNK, CHUNK), 1)
    keep = col <= row
    for p in range(B_HEADS // 2):
        w0 = jnp.where(keep, sw_ref[2 * p], 0.0)
        w1 = jnp.where(keep, sw_ref[2 * p + 1], 0.0)
        wp_ref[p] = jnp.concatenate([w0, w1], axis=1).astype(BF16)


def _mixer_kernel(z_ref, caw_ref, cab_ref, lag_ref, lab_ref, lbg_ref, lbb_ref, sw_ref, sb_ref, ccw_ref,
                  gq_ref, gk_ref,
                  mix_ref, q1_ref, kv1_ref, q4_ref, kv4_ref, q16_ref, kv16_ref,
                  sta_ref, stc_ref, st1_ref, st4_ref, st16_ref,
                  abuf, cbuf, wp_ref, qs_ref, kvs_ref, ashift):
    t = pl.program_id(1)
    tt = MIX_TT

    @pl.when(t == 0)
    def _():
        abuf[0:A_HALO, :] = jnp.zeros((A_HALO, A_WIDTH), F32)
        cbuf[0:C_HALO, :] = jnp.zeros((C_HALO, C_WIDTH), F32)
        _pair_weights(sw_ref, wp_ref)

    @pl.when(t > 0)
    def _():
        abuf[0:A_HALO, :] = abuf[tt:tt + A_HALO, :]
        cbuf[0:C_HALO, :] = cbuf[tt:tt + C_HALO, :]

    za = z_ref[0, :, 0:2 * A_WIDTH]
    abuf[A_HALO:A_HALO + tt, :] = za[:, :A_WIDTH] * _sigmoid(za[:, A_WIDTH:])
    sta_ref[0] = abuf[tt:tt + A_HALO, :]
    base = A_HALO - (A_CONV_LEN - 1)
    span = tt + A_HALO - SUBLANE
    for s in range(1, SUBLANE):
        ashift[s - 1, 0:span, :] = abuf[s:s + span, :]

    def tap_rows(first):
        phase = first % SUBLANE
        src = abuf if phase == 0 else ashift.at[phase - 1]
        return src[first - phase:first - phase + CONV_ROWS, :]

    for r0 in range(0, tt, CONV_ROWS):
        acc = caw_ref[0:1, :] * tap_rows(r0 + base)
        for j in range(1, A_CONV_LEN):
            acc = acc + caw_ref[j:j + 1, :] * tap_rows(r0 + base + j)
        y = _layernorm_rows(acc + cab_ref[...], lag_ref[...], lab_ref[...])
        mix_ref[0, r0:r0 + CONV_ROWS, 0:A_WIDTH] = _silu(y).astype(mix_ref.dtype)

    lane = lax.broadcasted_iota(jnp.int32, (CHUNK, LANE), 1)
    for c0 in range(0, tt, CHUNK):
        gb = _gelu(z_ref[0, c0:c0 + CHUNK, COL_B:COL_B + 2 * B_WIDTH])
        u = gb[:, :B_WIDTH]
        v = _layernorm_rows(gb[:, B_WIDTH:], lbg_ref[...], lbb_ref[...])
        pieces = []
        for p in range(B_HEADS // 2):
            v128 = v[:, LANE * p:LANE * (p + 1)]
            rhs = jnp.concatenate([jnp.where(lane < HEAD_DIM, v128, 0.0),
                                   jnp.where(lane >= HEAD_DIM, v128, 0.0)], axis=0).astype(BF16)
            pieces.append(jnp.dot(wp_ref[p], rhs, preferred_element_type=F32))
        mixed = jnp.concatenate(pieces, axis=1) + sb_ref[...]
        mix_ref[0, c0:c0 + CHUNK, A_WIDTH:A_WIDTH + B_WIDTH] = (u * mixed).astype(mix_ref.dtype)

    g_b, g_c, x_c = _split_c(z_ref, slice(None))
    cbuf[C_HALO:C_HALO + tt, :] = g_c * x_c
    stc_ref[0] = cbuf[tt:tt + C_HALO, :]
    cbase = C_HALO - (C_CONV_LEN - 1)
    conv = ccw_ref[0:1, :] * cbuf[cbase:cbase + tt, :]
    for j in range(1, C_CONV_LEN):
        conv = conv + ccw_ref[j:j + 1, :] * cbuf[cbase + j:cbase + j + tt, :]
    o_c = jnp.concatenate([g_b * conv, jnp.zeros((tt, ABC_PAD - 2 * A_WIDTH - C_WIDTH), F32)], axis=1)
    mix_ref[0, :, 2 * A_WIDTH:ABC_PAD] = o_c.astype(mix_ref.dtype)

    q, k, v = _split_qkv(z_ref, slice(None))
    qn = _head_norm(q, gq_ref[...] * ATTN_SCALE)
    kn = _head_norm(k, gk_ref[...])
    outs = ((q1_ref, kv1_ref, st1_ref), (q4_ref, kv4_ref, st4_ref), (q16_ref, kv16_ref, st16_ref))
    for g, (window, dil) in enumerate(ATTN_GROUPS):
        q_ref, kv_ref, st_ref = outs[g]
        hs = slice(HEADS_PER_GROUP * g, HEADS_PER_GROUP * (g + 1))
        q_g = jnp.concatenate(qn[hs], axis=1)
        kv_g = jnp.concatenate(kn[hs] + [v[:, GROUP_WIDTH * g:GROUP_WIDTH * (g + 1)]], axis=1)
        keep = min(window, tt)
        st_ref[0] = kv_g[tt - keep:, :]
        if dil == 1:
            q_ref[0, 0] = q_g
            kv_ref[0, 0] = kv_g
        else:
            qs_ref[0] = q_g[:, :LANE]
            qs_ref[1] = jnp.concatenate([q_g[:, LANE:], jnp.zeros((tt, 2 * LANE - GROUP_WIDTH), F32)], axis=1)
            for i in range(3):
                kvs_ref[i] = kv_g[:, LANE * i:LANE * (i + 1)]
            for r in range(dil):
                rows = pl.ds(r, tt // dil, stride=dil)
                q_ref[0, r, :, 0:LANE] = qs_ref[0, rows, :]
                q_ref[0, r, :, LANE:GROUP_WIDTH] = qs_ref[1, rows, :][:, :GROUP_WIDTH - LANE]
                for i in range(3):
                    kv_ref[0, r, :, LANE * i:LANE * (i + 1)] = kvs_ref[i, rows, :]


def _mixers_prompt(z, layer, wts):
    (caw, cab, lag, lab, lbg, lbb, sw, sb_rep, ccw, gq, gk) = wts
    n, s, _ = z.shape
    tt = MIX_TT
    nt = s // tt

    def lw(shape):
        nd = len(shape)
        return pl.BlockSpec((None,) + shape, lambda b, t: (layer,) + (0,) * nd)

    in_specs = [
        pl.BlockSpec((1, tt, IN_COLS), lambda b, t: (b, t, 0)),
        lw((A_CONV_LEN, A_WIDTH)), lw((1, A_WIDTH)), lw((1, A_WIDTH)), lw((1, A_WIDTH)),
        lw((1, B_WIDTH)), lw((1, B_WIDTH)), lw((B_HEADS, CHUNK, CHUNK)), lw((CHUNK, B_WIDTH)),
        lw((C_CONV_LEN, C_WIDTH)), lw((1, HEAD_DIM)), lw((1, HEAD_DIM)),
    ]
    out_shape = [jax.ShapeDtypeStruct((n, s, ABC_PAD), BF16)]
    out_specs = [pl.BlockSpec((1, tt, ABC_PAD), lambda b, t: (b, t, 0))]
    for _, dil in ATTN_GROUPS:
        for width in (GROUP_WIDTH, 2 * GROUP_WIDTH):
            out_shape.append(jax.ShapeDtypeStruct((n, dil, s // dil, width), F32))
            out_specs.append(pl.BlockSpec((1, dil, tt // dil, width), lambda b, t: (b, 0, t, 0)))
    out_shape.append(jax.ShapeDtypeStruct((n, A_HALO, A_WIDTH), F32))
    out_specs.append(pl.BlockSpec((1, A_HALO, A_WIDTH), lambda b, t: (b, 0, 0)))
    out_shape.append(jax.ShapeDtypeStruct((n, C_HALO, C_WIDTH), F32))
    out_specs.append(pl.BlockSpec((1, C_HALO, C_WIDTH), lambda b, t: (b, 0, 0)))
    for window, _ in ATTN_GROUPS:
        keep = min(window, s)
        blk = min(keep, tt)
        first = (s - keep) // blk
        out_shape.append(jax.ShapeDtypeStruct((n, keep, 2 * GROUP_WIDTH), F32))
        if keep <= tt:
            out_specs.append(pl.BlockSpec((1, blk, 2 * GROUP_WIDTH), lambda b, t: (b, 0, 0)))
        else:
            out_specs.append(pl.BlockSpec((1, blk, 2 * GROUP_WIDTH),
                                          lambda b, t, first=first: (b, jnp.maximum(t - first, 0), 0)))
    return pl.pallas_call(
        _mixer_kernel,
        grid=(n, nt),
        in_specs=in_specs,
        out_specs=out_specs,
        out_shape=out_shape,
        scratch_shapes=[
            pltpu.VMEM((tt + A_HALO, A_WIDTH), F32),
            pltpu.VMEM((tt + C_HALO, C_WIDTH), F32),
            pltpu.VMEM((B_HEADS // 2, CHUNK, 2 * CHUNK), BF16),
            pltpu.VMEM((2, tt, LANE), F32),
            pltpu.VMEM((3, tt, LANE), F32),
            pltpu.VMEM((SUBLANE - 1, tt + A_HALO, A_WIDTH), F32),
        ],
        compiler_params=_cparams(("arbitrary", "arbitrary")),
        name="mixers_prompt",
    )(z, caw, cab, lag, lab, lbg, lbb, sw, sb_rep, ccw, gq, gk)


ATTN_QBLOCKS = 8


def _attn_kernel(q_ref, kvo_ref, kvp_ref, o_ref, l_ref, *, qblocks):
    c = pl.program_id(2)
    qi = lax.broadcasted_iota(jnp.int32, (WIN_KEYS, 2 * WIN_KEYS), 0)
    kj = lax.broadcasted_iota(jnp.int32, (WIN_KEYS, 2 * WIN_KEYS), 1)
    dist = qi + WIN_KEYS - kj
    band = (dist >= 0) & (dist <= WIN_KEYS)
    first_key = jnp.where(c > 0, 0, WIN_KEYS)
    for s in range(qblocks):
        rows = slice(WIN_KEYS * s, WIN_KEYS * (s + 1))
        q = q_ref[0, 0, rows, :]
        kvo = kvo_ref[0, 0, rows, :]
        if s == 0:
            kvp = kvp_ref[0, 0]
            mask = band & (kj >= first_key)
        else:
            kvp = kvo_ref[0, 0, WIN_KEYS * (s - 1):WIN_KEYS * s, :]
            mask = band
        o_parts, l_parts = [], []
        for h in range(HEADS_PER_GROUP):
            ks = slice(HEAD_DIM * h, HEAD_DIM * (h + 1))
            vs = slice(GROUP_WIDTH + HEAD_DIM * h, GROUP_WIDTH + HEAD_DIM * (h + 1))
            qh = q[:, ks].astype(BF16)
            kk = jnp.concatenate([kvp[:, ks], kvo[:, ks]], axis=0).astype(BF16)
            vv = jnp.concatenate([kvp[:, vs], kvo[:, vs]], axis=0).astype(BF16)
            sc = lax.dot_general(qh, kk, (((1,), (1,)), ((), ())), preferred_element_type=F32)
            sc = jnp.where(mask, sc, NEG_INF)
            m = jnp.max(sc, axis=-1, keepdims=True)
            ex = jnp.exp(sc - m)
            den = jnp.sum(ex, axis=-1, keepdims=True)
            probs = (ex / den).astype(BF16)
            o_parts.append(jnp.dot(probs, vv, preferred_element_type=F32))
            l_parts.append(jnp.broadcast_to(m + jnp.log(den), (WIN_KEYS, HEAD_DIM)))
        o_ref[0, 0, rows, :] = jnp.concatenate(o_parts, axis=1)
        l_ref[0, 0, rows, :] = jnp.concatenate(l_parts, axis=1)


def _attn_prompt(q, kv):
    n, dil, sub, _ = q.shape
    qblocks = min(ATTN_QBLOCKS, sub // WIN_KEYS)
    rows = qblocks * WIN_KEYS
    qspec = pl.BlockSpec((1, 1, rows, GROUP_WIDTH), lambda b, r, c: (b, r, c, 0))
    return pl.pallas_call(
        functools.partial(_attn_kernel, qblocks=qblocks),
        grid=(n, dil, sub // rows),
        in_specs=[
            qspec,
            pl.BlockSpec((1, 1, rows, 2 * GROUP_WIDTH), lambda b, r, c: (b, r, c, 0)),
            pl.BlockSpec((1, 1, WIN_KEYS, 2 * GROUP_WIDTH),
                         lambda b, r, c: (b, r, jnp.maximum(c * qblocks - 1, 0), 0)),
        ],
        out_specs=[qspec, qspec],
        out_shape=[jax.ShapeDtypeStruct(q.shape, F32), jax.ShapeDtypeStruct(q.shape, F32)],
        compiler_params=_cparams(("arbitrary", "arbitrary", "arbitrary")),
        name="attn_prompt",
    )(q, kv, kv)


def _combine_kernel(o1_ref, l1_ref, o4_ref, l4_ref, o16_ref, l16_ref, od_ref, s_o4, s_l4, s_o16, s_l16):
    tt = MIX_TT
    for dil, src, dst in ((4, o4_ref, s_o4), (4, l4_ref, s_l4), (16, o16_ref, s_o16), (16, l16_ref, s_l16)):
        for r in range(dil):
            x = src[0, r]
            rows = pl.ds(r, tt // dil, stride=dil)
            dst[0, rows, :] = x[:, :LANE]
            dst[1, rows, :] = jnp.concatenate(
                [x[:, LANE:], jnp.zeros((tt // dil, 2 * LANE - GROUP_WIDTH), F32)], axis=1)

    def whole(scr):
        return jnp.concatenate([scr[0], scr[1][:, :GROUP_WIDTH - LANE]], axis=1)

    outs = (o1_ref[0, 0], whole(s_o4), whole(s_o16))
    lses = (l1_ref[0, 0], whole(s_l4), whole(s_l16))
    mx = jnp.maximum(jnp.maximum(lses[0], lses[1]), lses[2])
    es = [jnp.exp(l - mx) for l in lses]
    den = es[0] + es[1] + es[2]
    pad = jnp.zeros((tt, OD_PAD // 3 - GROUP_WIDTH), F32)
    parts = []
    for g in range(3):
        parts += [outs[g] * (es[g] / den), pad]
    od_ref[0] = jnp.concatenate(parts, axis=1).astype(od_ref.dtype)


def _combine_prompt(o1, l1, o4, l4, o16, l16):
    n, _, s, _ = o1.shape
    tt = MIX_TT

    def spec(dil):
        return pl.BlockSpec((1, dil, tt // dil, GROUP_WIDTH), lambda b, t: (b, 0, t, 0))

    return pl.pallas_call(
        _combine_kernel,
        grid=(n, s // tt),
        in_specs=[spec(1), spec(1), spec(4), spec(4), spec(16), spec(16)],
        out_specs=pl.BlockSpec((1, tt, OD_PAD), lambda b, t: (b, t, 0)),
        out_shape=jax.ShapeDtypeStruct((n, s, OD_PAD), BF16),
        scratch_shapes=[pltpu.VMEM((2, tt, LANE), F32)] * 4,
        compiler_params=_cparams(("arbitrary", "arbitrary")),
        name="combine_prompt",
    )(o1, l1, o4, l4, o16, l16)


def _dec_kernel(z_ref, ha_ref, hc_ref, c1_ref, c4_ref, c16_ref,
                caw_ref, cab_ref, lag_ref, lab_ref, lbg_ref, lbb_ref, sw8_ref, sb_ref, ccw_ref, gq_ref, gk_ref,
                mix_ref, od_ref, na_ref, nc_ref, cv_ref, n1_ref, n4_ref, n16_ref,
                abuf, cbuf, kvbuf, exbuf):
    t_new = z_ref.shape[1]
    hist_a = A_CONV_LEN - 1
    hist_c = C_CONV_LEN - 1

    za = z_ref[0, :, 0:2 * A_WIDTH]
    abuf[0:hist_a, :] = ha_ref[0]
    abuf[hist_a:hist_a + t_new, :] = za[:, :A_WIDTH] * _sigmoid(za[:, A_WIDTH:])
    acc = caw_ref[0:1, :] * abuf[0:t_new, :]
    for j in range(1, A_CONV_LEN):
        acc = acc + caw_ref[j:j + 1, :] * abuf[j:j + t_new, :]
    y = _layernorm_rows(acc + cab_ref[...], lag_ref[...], lab_ref[...])
    mix_ref[0, :, 0:A_WIDTH] = _silu(y).astype(mix_ref.dtype)
    na_ref[0] = abuf[t_new:t_new + hist_a, :]

    gb = _gelu(z_ref[0, :, COL_B:COL_B + 2 * B_WIDTH])
    u = gb[:, :B_WIDTH]
    v = _layernorm_rows(gb[:, B_WIDTH:], lbg_ref[...], lbb_ref[...])
    cv_ref[0] = v
    row = lax.broadcasted_iota(jnp.int32, (t_new, B_WIDTH), 0)
    mixed = sb_ref[0:t_new, :]
    for s in range(t_new):
        mixed = mixed + jnp.where(row >= s, sw8_ref[s], 0.0) * v[s:s + 1, :]
    mix_ref[0, :, A_WIDTH:A_WIDTH + B_WIDTH] = (u * mixed).astype(mix_ref.dtype)

    g_b, g_c, x_c = _split_c(z_ref, slice(None))
    cbuf[0:hist_c, :] = hc_ref[0]
    cbuf[hist_c:hist_c + t_new, :] = g_c * x_c
    conv = ccw_ref[0:1, :] * cbuf[0:t_new, :]
    for j in range(1, C_CONV_LEN):
        conv = conv + ccw_ref[j:j + 1, :] * cbuf[j:j + t_new, :]
    o_c = jnp.concatenate([g_b * conv, jnp.zeros((t_new, ABC_PAD - 2 * A_WIDTH - C_WIDTH), F32)], axis=1)
    mix_ref[0, :, 2 * A_WIDTH:ABC_PAD] = o_c.astype(mix_ref.dtype)
    nc_ref[0] = cbuf[t_new:t_new + hist_c, :]

    q, k, v_d = _split_qkv(z_ref, slice(None))
    qn = _head_norm(q, gq_ref[...] * ATTN_SCALE)
    kn = _head_norm(k, gk_ref[...])
    caches = (c1_ref, c4_ref, c16_ref)
    news = (n1_ref, n4_ref, n16_ref)
    qrow = lax.broadcasted_iota(jnp.int32, (LANE, 2 * GROUP_WIDTH), 0)
    qlane = lax.broadcasted_iota(jnp.int32, (LANE, 2 * GROUP_WIDTH), 1)
    qmask = (qrow >> 3) == (qlane >> 6)
    zeros_q = jnp.zeros((t_new, GROUP_WIDTH), F32)
    lses, dens, offs = [], [], []
    off = 0
    for g, (window, dil) in enumerate(ATTN_GROUPS):
        hs = slice(HEADS_PER_GROUP * g, HEADS_PER_GROUP * (g + 1))
        buf_len = caches[g].shape[1]
        rows = buf_len + t_new
        kv_new = jnp.concatenate(kn[hs] + [v_d[:, GROUP_WIDTH * g:GROUP_WIDTH * (g + 1)]], axis=1)
        kvbuf[off:off + buf_len, :] = caches[g][0]
        kvbuf[off + buf_len:off + rows, :] = kv_new
        news[g][0] = kvbuf[off + t_new:off + rows, :]
        q_g = jnp.concatenate(qn[hs] + [zeros_q], axis=1)
        q_rep = jnp.concatenate([q_g] * HEADS_PER_GROUP
                                + [jnp.zeros((LANE - HEADS_PER_GROUP * t_new, 2 * GROUP_WIDTH), F32)], axis=0)
        q_hi, q_lo = _split_bf16(jnp.where(qmask, q_rep, 0.0))
        kv_hi, kv_lo = _split_bf16(kvbuf[off:off + rows, :])
        nt_dims = (((1,), (1,)), ((), ()))
        sc = (lax.dot_general(kv_hi, q_hi, nt_dims, preferred_element_type=F32)
              + lax.dot_general(kv_lo, q_hi, nt_dims, preferred_element_type=F32)
              + lax.dot_general(kv_hi, q_lo, nt_dims, preferred_element_type=F32))
        krow = lax.broadcasted_iota(jnp.int32, (rows, LANE), 0)
        tok = lax.broadcasted_iota(jnp.int32, (rows, LANE), 1) & (t_new - 1)
        dist = buf_len + tok - krow
        valid = (dist >= 0) & (dist <= dil * WIN_KEYS) & ((dist & (dil - 1)) == 0)
        sc = jnp.where(valid, sc, NEG_INF)
        m = jnp.max(sc, axis=0, keepdims=True)
        ex = jnp.exp(sc - m)
        den = jnp.sum(ex, axis=0, keepdims=True)
        exbuf[off:off + rows, :] = ex
        lses.append(m + jnp.log(den))
        dens.append(den)
        offs.append((off, rows))
        off += rows
    mx = jnp.maximum(jnp.maximum(lses[0], lses[1]), lses[2])
    es = [jnp.exp(l - mx) for l in lses]
    tot = es[0] + es[1] + es[2]
    lane = lax.broadcasted_iota(jnp.int32, (t_new, 2 * GROUP_WIDTH), 1)
    pad = jnp.zeros((t_new, OD_PAD // 3 - GROUP_WIDTH), F32)
    parts = []
    for g in range(3):
        off, rows = offs[g]
        coef = es[g] / (tot * dens[g])
        probs = (exbuf[off:off + rows, :] * coef).astype(BF16)
        kv_all = kvbuf[off:off + rows, :].astype(BF16)
        o_t = lax.dot_general(probs, kv_all, (((0,), (0,)), ((), ())), preferred_element_type=F32)
        o_g = jnp.zeros((t_new, 2 * GROUP_WIDTH), F32)
        for h in range(HEADS_PER_GROUP):
            sel = (lane >= GROUP_WIDTH + HEAD_DIM * h) & (lane < GROUP_WIDTH + HEAD_DIM * (h + 1))
            o_g = o_g + jnp.where(sel, o_t[t_new * h:t_new * (h + 1), :], 0.0)
        parts += [o_g[:, GROUP_WIDTH:], pad]
    od_ref[0] = jnp.concatenate(parts, axis=1).astype(od_ref.dtype)


def _mixers_sample(z, layer, hist_a, hist_c, caches, wts):
    (caw, cab, lag, lab, lbg, lbb, sw8, sb_rep, ccw, gq, gk) = wts
    n, t_new, _ = z.shape

    def lw(shape):
        nd = len(shape)
        return pl.BlockSpec((None,) + shape, lambda b: (layer,) + (0,) * nd)

    def st(shape):
        nd = len(shape)
        return pl.BlockSpec((None, 1) + shape, lambda b: (layer, b) + (0,) * nd)

    lens = [c.shape[2] for c in caches]
    total_rows = sum(lens) + 3 * t_new
    in_specs = [
        pl.BlockSpec((1, t_new, IN_COLS), lambda b: (b, 0, 0)),
        st((A_CONV_LEN - 1, A_WIDTH)), st((C_CONV_LEN - 1, C_WIDTH)),
        st((lens[0], 2 * GROUP_WIDTH)), st((lens[1], 2 * GROUP_WIDTH)), st((lens[2], 2 * GROUP_WIDTH)),
        lw((A_CONV_LEN, A_WIDTH)), lw((1, A_WIDTH)), lw((1, A_WIDTH)), lw((1, A_WIDTH)),
        lw((1, B_WIDTH)), lw((1, B_WIDTH)), lw((t_new, t_new, B_WIDTH)), lw((CHUNK, B_WIDTH)),
        lw((C_CONV_LEN, C_WIDTH)), lw((1, HEAD_DIM)), lw((1, HEAD_DIM)),
    ]

    def ob(shape):
        nd = len(shape)
        return pl.BlockSpec((1,) + shape, lambda b: (b,) + (0,) * nd)

    out_shape = [
        jax.ShapeDtypeStruct((n, t_new, ABC_PAD), F32),
        jax.ShapeDtypeStruct((n, t_new, OD_PAD), F32),
        jax.ShapeDtypeStruct((n, A_CONV_LEN - 1, A_WIDTH), F32),
        jax.ShapeDtypeStruct((n, C_CONV_LEN - 1, C_WIDTH), F32),
        jax.ShapeDtypeStruct((n, t_new, B_WIDTH), F32),
    ] + [jax.ShapeDtypeStruct((n, ln, 2 * GROUP_WIDTH), F32) for ln in lens]
    out_specs = [ob(s.shape[1:]) for s in out_shape]
    return pl.pallas_call(
        _dec_kernel,
        grid=(n,),
        in_specs=in_specs,
        out_specs=out_specs,
        out_shape=out_shape,
        scratch_shapes=[
            pltpu.VMEM((A_CONV_LEN - 1 + t_new + 2, A_WIDTH), F32),
            pltpu.VMEM((16, C_WIDTH), F32),
            pltpu.VMEM((total_rows, 2 * GROUP_WIDTH), F32),
            pltpu.VMEM((total_rows, LANE), F32),
        ],
        compiler_params=_cparams(("arbitrary",)),
        name="mixers_sample",
    )(z, hist_a, hist_c, *caches, caw, cab, lag, lab, lbg, lbb, sw8, sb_rep, ccw, gq, gk)


def _outproj_kernel(h_ref, mix_ref, od_ref, w_ref, wd_ref, o_ref, *, hp):
    acc = _mm(mix_ref[...], w_ref[0:ABC_PAD, :], hp) + _mm(od_ref[...], wd_ref[...], hp)
    o_ref[...] = h_ref[...] + acc


def _outproj(h, mix, od, w_all, wd_all, layer, tm, tn, hp):
    rows = h.shape[0]
    return pl.pallas_call(
        functools.partial(_outproj_kernel, hp=hp),
        grid=(rows // tm, D_MODEL // tn),
        in_specs=[
            pl.BlockSpec((tm, tn), lambda i, j: (i, j)),
            pl.BlockSpec((tm, ABC_PAD), lambda i, j: (i, 0)),
            pl.BlockSpec((tm, OD_PAD), lambda i, j: (i, 0)),
            pl.BlockSpec((None, D_MODEL, tn), lambda i, j: (layer, 0, j)),
            pl.BlockSpec((None, OD_PAD, tn), lambda i, j: (layer, 0, j)),
        ],
        out_specs=pl.BlockSpec((tm, tn), lambda i, j: (i, j)),
        out_shape=jax.ShapeDtypeStruct((rows, D_MODEL), F32),
        compiler_params=_cparams(("arbitrary", "arbitrary")),
        name="outproj",
    )(h, mix, od, w_all, wd_all)


def _top2_in_top_group(logits):
    rows = logits.shape[0]
    lane = lax.broadcasted_iota(jnp.int32, (rows, ROUTER_COLS), 1)
    big = jnp.int32(ROUTER_COLS)
    is_grp = (lane >= N_EXPERTS) & (lane < N_EXPERTS + MOE_GROUPS)
    gl = jnp.where(is_grp, logits, NEG_INF)
    gmax = jnp.max(gl, axis=-1, keepdims=True)
    gidx = jnp.min(jnp.where(gl == gmax, lane - N_EXPERTS, big), axis=-1, keepdims=True)
    gate = 1.0 / jnp.sum(jnp.where(is_grp, jnp.exp(gl - gmax), 0.0), axis=-1, keepdims=True)
    in_grp = (lane < N_EXPERTS) & ((lane >> 2) == gidx)
    el = jnp.where(in_grp, logits, NEG_INF)
    t1 = jnp.max(el, axis=-1, keepdims=True)
    i1 = jnp.min(jnp.where(el == t1, lane, big), axis=-1, keepdims=True)
    el2 = jnp.where(lane == i1, NEG_INF, el)
    t2 = jnp.max(el2, axis=-1, keepdims=True)
    i2 = jnp.min(jnp.where(el2 == t2, lane, big), axis=-1, keepdims=True)
    e2 = jnp.exp(t2 - t1)
    return lane, i1, i2, gate / (1.0 + e2), gate * e2 / (1.0 + e2)


META_E1, META_E2, META_R1, META_R2, META_W1, META_W2 = range(6)


def _outrouter_kernel(h_ref, mix_ref, od_ref, g_ref, wr_ref, br_ref, wout_hbm, woutd_hbm,
                      h_out, m_ref, meta_ref, meta_t_ref, cnt_ref,
                      wo_b, wod_b, stage, sem, carry_ref, *, layer):
    @pl.when(pl.program_id(0) == 0)
    def _():
        carry_ref[...] = jnp.zeros_like(carry_ref)
        _stream_cast(wout_hbm, layer, wo_b, stage, sem)
        _stream_cast(woutd_hbm, layer, wod_b, stage, sem)

    acc = (jnp.dot(mix_ref[...], wo_b[...], preferred_element_type=F32)
           + jnp.dot(od_ref[...], wod_b[...], preferred_element_type=F32))
    h1 = h_ref[...] + acc
    h_out[...] = h1
    m = _rms_rows(h1, g_ref[...])
    m_ref[...] = m
    logits = _mm(m, wr_ref[...], False) + br_ref[...]
    rows = logits.shape[0]
    lane, i1, i2, w1, w2 = _top2_in_top_group(logits)
    sel = jnp.where((lane == i1) | (lane == i2), 1.0, 0.0)
    r = lax.broadcasted_iota(jnp.int32, (rows, rows), 0)
    c = lax.broadcasted_iota(jnp.int32, (rows, rows), 1)
    earlier = jnp.where(c < r, 1.0, 0.0).astype(BF16)
    rank = jnp.dot(earlier, sel.astype(BF16), preferred_element_type=F32) + carry_ref[...]
    r1 = jnp.sum(jnp.where(lane == i1, rank, 0.0), axis=-1, keepdims=True)
    r2 = jnp.sum(jnp.where(lane == i2, rank, 0.0), axis=-1, keepdims=True)
    carry_ref[...] += jnp.sum(sel, axis=0, keepdims=True)
    cnt_ref[...] = carry_ref[...]
    meta = jnp.zeros((rows, ROUTER_COLS), F32)
    for pos, val in ((META_E1, i1.astype(F32)), (META_E2, i2.astype(F32)), (META_R1, r1), (META_R2, r2),
                     (META_W1, w1), (META_W2, w2)):
        meta = jnp.where(lane == pos, val, meta)
    meta_ref[...] = meta
    meta_t_ref[...] = meta.T[:SUBLANE, :]


OUTROUTER_TM = 512


def _outrouter(h, mix, od, g_all, wr_all, br_all, w_out, w_out_d, layer):
    rows = h.shape[0]
    tm = OUTROUTER_TM
    row_spec = pl.BlockSpec((tm, D_MODEL), lambda i: (i, 0))
    return pl.pallas_call(
        functools.partial(_outrouter_kernel, layer=layer),
        grid=(rows // tm,),
        in_specs=[
            row_spec,
            pl.BlockSpec((tm, ABC_PAD), lambda i: (i, 0)),
            pl.BlockSpec((tm, OD_PAD), lambda i: (i, 0)),
            pl.BlockSpec((None, 1, D_MODEL), lambda i: (layer, 0, 0)),
            pl.BlockSpec((None, D_MODEL, ROUTER_COLS), lambda i: (layer, 0, 0)),
            pl.BlockSpec((None, 1, ROUTER_COLS), lambda i: (layer, 0, 0)),
            pl.BlockSpec(memory_space=pl.ANY),
            pl.BlockSpec(memory_space=pl.ANY),
        ],
        out_specs=[row_spec, row_spec,
                   pl.BlockSpec((tm, ROUTER_COLS), lambda i: (i, 0)),
                   pl.BlockSpec((SUBLANE, tm), lambda i: (0, i)),
                   pl.BlockSpec((1, ROUTER_COLS), lambda i: (0, 0))],
        out_shape=[jax.ShapeDtypeStruct((rows, D_MODEL), F32),
                   jax.ShapeDtypeStruct((rows, D_MODEL), F32),
                   jax.ShapeDtypeStruct((rows, ROUTER_COLS), F32),
                   jax.ShapeDtypeStruct((SUBLANE, rows), F32),
                   jax.ShapeDtypeStruct((1, ROUTER_COLS), F32)],
        scratch_shapes=[pltpu.VMEM((ABC_PAD, D_MODEL), BF16),
                        pltpu.VMEM((OD_PAD, D_MODEL), BF16),
                        pltpu.VMEM((2, D_MODEL, W_CHUNK), F32),
                        pltpu.SemaphoreType.DMA((2,)),
                        pltpu.VMEM((1, ROUTER_COLS), F32)],
        compiler_params=_cparams(("arbitrary",)),
        name="outrouter",
    )(h, mix, od, g_all, wr_all, br_all, w_out, w_out_d)


EXPERT_TILE = 256


def _experts_kernel(te_ref, tv_ref, s0_ref, s1_ref, nu_ref,
                    m_hbm, wg_ref, wu_ref, wd_ref, yk_hbm,
                    gsrc_ref, ssrc_ref, xbuf, ybuf, wgb, wub, wdb, gsem, ssem, *, n_tok):
    i = pl.program_id(0)
    tile = EXPERT_TILE
    nused = nu_ref[0]
    slot = lax.rem(i, 2)
    plane = n_tok + 2 * SUBLANE

    def rows_moved(t):
        return pl.multiple_of(((tv_ref[t] + SUBLANE - 1) // SUBLANE) * SUBLANE, SUBLANE)

    def gather_copy(s, j, b):
        return pltpu.make_async_copy(m_hbm.at[pl.ds(gsrc_ref[s], 1), :], xbuf.at[b, pl.ds(j, 1), :], gsem.at[b])

    def scatter_copy(s, j, b):
        return pltpu.make_async_copy(ybuf.at[b, pl.ds(j, 1), :], yk_hbm.at[pl.ds(ssrc_ref[s], 1), :], ssem.at[b])

    def start_rows(copy_fn, t, b):
        def body(j8, carry):
            for u in range(SUBLANE):
                j = j8 * SUBLANE + u
                copy_fn(t * tile + j, j, b).start()
            return carry
        lax.fori_loop(0, rows_moved(t) // SUBLANE, body, 0)

    def wait_gather(t, b):
        n = rows_moved(t)
        pltpu.make_async_copy(m_hbm.at[pl.ds(0, n), :], xbuf.at[b, pl.ds(0, n), :], gsem.at[b]).wait()

    def wait_scatter(t, b):
        n = rows_moved(t)
        pltpu.make_async_copy(ybuf.at[b, pl.ds(0, n), :], yk_hbm.at[pl.ds(0, n), :], ssem.at[b]).wait()

    @pl.when(i == 0)
    def _():
        def fill(t8, carry):
            for u in range(SUBLANE):
                t = t8 * SUBLANE + u
                gsrc_ref[s0_ref[t]] = t
                gsrc_ref[s1_ref[t]] = t
                ssrc_ref[s0_ref[t]] = t
                ssrc_ref[s1_ref[t]] = plane + t
            return carry
        lax.fori_loop(0, n_tok // SUBLANE, fill, 0)

        def fill_pad(t, carry):
            def one(j, c):
                gsrc_ref[t * tile + j] = 0
                ssrc_ref[t * tile + j] = n_tok + SUBLANE * lax.rem(t, 2) + lax.rem(j, SUBLANE)
                return c
            lax.fori_loop(tv_ref[t], rows_moved(t), one, 0)
            return carry
        lax.fori_loop(0, nused, fill_pad, 0)
        xbuf[...] = jnp.zeros_like(xbuf)
        for k in range(2):
            spare = pltpu.make_async_copy(xbuf.at[0, pl.ds(0, 2 * SUBLANE), :],
                                          yk_hbm.at[pl.ds(k * plane + n_tok, 2 * SUBLANE), :], ssem.at[0])
            spare.start()
            spare.wait()
        start_rows(gather_copy, 0, 0)

    @pl.when(i < nused)
    def _():
        @pl.when(i + 1 < nused)
        def _():
            start_rows(gather_copy, i + 1, 1 - slot)

        changed = jnp.logical_or(i == 0, te_ref[i] != te_ref[jnp.maximum(i - 1, 0)])

        @pl.when(changed)
        def _():
            wgb[...] = wg_ref[...].astype(BF16)
            wub[...] = wu_ref[...].astype(BF16)
            wdb[...] = wd_ref[...].astype(BF16)

        wait_gather(i, slot)

        @pl.when(i >= 2)
        def _():
            wait_scatter(i - 2, slot)

        x = xbuf[slot].astype(BF16)
        gate = jnp.dot(x, wgb[...], preferred_element_type=F32)
        up = jnp.dot(x, wub[...], preferred_element_type=F32)
        ybuf[slot] = jnp.dot((_silu(gate) * up).astype(BF16), wdb[...], preferred_element_type=F32)
        start_rows(scatter_copy, i, slot)

        @pl.when(i == nused - 1)
        def _():
            @pl.when(i >= 1)
            def _():
                wait_scatter(i - 1, 1 - slot)
            wait_scatter(i, slot)


def _experts_sparse(m, tile_expert, tile_rows, slot0, slot1, nused, wg_all, wu_all, wd_all, layer):
    n_tok = m.shape[0]
    n_tiles = tile_expert.shape[0]

    def wspec(shape):
        return pl.BlockSpec((None, None) + shape, lambda i, te, tv, s0, s1, nu: (layer, te[i], 0, 0))

    grid_spec = pltpu.PrefetchScalarGridSpec(
        num_scalar_prefetch=5,
        grid=(n_tiles,),
        in_specs=[pl.BlockSpec(memory_space=pl.ANY),
                  wspec((D_MODEL, EXPERT_FF)), wspec((D_MODEL, EXPERT_FF)), wspec((EXPERT_FF, D_MODEL))],
        out_specs=pl.BlockSpec(memory_space=pl.ANY),
        scratch_shapes=[
            pltpu.SMEM((n_tiles * EXPERT_TILE,), jnp.int32),
            pltpu.SMEM((n_tiles * EXPERT_TILE,), jnp.int32),
            pltpu.VMEM((2, EXPERT_TILE, D_MODEL), F32),
            pltpu.VMEM((2, EXPERT_TILE, D_MODEL), F32),
            pltpu.VMEM((D_MODEL, EXPERT_FF), BF16),
            pltpu.VMEM((D_MODEL, EXPERT_FF), BF16),
            pltpu.VMEM((EXPERT_FF, D_MODEL), BF16),
            pltpu.SemaphoreType.DMA((2,)),
            pltpu.SemaphoreType.DMA((2,)),
        ],
    )
    return pl.pallas_call(
        functools.partial(_experts_kernel, n_tok=n_tok),
        grid_spec=grid_spec,
        out_shape=jax.ShapeDtypeStruct((2 * (n_tok + 2 * SUBLANE), D_MODEL), F32),
        compiler_params=_cparams(("arbitrary",)),
        name="experts_sparse",
    )(tile_expert, tile_rows, slot0, slot1, nused, m, wg_all, wu_all, wd_all)


def _expert_plan(meta_t, cnt, n_tiles):
    tile = EXPERT_TILE
    counts = cnt[0, :N_EXPERTS].astype(jnp.int32)
    padded = ((counts + tile - 1) // tile) * tile
    ends = jnp.cumsum(padded)
    base = ends - padded
    e1 = meta_t[META_E1].astype(jnp.int32)
    e2 = meta_t[META_E2].astype(jnp.int32)
    slot0 = base[e1] + meta_t[META_R1].astype(jnp.int32)
    slot1 = base[e2] + meta_t[META_R2].astype(jnp.int32)
    start = jnp.arange(n_tiles, dtype=jnp.int32) * tile
    expert_of = jnp.sum((start[:, None] >= ends[None, :]).astype(jnp.int32), axis=1)
    last_used = jnp.max(jnp.where(counts > 0, jnp.arange(N_EXPERTS, dtype=jnp.int32), 0))
    tile_expert = jnp.minimum(expert_of, last_used)
    e_clamped = jnp.minimum(expert_of, N_EXPERTS - 1)
    tile_rows = jnp.clip(counts[e_clamped] - (start - base[e_clamped]), 0, tile)
    tile_rows = jnp.where(expert_of < N_EXPERTS, tile_rows, 0).astype(jnp.int32)
    nused = (ends[-1] // tile).astype(jnp.int32).reshape(1)
    return tile_expert, tile_rows, slot0, slot1, nused


def _router_kernel(h_ref, g_ref, wr_ref, br_ref, m_ref, comb_ref, *, hp):
    m = _rms_rows(h_ref[...], g_ref[...]).astype(m_ref.dtype)
    m_ref[...] = m
    logits = _mm(m, wr_ref[...], hp) + br_ref[...]
    rows = logits.shape[0]
    lane = lax.broadcasted_iota(jnp.int32, (rows, ROUTER_COLS), 1)
    big = jnp.int32(ROUTER_COLS)
    is_grp = (lane >= N_EXPERTS) & (lane < N_EXPERTS + MOE_GROUPS)
    gl = jnp.where(is_grp, logits, NEG_INF)
    gmax = jnp.max(gl, axis=-1, keepdims=True)
    gidx = jnp.min(jnp.where(gl == gmax, lane - N_EXPERTS, big), axis=-1, keepdims=True)
    gate = 1.0 / jnp.sum(jnp.where(is_grp, jnp.exp(gl - gmax), 0.0), axis=-1, keepdims=True)
    in_grp = (lane < N_EXPERTS) & ((lane >> 2) == gidx)
    el = jnp.where(in_grp, logits, NEG_INF)
    t1 = jnp.max(el, axis=-1, keepdims=True)
    i1 = jnp.min(jnp.where(el == t1, lane, big), axis=-1, keepdims=True)
    el2 = jnp.where(lane == i1, NEG_INF, el)
    t2 = jnp.max(el2, axis=-1, keepdims=True)
    i2 = jnp.min(jnp.where(el2 == t2, lane, big), axis=-1, keepdims=True)
    e2 = jnp.exp(t2 - t1)
    w1 = gate / (1.0 + e2)
    w2 = gate * e2 / (1.0 + e2)
    comb_ref[...] = jnp.where(lane == i1, w1, 0.0) + jnp.where(lane == i2, w2, 0.0)


def _router(h, g_all, wr_all, br_all, layer, tm, hp):
    rows = h.shape[0]
    return pl.pallas_call(
        functools.partial(_router_kernel, hp=hp),
        grid=(rows // tm,),
        in_specs=[
            pl.BlockSpec((tm, D_MODEL), lambda i: (i, 0)),
            pl.BlockSpec((None, 1, D_MODEL), lambda i: (layer, 0, 0)),
            pl.BlockSpec((None, D_MODEL, ROUTER_COLS), lambda i: (layer, 0, 0)),
            pl.BlockSpec((None, 1, ROUTER_COLS), lambda i: (layer, 0, 0)),
        ],
        out_specs=[pl.BlockSpec((tm, D_MODEL), lambda i: (i, 0)),
                   pl.BlockSpec((tm, ROUTER_COLS), lambda i: (i, 0))],
        out_shape=[jax.ShapeDtypeStruct((rows, D_MODEL), _act_dtype(hp)),
                   jax.ShapeDtypeStruct((rows, ROUTER_COLS), F32)],
        compiler_params=_cparams(("arbitrary",)),
        name="router",
    )(h, g_all, wr_all, br_all)


def _moe_kernel(h_ref, m_ref, comb_ref, wg_ref, wu_ref, wd_ref, o_ref, *, hp):
    e = pl.program_id(1)

    @pl.when(e == 0)
    def _():
        o_ref[...] = h_ref[...]

    x = m_ref[...]
    gate = _mm(x, wg_ref[...], hp)
    up = _mm(x, wu_ref[...], hp)
    lane = lax.broadcasted_iota(jnp.int32, comb_ref.shape, 1)
    w = jnp.sum(jnp.where(lane == e, comb_ref[...], 0.0), axis=-1, keepdims=True)
    o_ref[...] += _mm(_silu(gate) * up * w, wd_ref[...], hp)


def _moe_dense(h, m, comb, wg_all, wu_all, wd_all, layer, tm, hp):
    rows = h.shape[0]
    return pl.pallas_call(
        functools.partial(_moe_kernel, hp=hp),
        grid=(rows // tm, N_EXPERTS),
        in_specs=[
            pl.BlockSpec((tm, D_MODEL), lambda i, e: (i, 0)),
            pl.BlockSpec((tm, D_MODEL), lambda i, e: (i, 0)),
            pl.BlockSpec((tm, ROUTER_COLS), lambda i, e: (i, 0)),
            pl.BlockSpec((None, None, D_MODEL, EXPERT_FF), lambda i, e: (layer, e, 0, 0)),
            pl.BlockSpec((None, None, D_MODEL, EXPERT_FF), lambda i, e: (layer, e, 0, 0)),
            pl.BlockSpec((None, None, EXPERT_FF, D_MODEL), lambda i, e: (layer, e, 0, 0)),
        ],
        out_specs=pl.BlockSpec((tm, D_MODEL), lambda i, e: (i, 0)),
        out_shape=jax.ShapeDtypeStruct((rows, D_MODEL), F32),
        compiler_params=_cparams(("arbitrary", "arbitrary")),
        name="moe_dense",
    )(h, m, comb, wg_all, wu_all, wd_all)


def _ple_kernel(h_ref, hc_ref, g_ref, p_ref, wg_ref, wp_ref, o_ref, xn_ref, *, hp):
    @pl.when(pl.program_id(1) == 0)
    def _():
        xn_ref[...] = _rms_rows(h_ref[...], g_ref[...]).astype(xn_ref.dtype)

    gate = _sigmoid(_mm(xn_ref[...], wg_ref[...], hp))
    o_ref[...] = hc_ref[...] + gate * _mm(p_ref[...], wp_ref[...], hp)


def _ple(h, p_all, g_all, wg_all, wp_all, layer, tm, tn, hp):
    rows = h.shape[0]
    return pl.pallas_call(
        functools.partial(_ple_kernel, hp=hp),
        grid=(rows // tm, D_MODEL // tn),
        in_specs=[
            pl.BlockSpec((tm, D_MODEL), lambda i, j: (i, 0)),
            pl.BlockSpec((tm, tn), lambda i, j: (i, j)),
            pl.BlockSpec((None, 1, D_MODEL), lambda i, j: (layer, 0, 0)),
            pl.BlockSpec((None, tm, PLE_DIM), lambda i, j: (layer, i, 0)),
            pl.BlockSpec((None, D_MODEL, tn), lambda i, j: (layer, 0, j)),
            pl.BlockSpec((None, PLE_DIM, tn), lambda i, j: (layer, 0, j)),
        ],
        out_specs=pl.BlockSpec((tm, tn), lambda i, j: (i, j)),
        out_shape=jax.ShapeDtypeStruct((rows, D_MODEL), F32),
        scratch_shapes=[pltpu.VMEM((tm, D_MODEL), _act_dtype(hp))],
        compiler_params=_cparams(("arbitrary", "arbitrary")),
        name="ple",
    )(h, h, g_all, p_all, wg_all, wp_all)


def _token_tail_hp(h, mix, od, p_all, layer, tw):
    (w_out, w_out_d, g_ffn, w_router, b_router, w_gate, w_up, w_down, g_ple, w_ple_gate, w_ple_proj) = tw
    tm = h.shape[0]
    h = _outproj(h, mix, od, w_out, w_out_d, layer, tm, 512, True)
    m, comb = _router(h, g_ffn, w_router, b_router, layer, tm, True)
    h = _moe_dense(h, m, comb, w_gate, w_up, w_down, layer, tm, True)
    return _ple(h, p_all, g_ple, w_ple_gate, w_ple_proj, layer, tm, 512, True)


POST_TM = 128
W_CHUNK = 256
Z_PARTS = 1
Z_CHUNK = 1280
POST_VMEM_LIMIT = 60 * 1024 * 1024


def _stream_cast(w_hbm, layer, dst, stage, sem, col0=0):
    k_rows = dst.shape[0]
    n_chunks = dst.shape[1] // W_CHUNK

    def chunk_copy(c):
        return pltpu.make_async_copy(w_hbm.at[layer, pl.ds(0, k_rows), pl.ds(col0 + c * W_CHUNK, W_CHUNK)],
                                     stage.at[c % 2, pl.ds(0, k_rows), :], sem.at[c % 2])

    chunk_copy(0).start()
    for c in range(n_chunks):
        if c + 1 < n_chunks:
            chunk_copy(c + 1).start()
        chunk_copy(c).wait()
        dst[:, c * W_CHUNK:(c + 1) * W_CHUNK] = stage[c % 2, 0:k_rows, :].astype(BF16)


def _post_kernel(*refs, layer, with_inproj):
    if with_inproj:
        (h_ref, y0_ref, y1_ref, meta_ref, p_ref, gp_ref, wpp_ref, wpg_hbm, gm_ref, win_hbm,
         h_out, z_out, wpg_b, stage, sem, win_b, a_scr) = refs
    else:
        (h_ref, y0_ref, y1_ref, meta_ref, p_ref, gp_ref, wpp_ref, wpg_hbm,
         h_out, wpg_b, stage, sem) = refs
    part = pl.program_id(1)
    part_cols = IN_COLS // Z_PARTS

    @pl.when(jnp.logical_and(pl.program_id(0) == 0, part == 0))
    def _():
        _stream_cast(wpg_hbm, layer, wpg_b, stage, sem)
        if with_inproj:
            for k in range(Z_PARTS):
                _stream_cast(win_hbm, layer + 1, win_b.at[k], stage, sem, col0=k * part_cols)

    @pl.when(part == 0)
    def _():
        meta = meta_ref[...]
        lane = lax.broadcasted_iota(jnp.int32, meta.shape, 1)
        w1 = jnp.sum(jnp.where(lane == META_W1, meta, 0.0), axis=-1, keepdims=True)
        w2 = jnp.sum(jnp.where(lane == META_W2, meta, 0.0), axis=-1, keepdims=True)
        hn = h_ref[...] + w1 * y0_ref[...] + w2 * y1_ref[...]
        xn = _rms_rows(hn, gp_ref[...]).astype(BF16)
        gate = _sigmoid(jnp.dot(xn, wpg_b[...], preferred_element_type=F32))
        h3 = hn + gate * _mm(p_ref[...], wpp_ref[...], False)
        h_out[...] = h3
        if with_inproj:
            a_scr[...] = _rms_rows(h3, gm_ref[...]).astype(BF16)

    if with_inproj:
        for c0 in range(0, part_cols, Z_CHUNK):
            z_out[:, c0:c0 + Z_CHUNK] = jnp.dot(a_scr[...], win_b[part, :, c0:c0 + Z_CHUNK],
                                                preferred_element_type=F32)


def _post(h, yk, meta, p_all, g_ple, w_ple_gate, w_ple_proj, g_mix, w_in, layer):
    rows = h.shape[0]
    tm = POST_TM
    with_inproj = layer + 1 < DEPTH
    row_spec = pl.BlockSpec((tm, D_MODEL), lambda i, j: (i, 0))
    in_specs = [
        row_spec,
        pl.BlockSpec((None, tm, D_MODEL), lambda i, j: (0, i, 0)),
        pl.BlockSpec((None, tm, D_MODEL), lambda i, j: (1, i, 0)),
        pl.BlockSpec((tm, ROUTER_COLS), lambda i, j: (i, 0)),
        pl.BlockSpec((None, tm, PLE_DIM), lambda i, j: (layer, i, 0)),
        pl.BlockSpec((None, 1, D_MODEL), lambda i, j: (layer, 0, 0)),
        pl.BlockSpec((None, PLE_DIM, D_MODEL), lambda i, j: (layer, 0, 0)),
        pl.BlockSpec(memory_space=pl.ANY),
    ]
    args = [h, yk, yk, meta, p_all, g_ple, w_ple_proj, w_ple_gate]
    out_specs = [row_spec]
    out_shape = [jax.ShapeDtypeStruct((rows, D_MODEL), F32)]
    scratch = [pltpu.VMEM((D_MODEL, D_MODEL), BF16),
               pltpu.VMEM((2, D_MODEL, W_CHUNK), F32),
               pltpu.SemaphoreType.DMA((2,))]
    if with_inproj:
        part_cols = IN_COLS // Z_PARTS
        in_specs += [pl.BlockSpec((None, 1, D_MODEL), lambda i, j: (layer + 1, 0, 0)),
                     pl.BlockSpec(memory_space=pl.ANY)]
        args += [g_mix, w_in]
        out_specs.append(pl.BlockSpec((tm, part_cols), lambda i, j: (i, j)))
        out_shape.append(jax.ShapeDtypeStruct((rows, IN_COLS), F32))
        scratch += [pltpu.VMEM((Z_PARTS, D_MODEL, part_cols), BF16), pltpu.VMEM((tm, D_MODEL), BF16)]
    return pl.pallas_call(
        functools.partial(_post_kernel, layer=layer, with_inproj=with_inproj),
        grid=(rows // tm, Z_PARTS if with_inproj else 1),
        in_specs=in_specs,
        out_specs=out_specs,
        out_shape=out_shape,
        scratch_shapes=scratch,
        compiler_params=pltpu.CompilerParams(dimension_semantics=("arbitrary", "arbitrary"),
                                             vmem_limit_bytes=POST_VMEM_LIMIT),
        name="post",
    )(*args)


def _token_tail_prompt(h, mix, od, p_all, layer, tw, g_mix, w_in):
    (w_out, w_out_d, g_ffn, w_router, b_router, w_gate, w_up, w_down, g_ple, w_ple_gate, w_ple_proj) = tw
    rows = h.shape[0]
    n_tiles = (2 * rows + N_EXPERTS * (EXPERT_TILE - 1) + EXPERT_TILE - 1) // EXPERT_TILE
    h, m, meta, meta_t, cnt = _outrouter(h, mix, od, g_ffn, w_router, b_router, w_out, w_out_d, layer)
    plan = _expert_plan(meta_t, cnt, n_tiles)
    yk = _experts_sparse(m, *plan, w_gate, w_up, w_down, layer).reshape(2, rows + 2 * SUBLANE, D_MODEL)
    out = _post(h, yk, meta, p_all, g_ple, w_ple_gate, w_ple_proj, g_mix, w_in, layer)
    return (out[0], out[1]) if layer + 1 < DEPTH else (out[0], None)


TAIL = 8


def _patch_kernel(x_ref, tail_ref, o_ref):
    del x_ref
    o_ref[...] = tail_ref[...]


def _patch_tail(x, tail):
    n, s, width = x.shape
    return pl.pallas_call(
        _patch_kernel,
        grid=(n,),
        in_specs=[pl.BlockSpec(memory_space=pl.ANY),
                  pl.BlockSpec((1, TAIL, width), lambda b: (b, 0, 0))],
        out_specs=pl.BlockSpec((1, TAIL, width), lambda b: (b, s // TAIL - 1, 0)),
        out_shape=jax.ShapeDtypeStruct(x.shape, x.dtype),
        input_output_aliases={0: 0},
        compiler_params=_cparams(("arbitrary",)),
        name="patch_tail",
    )(x, tail)


def kernel(x_prompt, x_sample, p_prompt, p_sample, state_conv_a, state_conv_c, cache_kv_w128, cache_kv_w512, cache_kv_w2048, g_mix, w_in, conv_a_w, conv_a_b, ln_a_g, ln_a_b, ln_b_g, ln_b_b, sgu_w, sgu_b, conv_c_w, g_q, g_k, w_out, g_ffn, w_router_grp, b_router_grp, w_router_exp, b_router_exp, w_gate, w_up, w_down, g_ple, w_ple_gate, w_ple_proj):
    n_p, s_p, _ = x_prompt.shape
    n_s, t_s, _ = x_sample.shape
    rows_p = n_p * s_p
    rows_s = n_s * t_s

    def row3(a):
        return a.reshape(DEPTH, 1, a.shape[-1])

    g_mix3, g_ffn3, g_ple3 = row3(g_mix), row3(g_ffn), row3(g_ple)
    cab3, lag3, lab3, lbg3, lbb3 = row3(conv_a_b), row3(ln_a_g), row3(ln_a_b), row3(ln_b_g), row3(ln_b_b)
    gq3, gk3 = row3(g_q), row3(g_k)
    sb_rep = jnp.repeat(jnp.swapaxes(sgu_b, 1, 2), HEAD_DIM, axis=2)
    sw8 = jnp.repeat(jnp.transpose(sgu_w[:, :, :t_s, :t_s], (0, 3, 2, 1)), HEAD_DIM, axis=3)
    w_out_d = jnp.pad(w_out[:, ABC_PAD - 64:].reshape(DEPTH, 3, GROUP_WIDTH, D_MODEL),
                      ((0, 0), (0, 0), (0, OD_PAD // 3 - GROUP_WIDTH), (0, 0))).reshape(DEPTH, OD_PAD, D_MODEL)
    w_router = jnp.concatenate(
        [jnp.transpose(w_router_exp, (0, 2, 1, 3)).reshape(DEPTH, D_MODEL, N_EXPERTS), w_router_grp,
         jnp.zeros((DEPTH, D_MODEL, ROUTER_COLS - N_EXPERTS - MOE_GROUPS), F32)], axis=2)
    b_router = jnp.concatenate(
        [b_router_exp.reshape(DEPTH, N_EXPERTS), b_router_grp,
         jnp.zeros((DEPTH, ROUTER_COLS - N_EXPERTS - MOE_GROUPS), F32)], axis=1).reshape(DEPTH, 1, ROUTER_COLS)
    tail_w = (w_out, w_out_d, g_ffn3, w_router, b_router, w_gate, w_up, w_down, g_ple3, w_ple_gate, w_ple_proj)
    mix_w_p = (conv_a_w, cab3, lag3, lab3, lbg3, lbb3, sgu_w, sb_rep, conv_c_w, gq3, gk3)
    mix_w_s = (conv_a_w, cab3, lag3, lab3, lbg3, lbb3, sw8, sb_rep, conv_c_w, gq3, gk3)

    caches = [c.reshape(c.shape[0], c.shape[1], c.shape[2], 2 * GROUP_WIDTH)
              for c in (cache_kv_w128, cache_kv_w512, cache_kv_w2048)]
    p_p = p_prompt.reshape(DEPTH, rows_p, PLE_DIM)
    rows_t = n_p * TAIL
    rows_h = rows_s + rows_t
    p_h = jnp.concatenate([p_sample.reshape(DEPTH, rows_s, PLE_DIM),
                           p_prompt[:, :, s_p - TAIL:].reshape(DEPTH, rows_t, PLE_DIM)], axis=1)
    h_h = jnp.concatenate([x_sample.reshape(rows_s, D_MODEL),
                           x_prompt[:, s_p - TAIL:].reshape(rows_t, D_MODEL)], axis=0)

    h = x_prompt.reshape(rows_p, D_MODEL)
    st_a, st_c, st_kv = [], [], [[], [], []]
    sa, sc, sv, skv = [], [], [], [[], [], []]
    z_next = _inproj(h, g_mix3, w_in, 0, 1024, 1024)
    for i in range(DEPTH):
        z_h = _inproj(h_h, g_mix3, w_in, i, rows_h, 512, hp=True)
        z = _patch_tail(z_next.reshape(n_p, s_p, IN_COLS), z_h[rows_s:].reshape(n_p, TAIL, IN_COLS))
        (mix, q1, kv1, q4, kv4, q16, kv16, sta, stc, s1, s4, s16) = _mixers_prompt(z, i, mix_w_p)
        o1, l1 = _attn_prompt(q1, kv1)
        o4, l4 = _attn_prompt(q4, kv4)
        o16, l16 = _attn_prompt(q16, kv16)
        od = _combine_prompt(o1, l1, o4, l4, o16, l16)
        (mix_s, od_s, na, nc, cv, n1, n4, n16) = _mixers_sample(
            z_h[:rows_s].reshape(n_s, t_s, IN_COLS), i, state_conv_a, state_conv_c, caches, mix_w_s)
        mix_h = jnp.concatenate([mix_s.reshape(rows_s, ABC_PAD),
                                 mix[:, s_p - TAIL:].reshape(rows_t, ABC_PAD).astype(F32)], axis=0)
        od_h = jnp.concatenate([od_s.reshape(rows_s, OD_PAD),
                                od[:, s_p - TAIL:].reshape(rows_t, OD_PAD).astype(F32)], axis=0)
        h, z_next = _token_tail_prompt(h, mix.reshape(rows_p, ABC_PAD), od.reshape(rows_p, OD_PAD), p_p, i,
                                       tail_w, g_mix3, w_in)
        h_h = _token_tail_hp(h_h, mix_h, od_h, p_h, i, tail_w)
        st_a.append(sta[:, A_HALO - (A_CONV_LEN - 1):])
        st_c.append(stc[:, C_HALO - (C_CONV_LEN - 1):])
        for g, s_kv in enumerate((s1, s4, s16)):
            st_kv[g].append(s_kv.reshape(n_p, s_kv.shape[1], 2, HEADS_PER_GROUP, HEAD_DIM))
        sa.append(na)
        sc.append(nc)
        sv.append(cv)
        for g, nk in enumerate((n1, n4, n16)):
            skv[g].append(nk.reshape(n_s, nk.shape[1], 2, HEADS_PER_GROUP, HEAD_DIM))
    y_prompt = _patch_tail(h.reshape(n_p, s_p, D_MODEL), h_h[rows_s:].reshape(n_p, TAIL, D_MODEL))
    conv_a_prompt = jnp.stack(st_a)
    conv_c_prompt = jnp.stack(st_c)
    kv_prompt = [jnp.stack(s) for s in st_kv]
    y_sample = h_h[:rows_s].reshape(n_s, t_s, D_MODEL)
    conv_a_sample = jnp.stack(sa)
    conv_c_sample = jnp.stack(sc)
    chunk_v_sample = jnp.stack(sv)
    kv_sample = [jnp.stack(s) for s in skv]

    return (y_prompt, y_sample, conv_a_prompt, conv_a_sample, conv_c_prompt, conv_c_sample, chunk_v_sample,
            kv_prompt[0], kv_sample[0], kv_prompt[1], kv_sample[1], kv_prompt[2], kv_sample[2])
```

```python
import functools

import jax
import jax.numpy as jnp
from jax import lax
from jax.experimental import pallas as pl
from jax.experimental.pallas import tpu as pltpu

F32 = jnp.float32
BF16 = jnp.bfloat16

D_MODEL = 2048
DEPTH = 4
PLE_DIM = 256
HEAD_DIM = 64
A_WIDTH = 512
A_CONV_LEN = 31
B_WIDTH = 512
B_HEADS = 8
CHUNK = 128
C_WIDTH = 448
C_CONV_LEN = 3
ATTN_GROUPS = ((128, 1), (512, 4), (2048, 16))
HEADS_PER_GROUP = 3
WIN_KEYS = 128
D_HEADS = 9
D_WIDTH = D_HEADS * HEAD_DIM
GROUP_WIDTH = HEADS_PER_GROUP * HEAD_DIM
IN_COLS = 2 * A_WIDTH + 2 * B_WIDTH + 3 * C_WIDTH + 3 * D_WIDTH
COL_B = 2 * A_WIDTH
COL_C = COL_B + 2 * B_WIDTH
COL_D = COL_C + 3 * C_WIDTH
ATTN_SCALE = HEAD_DIM ** -0.5
MOE_GROUPS = 4
EXPERTS_PER_GROUP = 4
N_EXPERTS = 16
EXPERT_FF = 512
RMS_EPS = 1e-6
LN_EPS = 1e-5

LANE = 128
SUBLANE = 8
ABC_PAD = 1536
OD_PAD = 3 * 256
ROUTER_COLS = 128
VMEM_LIMIT = 56 * 1024 * 1024
NEG_INF = float("-inf")


def _cparams(sem):
    return pltpu.CompilerParams(dimension_semantics=sem, vmem_limit_bytes=VMEM_LIMIT)


def _rms_rows(x, g):
    return x * lax.rsqrt(jnp.mean(x * x, axis=-1, keepdims=True) + RMS_EPS) * g


def _layernorm_rows(x, g, b):
    mu = jnp.mean(x, axis=-1, keepdims=True)
    xc = x - mu
    var = jnp.mean(xc * xc, axis=-1, keepdims=True)
    return xc * lax.rsqrt(var + LN_EPS) * g + b


def _sigmoid(x):
    return 1.0 / (1.0 + jnp.exp(-x))


def _silu(x):
    return x * _sigmoid(x)


def _gelu(x):
    return 0.5 * x * (1.0 + lax.erf(x * (2.0 ** -0.5)))


def _split_bf16(x):
    hi = x.astype(BF16)
    return hi, (x - hi.astype(F32)).astype(BF16)


def _mm(x, w, hp):
    if not hp:
        return jnp.dot(x.astype(BF16), w.astype(BF16), preferred_element_type=F32)
    rows = x.shape[0]
    xh, xl = _split_bf16(x)
    wh, wl = _split_bf16(w)
    r = jnp.dot(jnp.concatenate([xh, xl], axis=0), wh, preferred_element_type=F32)
    return r[:rows] + r[rows:] + jnp.dot(xh, wl, preferred_element_type=F32)


def _act_dtype(hp):
    return F32 if hp else BF16


def _head_norm(x, g):
    outs = []
    for h in range(D_HEADS):
        xh = x[:, HEAD_DIM * h:HEAD_DIM * (h + 1)]
        outs.append(_rms_rows(xh, g))
    return outs


def _split_qkv(z_ref, rows):
    q_lo = (COL_D // LANE) * LANE
    zq = z_ref[0, rows, q_lo:q_lo + 640]
    q = zq[:, COL_D - q_lo:COL_D - q_lo + D_WIDTH]
    k_lo = COL_D + D_WIDTH
    zk = z_ref[0, rows, k_lo:k_lo + 640]
    k = zk[:, :D_WIDTH]
    v_lo = ((COL_D + 2 * D_WIDTH) // LANE) * LANE
    zv = z_ref[0, rows, v_lo:v_lo + 640]
    v = zv[:, COL_D + 2 * D_WIDTH - v_lo:]
    return q, k, v


def _split_c(z_ref, rows):
    zc = z_ref[0, rows, COL_C:COL_C + 1408]
    return zc[:, 0:C_WIDTH], zc[:, C_WIDTH:2 * C_WIDTH], zc[:, 2 * C_WIDTH:3 * C_WIDTH]


def _inproj_kernel(x_ref, g_ref, w_ref, o_ref, xn_ref, *, hp):
    @pl.when(pl.program_id(1) == 0)
    def _():
        xn_ref[...] = _rms_rows(x_ref[...], g_ref[...]).astype(xn_ref.dtype)

    o_ref[...] = _mm(xn_ref[...], w_ref[...], hp)


def _inproj(h, g_all, w_all, layer, tm, tn, hp=False):
    rows = h.shape[0]
    return pl.pallas_call(
        functools.partial(_inproj_kernel, hp=hp),
        grid=(rows // tm, IN_COLS // tn),
        in_specs=[
            pl.BlockSpec((tm, D_MODEL), lambda i, j: (i, 0)),
            pl.BlockSpec((None, 1, D_MODEL), lambda i, j: (layer, 0, 0)),
            pl.BlockSpec((None, D_MODEL, tn), lambda i, j: (layer, 0, j)),
        ],
        out_specs=pl.BlockSpec((tm, tn), lambda i, j: (i, j)),
        out_shape=jax.ShapeDtypeStruct((rows, IN_COLS), F32),
        scratch_shapes=[pltpu.VMEM((tm, D_MODEL), _act_dtype(hp))],
        compiler_params=_cparams(("arbitrary", "arbitrary")),
        name="inproj",
    )(h, g_all, w_all)


MIX_TT = 256
CONV_ROWS = 64
A_HALO = 32
C_HALO = 8


def _pair_weights(sw_ref, wp_ref):
    row = lax.broadcasted_iota(jnp.int32, (CHUNK, CHUNK), 0)
    col = lax.broadcasted_iota(jnp.int32, (CHUNK, CHUNK), 1)
    keep = col <= row
    for p in range(B_HEADS // 2):
        w0 = jnp.where(keep, sw_ref[2 * p], 0.0)
        w1 = jnp.where(keep, sw_ref[2 * p + 1], 0.0)
        wp_ref[p] = jnp.concatenate([w0, w1], axis=1).astype(BF16)


def _mixer_kernel(z_ref, caw_ref, cab_ref, lag_ref, lab_ref, lbg_ref, lbb_ref, sw_ref, sb_ref, ccw_ref,
                  gq_ref, gk_ref,
                  mix_ref, q1_ref, kv1_ref, q4_ref, kv4_ref, q16_ref, kv16_ref,
                  sta_ref, stc_ref, st1_ref, st4_ref, st16_ref,
                  abuf, cbuf, wp_ref, qs_ref, kvs_ref, ashift):
    t = pl.program_id(1)
    tt = MIX_TT

    @pl.when(t == 0)
    def _():
        abuf[0:A_HALO, :] = jnp.zeros((A_HALO, A_WIDTH), F32)
        cbuf[0:C_HALO, :] = jnp.zeros((C_HALO, C_WIDTH), F32)
        _pair_weights(sw_ref, wp_ref)

    @pl.when(t > 0)
    def _():
        abuf[0:A_HALO, :] = abuf[tt:tt + A_HALO, :]
        cbuf[0:C_HALO, :] = cbuf[tt:tt + C_HALO, :]

    za = z_ref[0, :, 0:2 * A_WIDTH]
    abuf[A_HALO:A_HALO + tt, :] = za[:, :A_WIDTH] * _sigmoid(za[:, A_WIDTH:])
    sta_ref[0] = abuf[tt:tt + A_HALO, :]
    base = A_HALO - (A_CONV_LEN - 1)
    span = tt + A_HALO - SUBLANE
    for s in range(1, SUBLANE):
        ashift[s - 1, 0:span, :] = abuf[s:s + span, :]

    def tap_rows(first):
        phase = first % SUBLANE
        src = abuf if phase == 0 else ashift.at[phase - 1]
        return src[first - phase:first - phase + CONV_ROWS, :]

    for r0 in range(0, tt, CONV_ROWS):
        acc = caw_ref[0:1, :] * tap_rows(r0 + base)
        for j in range(1, A_CONV_LEN):
            acc = acc + caw_ref[j:j + 1, :] * tap_rows(r0 + base + j)
        y = _layernorm_rows(acc + cab_ref[...], lag_ref[...], lab_ref[...])
        mix_ref[0, r0:r0 + CONV_ROWS, 0:A_WIDTH] = _silu(y).astype(mix_ref.dtype)

    lane = lax.broadcasted_iota(jnp.int32, (CHUNK, LANE), 1)
    for c0 in range(0, tt, CHUNK):
        gb = _gelu(z_ref[0, c0:c0 + CHUNK, COL_B:COL_B + 2 * B_WIDTH])
        u = gb[:, :B_WIDTH]
        v = _layernorm_rows(gb[:, B_WIDTH:], lbg_ref[...], lbb_ref[...])
        pieces = []
        for p in range(B_HEADS // 2):
            v128 = v[:, LANE * p:LANE * (p + 1)]
            rhs = jnp.concatenate([jnp.where(lane < HEAD_DIM, v128, 0.0),
                                   jnp.where(lane >= HEAD_DIM, v128, 0.0)], axis=0).astype(BF16)
            pieces.append(jnp.dot(wp_ref[p], rhs, preferred_element_type=F32))
        mixed = jnp.concatenate(pieces, axis=1) + sb_ref[...]
        mix_ref[0, c0:c0 + CHUNK, A_WIDTH:A_WIDTH + B_WIDTH] = (u * mixed).astype(mix_ref.dtype)

    g_b, g_c, x_c = _split_c(z_ref, slice(None))
    cbuf[C_HALO:C_HALO + tt, :] = g_c * x_c
    stc_ref[0] = cbuf[tt:tt + C_HALO, :]
    cbase = C_HALO - (C_CONV_LEN - 1)
    conv = ccw_ref[0:1, :] * cbuf[cbase:cbase + tt, :]
    for j in range(1, C_CONV_LEN):
        conv = conv + ccw_ref[j:j + 1, :] * cbuf[cbase + j:cbase + j + tt, :]
    o_c = jnp.concatenate([g_b * conv, jnp.zeros((tt, ABC_PAD - 2 * A_WIDTH - C_WIDTH), F32)], axis=1)
    mix_ref[0, :, 2 * A_WIDTH:ABC_PAD] = o_c.astype(mix_ref.dtype)

    q, k, v = _split_qkv(z_ref, slice(None))
    qn = _head_norm(q, gq_ref[...] * ATTN_SCALE)
    kn = _head_norm(k, gk_ref[...])
    outs = ((q1_ref, kv1_ref, st1_ref), (q4_ref, kv4_ref, st4_ref), (q16_ref, kv16_ref, st16_ref))
    for g, (window, dil) in enumerate(ATTN_GROUPS):
        q_ref, kv_ref, st_ref = outs[g]
        hs = slice(HEADS_PER_GROUP * g, HEADS_PER_GROUP * (g + 1))
        q_g = jnp.concatenate(qn[hs], axis=1)
        kv_g = jnp.concatenate(kn[hs] + [v[:, GROUP_WIDTH * g:GROUP_WIDTH * (g + 1)]], axis=1)
        keep = min(window, tt)
        st_ref[0] = kv_g[tt - keep:, :]
        if dil == 1:
            q_ref[0, 0] = q_g
            kv_ref[0, 0] = kv_g
        else:
            qs_ref[0] = q_g[:, :LANE]
            qs_ref[1] = jnp.concatenate([q_g[:, LANE:], jnp.zeros((tt, 2 * LANE - GROUP_WIDTH), F32)], axis=1)
            for i in range(3):
                kvs_ref[i] = kv_g[:, LANE * i:LANE * (i + 1)]
            for r in range(dil):
                rows = pl.ds(r, tt // dil, stride=dil)
                q_ref[0, r, :, 0:LANE] = qs_ref[0, rows, :]
                q_ref[0, r, :, LANE:GROUP_WIDTH] = qs_ref[1, rows, :][:, :GROUP_WIDTH - LANE]
                for i in range(3):
                    kv_ref[0, r, :, LANE * i:LANE * (i + 1)] = kvs_ref[i, rows, :]


def _mixers_prompt(z, layer, wts):
    (caw, cab, lag, lab, lbg, lbb, sw, sb_rep, ccw, gq, gk) = wts
    n, s, _ = z.shape
    tt = MIX_TT
    nt = s // tt

    def lw(shape):
        nd = len(shape)
        return pl.BlockSpec((None,) + shape, lambda b, t: (layer,) + (0,) * nd)

    in_specs = [
        pl.BlockSpec((1, tt, IN_COLS), lambda b, t: (b, t, 0)),
        lw((A_CONV_LEN, A_WIDTH)), lw((1, A_WIDTH)), lw((1, A_WIDTH)), lw((1, A_WIDTH)),
        lw((1, B_WIDTH)), lw((1, B_WIDTH)), lw((B_HEADS, CHUNK, CHUNK)), lw((CHUNK, B_WIDTH)),
        lw((C_CONV_LEN, C_WIDTH)), lw((1, HEAD_DIM)), lw((1, HEAD_DIM)),
    ]
    out_shape = [jax.ShapeDtypeStruct((n, s, ABC_PAD), BF16)]
    out_specs = [pl.BlockSpec((1, tt, ABC_PAD), lambda b, t: (b, t, 0))]
    for _, dil in ATTN_GROUPS:
        for width in (GROUP_WIDTH, 2 * GROUP_WIDTH):
            out_shape.append(jax.ShapeDtypeStruct((n, dil, s // dil, width), F32))
            out_specs.append(pl.BlockSpec((1, dil, tt // dil, width), lambda b, t: (b, 0, t, 0)))
    out_shape.append(jax.ShapeDtypeStruct((n, A_HALO, A_WIDTH), F32))
    out_specs.append(pl.BlockSpec((1, A_HALO, A_WIDTH), lambda b, t: (b, 0, 0)))
    out_shape.append(jax.ShapeDtypeStruct((n, C_HALO, C_WIDTH), F32))
    out_specs.append(pl.BlockSpec((1, C_HALO, C_WIDTH), lambda b, t: (b, 0, 0)))
    for window, _ in ATTN_GROUPS:
        keep = min(window, s)
        blk = min(keep, tt)
        first = (s - keep) // blk
        out_shape.append(jax.ShapeDtypeStruct((n, keep, 2 * GROUP_WIDTH), F32))
        if keep <= tt:
            out_specs.append(pl.BlockSpec((1, blk, 2 * GROUP_WIDTH), lambda b, t: (b, 0, 0)))
        else:
            out_specs.append(pl.BlockSpec((1, blk, 2 * GROUP_WIDTH),
                                          lambda b, t, first=first: (b, jnp.maximum(t - first, 0), 0)))
    return pl.pallas_call(
        _mixer_kernel,
        grid=(n, nt),
        in_specs=in_specs,
        out_specs=out_specs,
        out_shape=out_shape,
        scratch_shapes=[
            pltpu.VMEM((tt + A_HALO, A_WIDTH), F32),
            pltpu.VMEM((tt + C_HALO, C_WIDTH), F32),
            pltpu.VMEM((B_HEADS // 2, CHUNK, 2 * CHUNK), BF16),
            pltpu.VMEM((2, tt, LANE), F32),
            pltpu.VMEM((3, tt, LANE), F32),
            pltpu.VMEM((SUBLANE - 1, tt + A_HALO, A_WIDTH), F32),
        ],
        compiler_params=_cparams(("arbitrary", "arbitrary")),
        name="mixers_prompt",
    )(z, caw, cab, lag, lab, lbg, lbb, sw, sb_rep, ccw, gq, gk)


ATTN_QBLOCKS = 8


def _attn_kernel(q_ref, kvo_ref, kvp_ref, o_ref, l_ref, *, qblocks):
    c = pl.program_id(2)
    qi = lax.broadcasted_iota(jnp.int32, (WIN_KEYS, 2 * WIN_KEYS), 0)
    kj = lax.broadcasted_iota(jnp.int32, (WIN_KEYS, 2 * WIN_KEYS), 1)
    dist = qi + WIN_KEYS - kj
    band = (dist >= 0) & (dist <= WIN_KEYS)
    first_key = jnp.where(c > 0, 0, WIN_KEYS)
    for s in range(qblocks):
        rows = slice(WIN_KEYS * s, WIN_KEYS * (s + 1))
        q = q_ref[0, 0, rows, :]
        kvo = kvo_ref[0, 0, rows, :]
        if s == 0:
            kvp = kvp_ref[0, 0]
            mask = band & (kj >= first_key)
        else:
            kvp = kvo_ref[0, 0, WIN_KEYS * (s - 1):WIN_KEYS * s, :]
            mask = band
        o_parts, l_parts = [], []
        for h in range(HEADS_PER_GROUP):
            ks = slice(HEAD_DIM * h, HEAD_DIM * (h + 1))
            vs = slice(GROUP_WIDTH + HEAD_DIM * h, GROUP_WIDTH + HEAD_DIM * (h + 1))
            qh = q[:, ks].astype(BF16)
            kk = jnp.concatenate([kvp[:, ks], kvo[:, ks]], axis=0).astype(BF16)
            vv = jnp.concatenate([kvp[:, vs], kvo[:, vs]], axis=0).astype(BF16)
            sc = lax.dot_general(qh, kk, (((1,), (1,)), ((), ())), preferred_element_type=F32)
            sc = jnp.where(mask, sc, NEG_INF)
            m = jnp.max(sc, axis=-1, keepdims=True)
            ex = jnp.exp(sc - m)
            den = jnp.sum(ex, axis=-1, keepdims=True)
            probs = (ex / den).astype(BF16)
            o_parts.append(jnp.dot(probs, vv, preferred_element_type=F32))
            l_parts.append(jnp.broadcast_to(m + jnp.log(den), (WIN_KEYS, HEAD_DIM)))
        o_ref[0, 0, rows, :] = jnp.concatenate(o_parts, axis=1)
        l_ref[0, 0, rows, :] = jnp.concatenate(l_parts, axis=1)


def _attn_prompt(q, kv):
    n, dil, sub, _ = q.shape
    qblocks = min(ATTN_QBLOCKS, sub // WIN_KEYS)
    rows = qblocks * WIN_KEYS
    qspec = pl.BlockSpec((1, 1, rows, GROUP_WIDTH), lambda b, r, c: (b, r, c, 0))
    return pl.pallas_call(
        functools.partial(_attn_kernel, qblocks=qblocks),
        grid=(n, dil, sub // rows),
        in_specs=[
            qspec,
            pl.BlockSpec((1, 1, rows, 2 * GROUP_WIDTH), lambda b, r, c: (b, r, c, 0)),
            pl.BlockSpec((1, 1, WIN_KEYS, 2 * GROUP_WIDTH),
                         lambda b, r, c: (b, r, jnp.maximum(c * qblocks - 1, 0), 0)),
        ],
        out_specs=[qspec, qspec],
        out_shape=[jax.ShapeDtypeStruct(q.shape, F32), jax.ShapeDtypeStruct(q.shape, F32)],
        compiler_params=_cparams(("arbitrary", "arbitrary", "arbitrary")),
        name="attn_prompt",
    )(q, kv, kv)


def _combine_kernel(o1_ref, l1_ref, o4_ref, l4_ref, o16_ref, l16_ref, od_ref, s_o4, s_l4, s_o16, s_l16):
    tt = MIX_TT
    for dil, src, dst in ((4, o4_ref, s_o4), (4, l4_ref, s_l4), (16, o16_ref, s_o16), (16, l16_ref, s_l16)):
        for r in range(dil):
            x = src[0, r]
            rows = pl.ds(r, tt // dil, stride=dil)
            dst[0, rows, :] = x[:, :LANE]
            dst[1, rows, :] = jnp.concatenate(
                [x[:, LANE:], jnp.zeros((tt // dil, 2 * LANE - GROUP_WIDTH), F32)], axis=1)

    def whole(scr):
        return jnp.concatenate([scr[0], scr[1][:, :GROUP_WIDTH - LANE]], axis=1)

    outs = (o1_ref[0, 0], whole(s_o4), whole(s_o16))
    lses = (l1_ref[0, 0], whole(s_l4), whole(s_l16))
    mx = jnp.maximum(jnp.maximum(lses[0], lses[1]), lses[2])
    es = [jnp.exp(l - mx) for l in lses]
    den = es[0] + es[1] + es[2]
    pad = jnp.zeros((tt, OD_PAD // 3 - GROUP_WIDTH), F32)
    parts = []
    for g in range(3):
        parts += [outs[g] * (es[g] / den), pad]
    od_ref[0] = jnp.concatenate(parts, axis=1).astype(od_ref.dtype)


def _combine_prompt(o1, l1, o4, l4, o16, l16):
    n, _, s, _ = o1.shape
    tt = MIX_TT

    def spec(dil):
        return pl.BlockSpec((1, dil, tt // dil, GROUP_WIDTH), lambda b, t: (b, 0, t, 0))

    return pl.pallas_call(
        _combine_kernel,
        grid=(n, s // tt),
        in_specs=[spec(1), spec(1), spec(4), spec(4), spec(16), spec(16)],
        out_specs=pl.BlockSpec((1, tt, OD_PAD), lambda b, t: (b, t, 0)),
        out_shape=jax.ShapeDtypeStruct((n, s, OD_PAD), BF16),
        scratch_shapes=[pltpu.VMEM((2, tt, LANE), F32)] * 4,
        compiler_params=_cparams(("arbitrary", "arbitrary")),
        name="combine_prompt",
    )(o1, l1, o4, l4, o16, l16)


def _dec_kernel(z_ref, ha_ref, hc_ref, c1_ref, c4_ref, c16_ref,
                caw_ref, cab_ref, lag_ref, lab_ref, lbg_ref, lbb_ref, sw8_ref, sb_ref, ccw_ref, gq_ref, gk_ref,
                mix_ref, od_ref, na_ref, nc_ref, cv_ref, n1_ref, n4_ref, n16_ref,
                abuf, cbuf, kvbuf, exbuf):
    t_new = z_ref.shape[1]
    hist_a = A_CONV_LEN - 1
    hist_c = C_CONV_LEN - 1

    za = z_ref[0, :, 0:2 * A_WIDTH]
    abuf[0:hist_a, :] = ha_ref[0]
    abuf[hist_a:hist_a + t_new, :] = za[:, :A_WIDTH] * _sigmoid(za[:, A_WIDTH:])
    acc = caw_ref[0:1, :] * abuf[0:t_new, :]
    for j in range(1, A_CONV_LEN):
        acc = acc + caw_ref[j:j + 1, :] * abuf[j:j + t_new, :]
    y = _layernorm_rows(acc + cab_ref[...], lag_ref[...], lab_ref[...])
    mix_ref[0, :, 0:A_WIDTH] = _silu(y).astype(mix_ref.dtype)
    na_ref[0] = abuf[t_new:t_new + hist_a, :]

    gb = _gelu(z_ref[0, :, COL_B:COL_B + 2 * B_WIDTH])
    u = gb[:, :B_WIDTH]
    v = _layernorm_rows(gb[:, B_WIDTH:], lbg_ref[...], lbb_ref[...])
    cv_ref[0] = v
    row = lax.broadcasted_iota(jnp.int32, (t_new, B_WIDTH), 0)
    mixed = sb_ref[0:t_new, :]
    for s in range(t_new):
        mixed = mixed + jnp.where(row >= s, sw8_ref[s], 0.0) * v[s:s + 1, :]
    mix_ref[0, :, A_WIDTH:A_WIDTH + B_WIDTH] = (u * mixed).astype(mix_ref.dtype)

    g_b, g_c, x_c = _split_c(z_ref, slice(None))
    cbuf[0:hist_c, :] = hc_ref[0]
    cbuf[hist_c:hist_c + t_new, :] = g_c * x_c
    conv = ccw_ref[0:1, :] * cbuf[0:t_new, :]
    for j in range(1, C_CONV_LEN):
        conv = conv + ccw_ref[j:j + 1, :] * cbuf[j:j + t_new, :]
    o_c = jnp.concatenate([g_b * conv, jnp.zeros((t_new, ABC_PAD - 2 * A_WIDTH - C_WIDTH), F32)], axis=1)
    mix_ref[0, :, 2 * A_WIDTH:ABC_PAD] = o_c.astype(mix_ref.dtype)
    nc_ref[0] = cbuf[t_new:t_new + hist_c, :]

    q, k, v_d = _split_qkv(z_ref, slice(None))
    qn = _head_norm(q, gq_ref[...] * ATTN_SCALE)
    kn = _head_norm(k, gk_ref[...])
    caches = (c1_ref, c4_ref, c16_ref)
    news = (n1_ref, n4_ref, n16_ref)
    qrow = lax.broadcasted_iota(jnp.int32, (LANE, 2 * GROUP_WIDTH), 0)
    qlane = lax.broadcasted_iota(jnp.int32, (LANE, 2 * GROUP_WIDTH), 1)
    qmask = (qrow >> 3) == (qlane >> 6)
    zeros_q = jnp.zeros((t_new, GROUP_WIDTH), F32)
    lses, dens, offs = [], [], []
    off = 0
    for g, (window, dil) in enumerate(ATTN_GROUPS):
        hs = slice(HEADS_PER_GROUP * g, HEADS_PER_GROUP * (g + 1))
        buf_len = caches[g].shape[1]
        rows = buf_len + t_new
        kv_new = jnp.concatenate(kn[hs] + [v_d[:, GROUP_WIDTH * g:GROUP_WIDTH * (g + 1)]], axis=1)
        kvbuf[off:off + buf_len, :] = caches[g][0]
        kvbuf[off + buf_len:off + rows, :] = kv_new
        news[g][0] = kvbuf[off + t_new:off + rows, :]
        q_g = jnp.concatenate(qn[hs] + [zeros_q], axis=1)
        q_rep = jnp.concatenate([q_g] * HEADS_PER_GROUP
                                + [jnp.zeros((LANE - HEADS_PER_GROUP * t_new, 2 * GROUP_WIDTH), F32)], axis=0)
        q_hi, q_lo = _split_bf16(jnp.where(qmask, q_rep, 0.0))
        kv_hi, kv_lo = _split_bf16(kvbuf[off:off + rows, :])
        nt_dims = (((1,), (1,)), ((), ()))
        sc = (lax.dot_general(kv_hi, q_hi, nt_dims, preferred_element_type=F32)
              + lax.dot_general(kv_lo, q_hi, nt_dims, preferred_element_type=F32)
              + lax.dot_general(kv_hi, q_lo, nt_dims, preferred_element_type=F32))
        krow = lax.broadcasted_iota(jnp.int32, (rows, LANE), 0)
        tok = lax.broadcasted_iota(jnp.int32, (rows, LANE), 1) & (t_new - 1)
        dist = buf_len + tok - krow
        valid = (dist >= 0) & (dist <= dil * WIN_KEYS) & ((dist & (dil - 1)) == 0)
        sc = jnp.where(valid, sc, NEG_INF)
        m = jnp.max(sc, axis=0, keepdims=True)
        ex = jnp.exp(sc - m)
        den = jnp.sum(ex, axis=0, keepdims=True)
        exbuf[off:off + rows, :] = ex
        lses.append(m + jnp.log(den))
        dens.append(den)
        offs.append((off, rows))
        off += rows
    mx = jnp.maximum(jnp.maximum(lses[0], lses[1]), lses[2])
    es = [jnp.exp(l - mx) for l in lses]
    tot = es[0] + es[1] + es[2]
    lane = lax.broadcasted_iota(jnp.int32, (t_new, 2 * GROUP_WIDTH), 1)
    pad = jnp.zeros((t_new, OD_PAD // 3 - GROUP_WIDTH), F32)
    parts = []
    for g in range(3):
        off, rows = offs[g]
        coef = es[g] / (tot * dens[g])
        probs = (exbuf[off:off + rows, :] * coef).astype(BF16)
        kv_all = kvbuf[off:off + rows, :].astype(BF16)
        o_t = lax.dot_general(probs, kv_all, (((0,), (0,)), ((), ())), preferred_element_type=F32)
        o_g = jnp.zeros((t_new, 2 * GROUP_WIDTH), F32)
        for h in range(HEADS_PER_GROUP):
            sel = (lane >= GROUP_WIDTH + HEAD_DIM * h) & (lane < GROUP_WIDTH + HEAD_DIM * (h + 1))
            o_g = o_g + jnp.where(sel, o_t[t_new * h:t_new * (h + 1), :], 0.0)
        parts += [o_g[:, GROUP_WIDTH:], pad]
    od_ref[0] = jnp.concatenate(parts, axis=1).astype(od_ref.dtype)


def _mixers_sample(z, layer, hist_a, hist_c, caches, wts):
    (caw, cab, lag, lab, lbg, lbb, sw8, sb_rep, ccw, gq, gk) = wts
    n, t_new, _ = z.shape

    def lw(shape):
        nd = len(shape)
        return pl.BlockSpec((None,) + shape, lambda b: (layer,) + (0,) * nd)

    def st(shape):
        nd = len(shape)
        return pl.BlockSpec((None, 1) + shape, lambda b: (layer, b) + (0,) * nd)

    lens = [c.shape[2] for c in caches]
    total_rows = sum(lens) + 3 * t_new
    in_specs = [
        pl.BlockSpec((1, t_new, IN_COLS), lambda b: (b, 0, 0)),
        st((A_CONV_LEN - 1, A_WIDTH)), st((C_CONV_LEN - 1, C_WIDTH)),
        st((lens[0], 2 * GROUP_WIDTH)), st((lens[1], 2 * GROUP_WIDTH)), st((lens[2], 2 * GROUP_WIDTH)),
        lw((A_CONV_LEN, A_WIDTH)), lw((1, A_WIDTH)), lw((1, A_WIDTH)), lw((1, A_WIDTH)),
        lw((1, B_WIDTH)), lw((1, B_WIDTH)), lw((t_new, t_new, B_WIDTH)), lw((CHUNK, B_WIDTH)),
        lw((C_CONV_LEN, C_WIDTH)), lw((1, HEAD_DIM)), lw((1, HEAD_DIM)),
    ]

    def ob(shape):
        nd = len(shape)
        return pl.BlockSpec((1,) + shape, lambda b: (b,) + (0,) * nd)

    out_shape = [
        jax.ShapeDtypeStruct((n, t_new, ABC_PAD), F32),
        jax.ShapeDtypeStruct((n, t_new, OD_PAD), F32),
        jax.ShapeDtypeStruct((n, A_CONV_LEN - 1, A_WIDTH), F32),
        jax.ShapeDtypeStruct((n, C_CONV_LEN - 1, C_WIDTH), F32),
        jax.ShapeDtypeStruct((n, t_new, B_WIDTH), F32),
    ] + [jax.ShapeDtypeStruct((n, ln, 2 * GROUP_WIDTH), F32) for ln in lens]
    out_specs = [ob(s.shape[1:]) for s in out_shape]
    return pl.pallas_call(
        _dec_kernel,
        grid=(n,),
        in_specs=in_specs,
        out_specs=out_specs,
        out_shape=out_shape,
        scratch_shapes=[
            pltpu.VMEM((A_CONV_LEN - 1 + t_new + 2, A_WIDTH), F32),
            pltpu.VMEM((16, C_WIDTH), F32),
            pltpu.VMEM((total_rows, 2 * GROUP_WIDTH), F32),
            pltpu.VMEM((total_rows, LANE), F32),
        ],
        compiler_params=_cparams(("arbitrary",)),
        name="mixers_sample",
    )(z, hist_a, hist_c, *caches, caw, cab, lag, lab, lbg, lbb, sw8, sb_rep, ccw, gq, gk)


def _outproj_kernel(h_ref, mix_ref, od_ref, w_ref, wd_ref, o_ref, *, hp):
    acc = _mm(mix_ref[...], w_ref[0:ABC_PAD, :], hp) + _mm(od_ref[...], wd_ref[...], hp)
    o_ref[...] = h_ref[...] + acc


def _outproj(h, mix, od, w_all, wd_all, layer, tm, tn, hp):
    rows = h.shape[0]
    return pl.pallas_call(
        functools.partial(_outproj_kernel, hp=hp),
        grid=(rows // tm, D_MODEL // tn),
        in_specs=[
            pl.BlockSpec((tm, tn), lambda i, j: (i, j)),
            pl.BlockSpec((tm, ABC_PAD), lambda i, j: (i, 0)),
            pl.BlockSpec((tm, OD_PAD), lambda i, j: (i, 0)),
            pl.BlockSpec((None, D_MODEL, tn), lambda i, j: (layer, 0, j)),
            pl.BlockSpec((None, OD_PAD, tn), lambda i, j: (layer, 0, j)),
        ],
        out_specs=pl.BlockSpec((tm, tn), lambda i, j: (i, j)),
        out_shape=jax.ShapeDtypeStruct((rows, D_MODEL), F32),
        compiler_params=_cparams(("arbitrary", "arbitrary")),
        name="outproj",
    )(h, mix, od, w_all, wd_all)


def _top2_in_top_group(logits):
    rows = logits.shape[0]
    lane = lax.broadcasted_iota(jnp.int32, (rows, ROUTER_COLS), 1)
    big = jnp.int32(ROUTER_COLS)
    is_grp = (lane >= N_EXPERTS) & (lane < N_EXPERTS + MOE_GROUPS)
    gl = jnp.where(is_grp, logits, NEG_INF)
    gmax = jnp.max(gl, axis=-1, keepdims=True)
    gidx = jnp.min(jnp.where(gl == gmax, lane - N_EXPERTS, big), axis=-1, keepdims=True)
    gate = 1.0 / jnp.sum(jnp.where(is_grp, jnp.exp(gl - gmax), 0.0), axis=-1, keepdims=True)
    in_grp = (lane < N_EXPERTS) & ((lane >> 2) == gidx)
    el = jnp.where(in_grp, logits, NEG_INF)
    t1 = jnp.max(el, axis=-1, keepdims=True)
    i1 = jnp.min(jnp.where(el == t1, lane, big), axis=-1, keepdims=True)
    el2 = jnp.where(lane == i1, NEG_INF, el)
    t2 = jnp.max(el2, axis=-1, keepdims=True)
    i2 = jnp.min(jnp.where(el2 == t2, lane, big), axis=-1, keepdims=True)
    e2 = jnp.exp(t2 - t1)
    return lane, i1, i2, gate / (1.0 + e2), gate * e2 / (1.0 + e2)


META_E1, META_E2, META_R1, META_R2, META_W1, META_W2 = range(6)


def _outrouter_kernel(h_ref, mix_ref, od_ref, g_ref, wr_ref, br_ref, wout_hbm, woutd_hbm,
                      h_out, m_ref, meta_ref, meta_t_ref, cnt_ref,
                      wo_b, wod_b, stage, sem, carry_ref, *, layer):
    @pl.when(pl.program_id(0) == 0)
    def _():
        carry_ref[...] = jnp.zeros_like(carry_ref)
        _stream_cast(wout_hbm, layer, wo_b, stage, sem)
        _stream_cast(woutd_hbm, layer, wod_b, stage, sem)

    acc = (jnp.dot(mix_ref[...], wo_b[...], preferred_element_type=F32)
           + jnp.dot(od_ref[...], wod_b[...], preferred_element_type=F32))
    h1 = h_ref[...] + acc
    h_out[...] = h1
    m = _rms_rows(h1, g_ref[...])
    m_ref[...] = m
    logits = _mm(m, wr_ref[...], False) + br_ref[...]
    rows = logits.shape[0]
    lane, i1, i2, w1, w2 = _top2_in_top_group(logits)
    sel = jnp.where((lane == i1) | (lane == i2), 1.0, 0.0)
    r = lax.broadcasted_iota(jnp.int32, (rows, rows), 0)
    c = lax.broadcasted_iota(jnp.int32, (rows, rows), 1)
    earlier = jnp.where(c < r, 1.0, 0.0).astype(BF16)
    rank = jnp.dot(earlier, sel.astype(BF16), preferred_element_type=F32) + carry_ref[...]
    r1 = jnp.sum(jnp.where(lane == i1, rank, 0.0), axis=-1, keepdims=True)
    r2 = jnp.sum(jnp.where(lane == i2, rank, 0.0), axis=-1, keepdims=True)
    carry_ref[...] += jnp.sum(sel, axis=0, keepdims=True)
    cnt_ref[...] = carry_ref[...]
    meta = jnp.zeros((rows, ROUTER_COLS), F32)
    for pos, val in ((META_E1, i1.astype(F32)), (META_E2, i2.astype(F32)), (META_R1, r1), (META_R2, r2),
                     (META_W1, w1), (META_W2, w2)):
        meta = jnp.where(lane == pos, val, meta)
    meta_ref[...] = meta
    meta_t_ref[...] = meta.T[:SUBLANE, :]


OUTROUTER_TM = 512


def _outrouter(h, mix, od, g_all, wr_all, br_all, w_out, w_out_d, layer):
    rows = h.shape[0]
    tm = OUTROUTER_TM
    row_spec = pl.BlockSpec((tm, D_MODEL), lambda i: (i, 0))
    return pl.pallas_call(
        functools.partial(_outrouter_kernel, layer=layer),
        grid=(rows // tm,),
        in_specs=[
            row_spec,
            pl.BlockSpec((tm, ABC_PAD), lambda i: (i, 0)),
            pl.BlockSpec((tm, OD_PAD), lambda i: (i, 0)),
            pl.BlockSpec((None, 1, D_MODEL), lambda i: (layer, 0, 0)),
            pl.BlockSpec((None, D_MODEL, ROUTER_COLS), lambda i: (layer, 0, 0)),
            pl.BlockSpec((None, 1, ROUTER_COLS), lambda i: (layer, 0, 0)),
            pl.BlockSpec(memory_space=pl.ANY),
            pl.BlockSpec(memory_space=pl.ANY),
        ],
        out_specs=[row_spec, row_spec,
                   pl.BlockSpec((tm, ROUTER_COLS), lambda i: (i, 0)),
                   pl.BlockSpec((SUBLANE, tm), lambda i: (0, i)),
                   pl.BlockSpec((1, ROUTER_COLS), lambda i: (0, 0))],
        out_shape=[jax.ShapeDtypeStruct((rows, D_MODEL), F32),
                   jax.ShapeDtypeStruct((rows, D_MODEL), F32),
                   jax.ShapeDtypeStruct((rows, ROUTER_COLS), F32),
                   jax.ShapeDtypeStruct((SUBLANE, rows), F32),
                   jax.ShapeDtypeStruct((1, ROUTER_COLS), F32)],
        scratch_shapes=[pltpu.VMEM((ABC_PAD, D_MODEL), BF16),
                        pltpu.VMEM((OD_PAD, D_MODEL), BF16),
                        pltpu.VMEM((2, D_MODEL, W_CHUNK), F32),
                        pltpu.SemaphoreType.DMA((2,)),
                        pltpu.VMEM((1, ROUTER_COLS), F32)],
        compiler_params=_cparams(("arbitrary",)),
        name="outrouter",
    )(h, mix, od, g_all, wr_all, br_all, w_out, w_out_d)


EXPERT_TILE = 256


def _experts_kernel(te_ref, tv_ref, s0_ref, s1_ref, nu_ref,
                    m_hbm, wg_ref, wu_ref, wd_ref, yk_hbm,
                    gsrc_ref, ssrc_ref, xbuf, ybuf, wgb, wub, wdb, gsem, ssem, *, n_tok):
    i = pl.program_id(0)
    tile = EXPERT_TILE
    nused = nu_ref[0]
    slot = lax.rem(i, 2)
    plane = n_tok + 2 * SUBLANE

    def rows_moved(t):
        return pl.multiple_of(((tv_ref[t] + SUBLANE - 1) // SUBLANE) * SUBLANE, SUBLANE)

    def gather_copy(s, j, b):
        return pltpu.make_async_copy(m_hbm.at[pl.ds(gsrc_ref[s], 1), :], xbuf.at[b, pl.ds(j, 1), :], gsem.at[b])

    def scatter_copy(s, j, b):
        return pltpu.make_async_copy(ybuf.at[b, pl.ds(j, 1), :], yk_hbm.at[pl.ds(ssrc_ref[s], 1), :], ssem.at[b])

    def start_rows(copy_fn, t, b, both_dma_threads=False):
        def body(j8, carry):
            for u in range(SUBLANE):
                j = j8 * SUBLANE + u
                copy_fn(t * tile + j, j, b).start(priority=u % 2 if both_dma_threads else 0)
            return carry
        lax.fori_loop(0, rows_moved(t) // SUBLANE, body, 0)

    def wait_gather(t, b):
        n = rows_moved(t)
        pltpu.make_async_copy(m_hbm.at[pl.ds(0, n), :], xbuf.at[b, pl.ds(0, n), :], gsem.at[b]).wait()

    def wait_scatter(t, b):
        n = rows_moved(t)
        pltpu.make_async_copy(ybuf.at[b, pl.ds(0, n), :], yk_hbm.at[pl.ds(0, n), :], ssem.at[b]).wait()

    @pl.when(i == 0)
    def _():
        def fill(t8, carry):
            for u in range(SUBLANE):
                t = t8 * SUBLANE + u
                gsrc_ref[s0_ref[t]] = t
                gsrc_ref[s1_ref[t]] = t
                ssrc_ref[s0_ref[t]] = t
                ssrc_ref[s1_ref[t]] = plane + t
            return carry
        lax.fori_loop(0, n_tok // SUBLANE, fill, 0)

        def fill_pad(t, carry):
            def one(j, c):
                gsrc_ref[t * tile + j] = 0
                ssrc_ref[t * tile + j] = n_tok + SUBLANE * lax.rem(t, 2) + lax.rem(j, SUBLANE)
                return c
            lax.fori_loop(tv_ref[t], rows_moved(t), one, 0)
            return carry
        lax.fori_loop(0, nused, fill_pad, 0)
        xbuf[...] = jnp.zeros_like(xbuf)
        for k in range(2):
            spare = pltpu.make_async_copy(xbuf.at[0, pl.ds(0, 2 * SUBLANE), :],
                                          yk_hbm.at[pl.ds(k * plane + n_tok, 2 * SUBLANE), :], ssem.at[0])
            spare.start()
            spare.wait()
        start_rows(gather_copy, 0, 0)

    @pl.when(i < nused)
    def _():
        @pl.when(i + 1 < nused)
        def _():
            start_rows(gather_copy, i + 1, 1 - slot)

        changed = jnp.logical_or(i == 0, te_ref[i] != te_ref[jnp.maximum(i - 1, 0)])

        @pl.when(changed)
        def _():
            wgb[...] = wg_ref[...].astype(BF16)
            wub[...] = wu_ref[...].astype(BF16)
            wdb[...] = wd_ref[...].astype(BF16)

        wait_gather(i, slot)

        @pl.when(i >= 2)
        def _():
            wait_scatter(i - 2, slot)

        x = xbuf[slot].astype(BF16)
        gate = jnp.dot(x, wgb[...], preferred_element_type=F32)
        up = jnp.dot(x, wub[...], preferred_element_type=F32)
        ybuf[slot] = jnp.dot((_silu(gate) * up).astype(BF16), wdb[...], preferred_element_type=F32)
        start_rows(scatter_copy, i, slot, both_dma_threads=True)

        @pl.when(i == nused - 1)
        def _():
            @pl.when(i >= 1)
            def _():
                wait_scatter(i - 1, 1 - slot)
            wait_scatter(i, slot)


def _experts_sparse(m, tile_expert, tile_rows, slot0, slot1, nused, wg_all, wu_all, wd_all, layer):
    n_tok = m.shape[0]
    n_tiles = tile_expert.shape[0]

    def wspec(shape):
        return pl.BlockSpec((None, None) + shape, lambda i, te, tv, s0, s1, nu: (layer, te[i], 0, 0))

    grid_spec = pltpu.PrefetchScalarGridSpec(
        num_scalar_prefetch=5,
        grid=(n_tiles,),
        in_specs=[pl.BlockSpec(memory_space=pl.ANY),
                  wspec((D_MODEL, EXPERT_FF)), wspec((D_MODEL, EXPERT_FF)), wspec((EXPERT_FF, D_MODEL))],
        out_specs=pl.BlockSpec(memory_space=pl.ANY),
        scratch_shapes=[
            pltpu.SMEM((n_tiles * EXPERT_TILE,), jnp.int32),
            pltpu.SMEM((n_tiles * EXPERT_TILE,), jnp.int32),
            pltpu.VMEM((2, EXPERT_TILE, D_MODEL), F32),
            pltpu.VMEM((2, EXPERT_TILE, D_MODEL), F32),
            pltpu.VMEM((D_MODEL, EXPERT_FF), BF16),
            pltpu.VMEM((D_MODEL, EXPERT_FF), BF16),
            pltpu.VMEM((EXPERT_FF, D_MODEL), BF16),
            pltpu.SemaphoreType.DMA((2,)),
            pltpu.SemaphoreType.DMA((2,)),
        ],
    )
    return pl.pallas_call(
        functools.partial(_experts_kernel, n_tok=n_tok),
        grid_spec=grid_spec,
        out_shape=jax.ShapeDtypeStruct((2 * (n_tok + 2 * SUBLANE), D_MODEL), F32),
        compiler_params=_cparams(("arbitrary",)),
        name="experts_sparse",
    )(tile_expert, tile_rows, slot0, slot1, nused, m, wg_all, wu_all, wd_all)


def _expert_plan(meta_t, cnt, n_tiles):
    tile = EXPERT_TILE
    counts = cnt[0, :N_EXPERTS].astype(jnp.int32)
    padded = ((counts + tile - 1) // tile) * tile
    ends = jnp.cumsum(padded)
    base = ends - padded
    e1 = meta_t[META_E1].astype(jnp.int32)
    e2 = meta_t[META_E2].astype(jnp.int32)
    slot0 = base[e1] + meta_t[META_R1].astype(jnp.int32)
    slot1 = base[e2] + meta_t[META_R2].astype(jnp.int32)
    start = jnp.arange(n_tiles, dtype=jnp.int32) * tile
    expert_of = jnp.sum((start[:, None] >= ends[None, :]).astype(jnp.int32), axis=1)
    last_used = jnp.max(jnp.where(counts > 0, jnp.arange(N_EXPERTS, dtype=jnp.int32), 0))
    tile_expert = jnp.minimum(expert_of, last_used)
    e_clamped = jnp.minimum(expert_of, N_EXPERTS - 1)
    tile_rows = jnp.clip(counts[e_clamped] - (start - base[e_clamped]), 0, tile)
    tile_rows = jnp.where(expert_of < N_EXPERTS, tile_rows, 0).astype(jnp.int32)
    nused = (ends[-1] // tile).astype(jnp.int32).reshape(1)
    return tile_expert, tile_rows, slot0, slot1, nused


def _router_kernel(h_ref, g_ref, wr_ref, br_ref, m_ref, comb_ref, *, hp):
    m = _rms_rows(h_ref[...], g_ref[...]).astype(m_ref.dtype)
    m_ref[...] = m
    logits = _mm(m, wr_ref[...], hp) + br_ref[...]
    rows = logits.shape[0]
    lane = lax.broadcasted_iota(jnp.int32, (rows, ROUTER_COLS), 1)
    big = jnp.int32(ROUTER_COLS)
    is_grp = (lane >= N_EXPERTS) & (lane < N_EXPERTS + MOE_GROUPS)
    gl = jnp.where(is_grp, logits, NEG_INF)
    gmax = jnp.max(gl, axis=-1, keepdims=True)
    gidx = jnp.min(jnp.where(gl == gmax, lane - N_EXPERTS, big), axis=-1, keepdims=True)
    gate = 1.0 / jnp.sum(jnp.where(is_grp, jnp.exp(gl - gmax), 0.0), axis=-1, keepdims=True)
    in_grp = (lane < N_EXPERTS) & ((lane >> 2) == gidx)
    el = jnp.where(in_grp, logits, NEG_INF)
    t1 = jnp.max(el, axis=-1, keepdims=True)
    i1 = jnp.min(jnp.where(el == t1, lane, big), axis=-1, keepdims=True)
    el2 = jnp.where(lane == i1, NEG_INF, el)
    t2 = jnp.max(el2, axis=-1, keepdims=True)
    i2 = jnp.min(jnp.where(el2 == t2, lane, big), axis=-1, keepdims=True)
    e2 = jnp.exp(t2 - t1)
    w1 = gate / (1.0 + e2)
    w2 = gate * e2 / (1.0 + e2)
    comb_ref[...] = jnp.where(lane == i1, w1, 0.0) + jnp.where(lane == i2, w2, 0.0)


def _router(h, g_all, wr_all, br_all, layer, tm, hp):
    rows = h.shape[0]
    return pl.pallas_call(
        functools.partial(_router_kernel, hp=hp),
        grid=(rows // tm,),
        in_specs=[
            pl.BlockSpec((tm, D_MODEL), lambda i: (i, 0)),
            pl.BlockSpec((None, 1, D_MODEL), lambda i: (layer, 0, 0)),
            pl.BlockSpec((None, D_MODEL, ROUTER_COLS), lambda i: (layer, 0, 0)),
            pl.BlockSpec((None, 1, ROUTER_COLS), lambda i: (layer, 0, 0)),
        ],
        out_specs=[pl.BlockSpec((tm, D_MODEL), lambda i: (i, 0)),
                   pl.BlockSpec((tm, ROUTER_COLS), lambda i: (i, 0))],
        out_shape=[jax.ShapeDtypeStruct((rows, D_MODEL), _act_dtype(hp)),
                   jax.ShapeDtypeStruct((rows, ROUTER_COLS), F32)],
        compiler_params=_cparams(("arbitrary",)),
        name="router",
    )(h, g_all, wr_all, br_all)


def _moe_kernel(h_ref, m_ref, comb_ref, wg_ref, wu_ref, wd_ref, o_ref, *, hp):
    e = pl.program_id(1)

    @pl.when(e == 0)
    def _():
        o_ref[...] = h_ref[...]

    x = m_ref[...]
    gate = _mm(x, wg_ref[...], hp)
    up = _mm(x, wu_ref[...], hp)
    lane = lax.broadcasted_iota(jnp.int32, comb_ref.shape, 1)
    w = jnp.sum(jnp.where(lane == e, comb_ref[...], 0.0), axis=-1, keepdims=True)
    o_ref[...] += _mm(_silu(gate) * up * w, wd_ref[...], hp)


def _moe_dense(h, m, comb, wg_all, wu_all, wd_all, layer, tm, hp):
    rows = h.shape[0]
    return pl.pallas_call(
        functools.partial(_moe_kernel, hp=hp),
        grid=(rows // tm, N_EXPERTS),
        in_specs=[
            pl.BlockSpec((tm, D_MODEL), lambda i, e: (i, 0)),
            pl.BlockSpec((tm, D_MODEL), lambda i, e: (i, 0)),
            pl.BlockSpec((tm, ROUTER_COLS), lambda i, e: (i, 0)),
            pl.BlockSpec((None, None, D_MODEL, EXPERT_FF), lambda i, e: (layer, e, 0, 0)),
            pl.BlockSpec((None, None, D_MODEL, EXPERT_FF), lambda i, e: (layer, e, 0, 0)),
            pl.BlockSpec((None, None, EXPERT_FF, D_MODEL), lambda i, e: (layer, e, 0, 0)),
        ],
        out_specs=pl.BlockSpec((tm, D_MODEL), lambda i, e: (i, 0)),
        out_shape=jax.ShapeDtypeStruct((rows, D_MODEL), F32),
        compiler_params=_cparams(("arbitrary", "arbitrary")),
        name="moe_dense",
    )(h, m, comb, wg_all, wu_all, wd_all)


def _ple_kernel(h_ref, hc_ref, g_ref, p_ref, wg_ref, wp_ref, o_ref, xn_ref, *, hp):
    @pl.when(pl.program_id(1) == 0)
    def _():
        xn_ref[...] = _rms_rows(h_ref[...], g_ref[...]).astype(xn_ref.dtype)

    gate = _sigmoid(_mm(xn_ref[...], wg_ref[...], hp))
    o_ref[...] = hc_ref[...] + gate * _mm(p_ref[...], wp_ref[...], hp)


def _ple(h, p_all, g_all, wg_all, wp_all, layer, tm, tn, hp):
    rows = h.shape[0]
    return pl.pallas_call(
        functools.partial(_ple_kernel, hp=hp),
        grid=(rows // tm, D_MODEL // tn),
        in_specs=[
            pl.BlockSpec((tm, D_MODEL), lambda i, j: (i, 0)),
            pl.BlockSpec((tm, tn), lambda i, j: (i, j)),
            pl.BlockSpec((None, 1, D_MODEL), lambda i, j: (layer, 0, 0)),
            pl.BlockSpec((None, tm, PLE_DIM), lambda i, j: (layer, i, 0)),
            pl.BlockSpec((None, D_MODEL, tn), lambda i, j: (layer, 0, j)),
            pl.BlockSpec((None, PLE_DIM, tn), lambda i, j: (layer, 0, j)),
        ],
        out_specs=pl.BlockSpec((tm, tn), lambda i, j: (i, j)),
        out_shape=jax.ShapeDtypeStruct((rows, D_MODEL), F32),
        scratch_shapes=[pltpu.VMEM((tm, D_MODEL), _act_dtype(hp))],
        compiler_params=_cparams(("arbitrary", "arbitrary")),
        name="ple",
    )(h, h, g_all, p_all, wg_all, wp_all)


def _token_tail_hp(h, mix, od, p_all, layer, tw):
    (w_out, w_out_d, g_ffn, w_router, b_router, w_gate, w_up, w_down, g_ple, w_ple_gate, w_ple_proj) = tw
    tm = h.shape[0]
    h = _outproj(h, mix, od, w_out, w_out_d, layer, tm, 512, True)
    m, comb = _router(h, g_ffn, w_router, b_router, layer, tm, True)
    h = _moe_dense(h, m, comb, w_gate, w_up, w_down, layer, tm, True)
    return _ple(h, p_all, g_ple, w_ple_gate, w_ple_proj, layer, tm, 512, True)


POST_TM = 128
W_CHUNK = 256
Z_PARTS = 1
Z_CHUNK = 1280
POST_VMEM_LIMIT = 60 * 1024 * 1024


def _stream_cast(w_hbm, layer, dst, stage, sem, col0=0):
    k_rows = dst.shape[0]
    n_chunks = dst.shape[1] // W_CHUNK

    def chunk_copy(c):
        return pltpu.make_async_copy(w_hbm.at[layer, pl.ds(0, k_rows), pl.ds(col0 + c * W_CHUNK, W_CHUNK)],
                                     stage.at[c % 2, pl.ds(0, k_rows), :], sem.at[c % 2])

    chunk_copy(0).start()
    for c in range(n_chunks):
        if c + 1 < n_chunks:
            chunk_copy(c + 1).start()
        chunk_copy(c).wait()
        dst[:, c * W_CHUNK:(c + 1) * W_CHUNK] = stage[c % 2, 0:k_rows, :].astype(BF16)


def _post_kernel(*refs, layer, with_inproj):
    if with_inproj:
        (h_ref, y0_ref, y1_ref, meta_ref, p_ref, gp_ref, wpp_ref, wpg_hbm, gm_ref, win_hbm,
         h_out, z_out, wpg_b, stage, sem, win_b, a_scr) = refs
    else:
        (h_ref, y0_ref, y1_ref, meta_ref, p_ref, gp_ref, wpp_ref, wpg_hbm,
         h_out, wpg_b, stage, sem) = refs
    part = pl.program_id(1)
    part_cols = IN_COLS // Z_PARTS

    @pl.when(jnp.logical_and(pl.program_id(0) == 0, part == 0))
    def _():
        _stream_cast(wpg_hbm, layer, wpg_b, stage, sem)
        if with_inproj:
            for k in range(Z_PARTS):
                _stream_cast(win_hbm, layer + 1, win_b.at[k], stage, sem, col0=k * part_cols)

    @pl.when(part == 0)
    def _():
        meta = meta_ref[...]
        lane = lax.broadcasted_iota(jnp.int32, meta.shape, 1)
        w1 = jnp.sum(jnp.where(lane == META_W1, meta, 0.0), axis=-1, keepdims=True)
        w2 = jnp.sum(jnp.where(lane == META_W2, meta, 0.0), axis=-1, keepdims=True)
        hn = h_ref[...] + w1 * y0_ref[...] + w2 * y1_ref[...]
        xn = _rms_rows(hn, gp_ref[...]).astype(BF16)
        gate = _sigmoid(jnp.dot(xn, wpg_b[...], preferred_element_type=F32))
        h3 = hn + gate * _mm(p_ref[...], wpp_ref[...], False)
        h_out[...] = h3
        if with_inproj:
            a_scr[...] = _rms_rows(h3, gm_ref[...]).astype(BF16)

    if with_inproj:
        for c0 in range(0, part_cols, Z_CHUNK):
            z_out[:, c0:c0 + Z_CHUNK] = jnp.dot(a_scr[...], win_b[part, :, c0:c0 + Z_CHUNK],
                                                preferred_element_type=F32)


def _post(h, yk, meta, p_all, g_ple, w_ple_gate, w_ple_proj, g_mix, w_in, layer):
    rows = h.shape[0]
    tm = POST_TM
    with_inproj = layer + 1 < DEPTH
    row_spec = pl.BlockSpec((tm, D_MODEL), lambda i, j: (i, 0))
    in_specs = [
        row_spec,
        pl.BlockSpec((None, tm, D_MODEL), lambda i, j: (0, i, 0)),
        pl.BlockSpec((None, tm, D_MODEL), lambda i, j: (1, i, 0)),
        pl.BlockSpec((tm, ROUTER_COLS), lambda i, j: (i, 0)),
        pl.BlockSpec((None, tm, PLE_DIM), lambda i, j: (layer, i, 0)),
        pl.BlockSpec((None, 1, D_MODEL), lambda i, j: (layer, 0, 0)),
        pl.BlockSpec((None, PLE_DIM, D_MODEL), lambda i, j: (layer, 0, 0)),
        pl.BlockSpec(memory_space=pl.ANY),
    ]
    args = [h, yk, yk, meta, p_all, g_ple, w_ple_proj, w_ple_gate]
    out_specs = [row_spec]
    out_shape = [jax.ShapeDtypeStruct((rows, D_MODEL), F32)]
    scratch = [pltpu.VMEM((D_MODEL, D_MODEL), BF16),
               pltpu.VMEM((2, D_MODEL, W_CHUNK), F32),
               pltpu.SemaphoreType.DMA((2,))]
    if with_inproj:
        part_cols = IN_COLS // Z_PARTS
        in_specs += [pl.BlockSpec((None, 1, D_MODEL), lambda i, j: (layer + 1, 0, 0)),
                     pl.BlockSpec(memory_space=pl.ANY)]
        args += [g_mix, w_in]
        out_specs.append(pl.BlockSpec((tm, part_cols), lambda i, j: (i, j)))
        out_shape.append(jax.ShapeDtypeStruct((rows, IN_COLS), F32))
        scratch += [pltpu.VMEM((Z_PARTS, D_MODEL, part_cols), BF16), pltpu.VMEM((tm, D_MODEL), BF16)]
    return pl.pallas_call(
        functools.partial(_post_kernel, layer=layer, with_inproj=with_inproj),
        grid=(rows // tm, Z_PARTS if with_inproj else 1),
        in_specs=in_specs,
        out_specs=out_specs,
        out_shape=out_shape,
        scratch_shapes=scratch,
        compiler_params=pltpu.CompilerParams(dimension_semantics=("arbitrary", "arbitrary"),
                                             vmem_limit_bytes=POST_VMEM_LIMIT),
        name="post",
    )(*args)


def _token_tail_prompt(h, mix, od, p_all, layer, tw, g_mix, w_in):
    (w_out, w_out_d, g_ffn, w_router, b_router, w_gate, w_up, w_down, g_ple, w_ple_gate, w_ple_proj) = tw
    rows = h.shape[0]
    n_tiles = (2 * rows + N_EXPERTS * (EXPERT_TILE - 1) + EXPERT_TILE - 1) // EXPERT_TILE
    h, m, meta, meta_t, cnt = _outrouter(h, mix, od, g_ffn, w_router, b_router, w_out, w_out_d, layer)
    plan = _expert_plan(meta_t, cnt, n_tiles)
    yk = _experts_sparse(m, *plan, w_gate, w_up, w_down, layer).reshape(2, rows + 2 * SUBLANE, D_MODEL)
    out = _post(h, yk, meta, p_all, g_ple, w_ple_gate, w_ple_proj, g_mix, w_in, layer)
    return (out[0], out[1]) if layer + 1 < DEPTH else (out[0], None)


TAIL = 8


def _patch_kernel(x_ref, tail_ref, o_ref):
    del x_ref
    o_ref[...] = tail_ref[...]


def _patch_tail(x, tail):
    n, s, width = x.shape
    return pl.pallas_call(
        _patch_kernel,
        grid=(n,),
        in_specs=[pl.BlockSpec(memory_space=pl.ANY),
                  pl.BlockSpec((1, TAIL, width), lambda b: (b, 0, 0))],
        out_specs=pl.BlockSpec((1, TAIL, width), lambda b: (b, s // TAIL - 1, 0)),
        out_shape=jax.ShapeDtypeStruct(x.shape, x.dtype),
        input_output_aliases={0: 0},
        compiler_params=_cparams(("arbitrary",)),
        name="patch_tail",
    )(x, tail)


def kernel(x_prompt, x_sample, p_prompt, p_sample, state_conv_a, state_conv_c, cache_kv_w128, cache_kv_w512, cache_kv_w2048, g_mix, w_in, conv_a_w, conv_a_b, ln_a_g, ln_a_b, ln_b_g, ln_b_b, sgu_w, sgu_b, conv_c_w, g_q, g_k, w_out, g_ffn, w_router_grp, b_router_grp, w_router_exp, b_router_exp, w_gate, w_up, w_down, g_ple, w_ple_gate, w_ple_proj):
    n_p, s_p, _ = x_prompt.shape
    n_s, t_s, _ = x_sample.shape
    rows_p = n_p * s_p
    rows_s = n_s * t_s

    def row3(a):
        return a.reshape(DEPTH, 1, a.shape[-1])

    g_mix3, g_ffn3, g_ple3 = row3(g_mix), row3(g_ffn), row3(g_ple)
    cab3, lag3, lab3, lbg3, lbb3 = row3(conv_a_b), row3(ln_a_g), row3(ln_a_b), row3(ln_b_g), row3(ln_b_b)
    gq3, gk3 = row3(g_q), row3(g_k)
    sb_rep = jnp.repeat(jnp.swapaxes(sgu_b, 1, 2), HEAD_DIM, axis=2)
    sw8 = jnp.repeat(jnp.transpose(sgu_w[:, :, :t_s, :t_s], (0, 3, 2, 1)), HEAD_DIM, axis=3)
    w_out_d = jnp.pad(w_out[:, ABC_PAD - 64:].reshape(DEPTH, 3, GROUP_WIDTH, D_MODEL),
                      ((0, 0), (0, 0), (0, OD_PAD // 3 - GROUP_WIDTH), (0, 0))).reshape(DEPTH, OD_PAD, D_MODEL)
    w_router = jnp.concatenate(
        [jnp.transpose(w_router_exp, (0, 2, 1, 3)).reshape(DEPTH, D_MODEL, N_EXPERTS), w_router_grp,
         jnp.zeros((DEPTH, D_MODEL, ROUTER_COLS - N_EXPERTS - MOE_GROUPS), F32)], axis=2)
    b_router = jnp.concatenate(
        [b_router_exp.reshape(DEPTH, N_EXPERTS), b_router_grp,
         jnp.zeros((DEPTH, ROUTER_COLS - N_EXPERTS - MOE_GROUPS), F32)], axis=1).reshape(DEPTH, 1, ROUTER_COLS)
    tail_w = (w_out, w_out_d, g_ffn3, w_router, b_router, w_gate, w_up, w_down, g_ple3, w_ple_gate, w_ple_proj)
    mix_w_p = (conv_a_w, cab3, lag3, lab3, lbg3, lbb3, sgu_w, sb_rep, conv_c_w, gq3, gk3)
    mix_w_s = (conv_a_w, cab3, lag3, lab3, lbg3, lbb3, sw8, sb_rep, conv_c_w, gq3, gk3)

    caches = [c.reshape(c.shape[0], c.shape[1], c.shape[2], 2 * GROUP_WIDTH)
              for c in (cache_kv_w128, cache_kv_w512, cache_kv_w2048)]
    p_p = p_prompt.reshape(DEPTH, rows_p, PLE_DIM)
    rows_t = n_p * TAIL
    rows_h = rows_s + rows_t
    p_h = jnp.concatenate([p_sample.reshape(DEPTH, rows_s, PLE_DIM),
                           p_prompt[:, :, s_p - TAIL:].reshape(DEPTH, rows_t, PLE_DIM)], axis=1)
    h_h = jnp.concatenate([x_sample.reshape(rows_s, D_MODEL),
                           x_prompt[:, s_p - TAIL:].reshape(rows_t, D_MODEL)], axis=0)

    h = x_prompt.reshape(rows_p, D_MODEL)
    st_a, st_c, st_kv = [], [], [[], [], []]
    sa, sc, sv, skv = [], [], [], [[], [], []]
    z_next = _inproj(h, g_mix3, w_in, 0, 1024, 1024)
    for i in range(DEPTH):
        z_h = _inproj(h_h, g_mix3, w_in, i, rows_h, 512, hp=True)
        z = _patch_tail(z_next.reshape(n_p, s_p, IN_COLS), z_h[rows_s:].reshape(n_p, TAIL, IN_COLS))
        (mix, q1, kv1, q4, kv4, q16, kv16, sta, stc, s1, s4, s16) = _mixers_prompt(z, i, mix_w_p)
        o1, l1 = _attn_prompt(q1, kv1)
        o4, l4 = _attn_prompt(q4, kv4)
        o16, l16 = _attn_prompt(q16, kv16)
        od = _combine_prompt(o1, l1, o4, l4, o16, l16)
        (mix_s, od_s, na, nc, cv, n1, n4, n16) = _mixers_sample(
            z_h[:rows_s].reshape(n_s, t_s, IN_COLS), i, state_conv_a, state_conv_c, caches, mix_w_s)
        mix_h = jnp.concatenate([mix_s.reshape(rows_s, ABC_PAD),
                                 mix[:, s_p - TAIL:].reshape(rows_t, ABC_PAD).astype(F32)], axis=0)
        od_h = jnp.concatenate([od_s.reshape(rows_s, OD_PAD),
                                od[:, s_p - TAIL:].reshape(rows_t, OD_PAD).astype(F32)], axis=0)
        h, z_next = _token_tail_prompt(h, mix.reshape(rows_p, ABC_PAD), od.reshape(rows_p, OD_PAD), p_p, i,
                                       tail_w, g_mix3, w_in)
        h_h = _token_tail_hp(h_h, mix_h, od_h, p_h, i, tail_w)
        st_a.append(sta[:, A_HALO - (A_CONV_LEN - 1):])
        st_c.append(stc[:, C_HALO - (C_CONV_LEN - 1):])
        for g, s_kv in enumerate((s1, s4, s16)):
            st_kv[g].append(s_kv.reshape(n_p, s_kv.shape[1], 2, HEADS_PER_GROUP, HEAD_DIM))
        sa.append(na)
        sc.append(nc)
        sv.append(cv)
        for g, nk in enumerate((n1, n4, n16)):
            skv[g].append(nk.reshape(n_s, nk.shape[1], 2, HEADS_PER_GROUP, HEAD_DIM))
    y_prompt = _patch_tail(h.reshape(n_p, s_p, D_MODEL), h_h[rows_s:].reshape(n_p, TAIL, D_MODEL))
    conv_a_prompt = jnp.stack(st_a)
    conv_c_prompt = jnp.stack(st_c)
    kv_prompt = [jnp.stack(s) for s in st_kv]
    y_sample = h_h[:rows_s].reshape(n_s, t_s, D_MODEL)
    conv_a_sample = jnp.stack(sa)
    conv_c_sample = jnp.stack(sc)
    chunk_v_sample = jnp.stack(sv)
    kv_sample = [jnp.stack(s) for s in skv]

    return (y_prompt, y_sample, conv_a_prompt, conv_a_sample, conv_c_prompt, conv_c_sample, chunk_v_sample,
            kv_prompt[0], kv_sample[0], kv_prompt[1], kv_sample[1], kv_prompt[2], kv_sample[2])
```
